```python
import math
import jax, jax.numpy as jnp
from jax import lax
import numpy as np

D_MODEL = 1024
BATCH = 4
SEQ = 4096
DEPTH = 1
DEC_BATCH = 32
DEC_SEQ = 64
PAST_LEN = 4096

CHUNK = 64
NORM_EPS = 1e-6
S5_WIDTH = D_MODEL // 2
S5_GROUP = 16
S5_GROUPS = S5_WIDTH // S5_GROUP
S5_STATE = 64
S5_DT_MIN = 1e-3
S5_DT_MAX = 1e-1
S5_EIG_CLIP = -1e-4
HG_WIDTH = D_MODEL // 2
HG_HEADS = 4
HG_DK = HG_WIDTH // HG_HEADS
HG_DV = HG_WIDTH // HG_HEADS
IN_COLS = S5_WIDTH + 4 * HG_WIDTH + 2 * D_MODEL
N_EXPERTS = 32
TOP_K = 4
D_EXPERT = D_MODEL
SWIGLU_ALPHA = 1.702
SWIGLU_LIMIT = 7.0
EXPERT_BLOCK = 128

kernel_name = 'hybrid_s5_hgrn2_moe_stream_step'


def _rmsnorm(x, g):
    xf = x.astype(jnp.float32)
    r = lax.rsqrt(jnp.mean(xf * xf, axis=-1, keepdims=True) + NORM_EPS)
    return (xf * r * g.astype(jnp.float32)).astype(x.dtype)


def _s5_discretize(lam_re, lam_im, log_dt, b_re, b_im):
    lr = jnp.minimum(lam_re.astype(jnp.float32), S5_EIG_CLIP)
    li = lam_im.astype(jnp.float32)
    dt = jnp.exp(log_dt.astype(jnp.float32))[:, None]
    mag = jnp.exp(lr * dt)
    ar = mag * jnp.cos(li * dt)
    ai = mag * jnp.sin(li * dt)
    den = lr * lr + li * li
    zr = ((ar - 1.0) * lr + ai * li) / den
    zi = (ai * lr - (ar - 1.0) * li) / den
    br = b_re.astype(jnp.float32)
    bi = b_im.astype(jnp.float32)
    bbr = zr[..., None] * br - zi[..., None] * bi
    bbi = zr[..., None] * bi + zi[..., None] * br
    return ar, ai, bbr, bbi


def _cmul_combine(e1, e2):
    a1r, a1i, b1r, b1i = e1
    a2r, a2i, b2r, b2i = e2
    return (a1r * a2r - a1i * a2i,
            a1r * a2i + a1i * a2r,
            a2r * b1r - a2i * b1i + b2r,
            a2r * b1i + a2i * b1r + b2i)


def _s5(u, s0_re, s0_im, lam_re, lam_im, log_dt, b_re, b_im, c_re, c_im, d_skip):
    n, l, _ = u.shape
    ug = u.astype(jnp.float32).reshape(n, l, S5_GROUPS, S5_GROUP)
    ar, ai, bbr, bbi = _s5_discretize(lam_re, lam_im, log_dt, b_re, b_im)
    bu_re = jnp.einsum('nlgc,gpc->nlgp', ug, bbr)
    bu_im = jnp.einsum('nlgc,gpc->nlgp', ug, bbi)
    s0r = s0_re.astype(jnp.float32)
    s0i = s0_im.astype(jnp.float32)
    bu_re = bu_re.at[:, 0].add(ar * s0r - ai * s0i)
    bu_im = bu_im.at[:, 0].add(ar * s0i + ai * s0r)
    a_re = jnp.broadcast_to(ar, bu_re.shape)
    a_im = jnp.broadcast_to(ai, bu_im.shape)
    _, _, s_re, s_im = lax.associative_scan(_cmul_combine, (a_re, a_im, bu_re, bu_im), axis=1)
    y = (jnp.einsum('nlgp,gcp->nlgc', s_re, c_re.astype(jnp.float32))
         - jnp.einsum('nlgp,gcp->nlgc', s_im, c_im.astype(jnp.float32))
         + d_skip.astype(jnp.float32).reshape(S5_GROUPS, S5_GROUP) * ug)
    return y.reshape(n, l, S5_WIDTH), s_re[:, -1], s_im[:, -1]


def _gla_chunk(s, inputs):
    q, k, v, g = inputs
    c = q.shape[2]
    gc = jnp.cumsum(g, axis=2)
    causal = jnp.tril(jnp.ones((c, c), dtype=bool))
    diff = gc[:, :, :, None, :] - gc[:, :, None, :, :]
    decay = jnp.exp(jnp.where(causal[:, :, None], diff, -jnp.inf))
    att = jnp.einsum('nhtd,nhsd,nhtsd->nhts', q, k, decay)
    o = (jnp.einsum('nhts,nhsv->nhtv', att, v)
         + jnp.einsum('nhtd,nhdv->nhtv', q * jnp.exp(gc), s))
    g_last = gc[:, :, -1]
    s_new = (jnp.exp(g_last)[..., None] * s
             + jnp.einsum('nhsd,nhsv->nhdv', k * jnp.exp(g_last[:, :, None] - gc), v))
    return s_new, o


def _hgrn2(q, k, v, logf, s0):
    n, l = q.shape[:2]
    nc = -(-l // CHUNK)
    pad = nc * CHUNK - l

    def blocks(a):
        a = jnp.pad(a.astype(jnp.float32), ((0, 0), (0, pad), (0, 0), (0, 0)))
        return a.reshape(n, nc, CHUNK, HG_HEADS, a.shape[-1]).transpose(1, 0, 3, 2, 4)

    s_fin, o = lax.scan(_gla_chunk, s0.astype(jnp.float32),
                        (blocks(q), blocks(k), blocks(v), blocks(logf)))
    o = o.transpose(1, 0, 3, 2, 4).reshape(n, nc * CHUNK, HG_HEADS, HG_DV)[:, :l]
    return o, s_fin


def _moe(h, router_w, router_b, w1, b1, w2, b2):
    t, d = h.shape
    n = t * TOP_K
    logits = h.astype(jnp.float32) @ router_w.astype(jnp.float32) + router_b.astype(jnp.float32)
    top_v, top_i = lax.top_k(logits, TOP_K)
    gates = jax.nn.softmax(top_v, axis=-1).astype(h.dtype)
    flat_e = top_i.reshape(-1)
    flat_tok = jnp.arange(n, dtype=jnp.int32) // TOP_K
    order = jnp.argsort(flat_e)
    e_sorted = flat_e[order]
    tok_sorted = flat_tok[order]
    gate_sorted = gates.reshape(-1)[order]
    counts = jnp.bincount(flat_e, length=N_EXPERTS)
    padded = (counts + EXPERT_BLOCK - 1) // EXPERT_BLOCK * EXPERT_BLOCK
    pad_end = jnp.cumsum(padded)
    pad_start = pad_end - padded
    start = jnp.cumsum(counts) - counts
    dest = pad_start[e_sorted] + jnp.arange(n, dtype=jnp.int32) - start[e_sorted]
    n_blocks = -(-n // EXPERT_BLOCK) + N_EXPERTS
    rows = n_blocks * EXPERT_BLOCK
    row_tok = jnp.full((rows,), t, dtype=jnp.int32).at[dest].set(tok_sorted)
    blk_e = jnp.minimum(
        jnp.searchsorted(pad_end, jnp.arange(n_blocks, dtype=jnp.int32) * EXPERT_BLOCK, side='right'),
        N_EXPERTS - 1)
    h_pad = jnp.concatenate([h, jnp.zeros((1, d), h.dtype)], axis=0)
    xb = h_pad[row_tok].reshape(n_blocks, EXPERT_BLOCK, d)

    def expert_block(args):
        xe, e = args
        u = xe @ w1[e] + b1[e]
        a, lin = jnp.split(u, 2, axis=-1)
        a = jnp.minimum(a, SWIGLU_LIMIT)
        lin = jnp.clip(lin, -SWIGLU_LIMIT, SWIGLU_LIMIT)
        return (a * jax.nn.sigmoid(SWIGLU_ALPHA * a) * (lin + 1.0)) @ w2[e] + b2[e]

    yb = lax.map(expert_block, (xb, blk_e)).reshape(rows, d)
    y = yb[dest] * gate_sorted[:, None]
    return jax.ops.segment_sum(y, tok_sorted, num_segments=t)


def _layer(x, s5_re0, s5_im0, hg0, lb, g_mix, w_in, lam_re, lam_im, log_dt, b_re, b_im,
           c_re, c_im, d_skip, w_glu, w_gate, hg_norm_g, hg_w_out, w_out, g_ffn,
           router_w, router_b, w1, b1, w2, b2):
    n, l, _ = x.shape
    h = _rmsnorm(x, g_mix)
    z = h @ w_in
    cuts = [S5_WIDTH, S5_WIDTH + HG_WIDTH, S5_WIDTH + 2 * HG_WIDTH, S5_WIDTH + 3 * HG_WIDTH,
            S5_WIDTH + 4 * HG_WIDTH, S5_WIDTH + 4 * HG_WIDTH + D_MODEL]
    u_a, zq, zf, zi, zg, ga, gb = jnp.split(z, cuts, axis=-1)
    y_s5, s5_re, s5_im = _s5(u_a, s5_re0, s5_im0, lam_re, lam_im, log_dt, b_re, b_im, c_re, c_im, d_skip)
    y_s5 = jax.nn.gelu(y_s5).astype(x.dtype)
    y_a = (y_s5 @ w_glu) * jax.nn.sigmoid(y_s5 @ w_gate)
    q = jax.nn.silu(zq).reshape(n, l, HG_HEADS, HG_DK)
    f = lb + (1.0 - lb) * jax.nn.sigmoid(zf.astype(jnp.float32))
    k = (1.0 - f).reshape(n, l, HG_HEADS, HG_DK)
    logf = jnp.log(f).reshape(n, l, HG_HEADS, HG_DK)
    v = zi.reshape(n, l, HG_HEADS, HG_DV)
    o, hg = _hgrn2(q, k, v, logf, hg0)
    o = _rmsnorm(o.astype(x.dtype), hg_norm_g) * jax.nn.silu(zg.reshape(n, l, HG_HEADS, HG_DV))
    y_b = o.reshape(n, l, HG_WIDTH) @ hg_w_out
    mixed = jax.nn.sigmoid(ga) * y_a + jax.nn.sigmoid(gb) * y_b
    x = x + mixed @ w_out
    h2 = _rmsnorm(x, g_ffn)
    x = x + _moe(h2.reshape(n * l, D_MODEL), router_w, router_b, w1, b1, w2, b2).reshape(n, l, D_MODEL)
    return x, s5_re, s5_im, hg


def setup_inputs(seed: int = 0) -> dict:
    key = jax.random.key(seed)
    ks = jax.random.split(key, 32)
    f32 = jnp.float32

    def nrm(k, shape, scale):
        return scale * jax.random.normal(k, shape, f32)

    G, P, GS = S5_GROUPS, S5_STATE, S5_GROUP
    return {
        'x_prompt': nrm(ks[0], (BATCH, SEQ, D_MODEL), 1.0),
        'x_sample': nrm(ks[1], (DEC_BATCH, DEC_SEQ, D_MODEL), 1.0),
        'state_s5_re': nrm(ks[2], (DEPTH, DEC_BATCH, G, P), 0.5),
        'state_s5_im': nrm(ks[3], (DEPTH, DEC_BATCH, G, P), 0.5),
        'state_hgrn': nrm(ks[4], (DEPTH, DEC_BATCH, HG_HEADS, HG_DK, HG_DV), 0.5),
        'g_mix': 1.0 + nrm(ks[5], (DEPTH, D_MODEL), 0.01),
        'w_in': nrm(ks[6], (DEPTH, D_MODEL, IN_COLS), D_MODEL ** -0.5),
        's5_lambda_re': -0.5 + nrm(ks[7], (DEPTH, G, P), 0.01),
        's5_lambda_im': math.pi * jnp.arange(P, dtype=f32)[None, None, :] + nrm(ks[8], (DEPTH, G, P), 0.01),
        's5_log_dt': jax.random.uniform(ks[9], (DEPTH, G), f32, math.log(S5_DT_MIN), math.log(S5_DT_MAX)),
        's5_b_re': nrm(ks[10], (DEPTH, G, P, GS), (2 * GS) ** -0.5),
        's5_b_im': nrm(ks[11], (DEPTH, G, P, GS), (2 * GS) ** -0.5),
        's5_c_re': nrm(ks[12], (DEPTH, G, GS, P), (2 * P) ** -0.5),
        's5_c_im': nrm(ks[13], (DEPTH, G, GS, P), (2 * P) ** -0.5),
        's5_d': nrm(ks[14], (DEPTH, S5_WIDTH), 1.0),
        's5_w_glu': nrm(ks[15], (DEPTH, S5_WIDTH, D_MODEL), S5_WIDTH ** -0.5),
        's5_w_gate': nrm(ks[16], (DEPTH, S5_WIDTH, D_MODEL), S5_WIDTH ** -0.5),
        'hgrn_lower_bound': nrm(ks[17], (DEPTH + 1, HG_WIDTH), 0.1),
        'hgrn_norm_g': 1.0 + nrm(ks[18], (DEPTH, HG_DV), 0.01),
        'hgrn_w_out': nrm(ks[19], (DEPTH, HG_WIDTH, D_MODEL), HG_WIDTH ** -0.5),
        'w_out': nrm(ks[20], (DEPTH, D_MODEL, D_MODEL), D_MODEL ** -0.5),
        'g_ffn': 1.0 + nrm(ks[21], (DEPTH, D_MODEL), 0.01),
        'router_w': nrm(ks[22], (DEPTH, D_MODEL, N_EXPERTS), D_MODEL ** -0.5),
        'router_b': nrm(ks[23], (DEPTH, N_EXPERTS), 0.01),
        'moe_w1': nrm(ks[24], (DEPTH, N_EXPERTS, D_MODEL, 2 * D_EXPERT), D_MODEL ** -0.5),
        'moe_b1': nrm(ks[25], (DEPTH, N_EXPERTS, 2 * D_EXPERT), 0.01),
        'moe_w2': nrm(ks[26], (DEPTH, N_EXPERTS, D_EXPERT, D_MODEL), D_EXPERT ** -0.5),
        'moe_b2': nrm(ks[27], (DEPTH, N_EXPERTS, D_MODEL), 0.01),
        'g_final': 1.0 + nrm(ks[28], (D_MODEL,), 0.01),
    }


def reference(x_prompt, x_sample, state_s5_re, state_s5_im, state_hgrn, g_mix, w_in,
              s5_lambda_re, s5_lambda_im, s5_log_dt, s5_b_re, s5_b_im, s5_c_re, s5_c_im, s5_d,
              s5_w_glu, s5_w_gate, hgrn_lower_bound, hgrn_norm_g, hgrn_w_out, w_out, g_ffn,
              router_w, router_b, moe_w1, moe_b1, moe_w2, moe_b2, g_final):
    lb_all = jnp.cumsum(jax.nn.softmax(hgrn_lower_bound.astype(jnp.float32), axis=0), axis=0)
    xp, xs = x_prompt, x_sample
    zero_s5 = jnp.zeros((BATCH, S5_GROUPS, S5_STATE), jnp.float32)
    zero_hg = jnp.zeros((BATCH, HG_HEADS, HG_DK, HG_DV), jnp.float32)
    p_re, p_im, p_hg, s_re, s_im, s_hg = [], [], [], [], [], []
    for li in range(DEPTH):
        w = (g_mix[li], w_in[li], s5_lambda_re[li], s5_lambda_im[li], s5_log_dt[li],
             s5_b_re[li], s5_b_im[li], s5_c_re[li], s5_c_im[li], s5_d[li], s5_w_glu[li],
             s5_w_gate[li], hgrn_norm_g[li], hgrn_w_out[li], w_out[li], g_ffn[li],
             router_w[li], router_b[li], moe_w1[li], moe_b1[li], moe_w2[li], moe_b2[li])
        xp, a, b, c = _layer(xp, zero_s5, zero_s5, zero_hg, lb_all[li], *w)
        p_re.append(a)
        p_im.append(b)
        p_hg.append(c)
        xs, a, b, c = _layer(xs, state_s5_re[li], state_s5_im[li], state_hgrn[li], lb_all[li], *w)
        s_re.append(a)
        s_im.append(b)
        s_hg.append(c)
    y_prompt = _rmsnorm(xp, g_final)
    y_sample = _rmsnorm(xs, g_final)
    s5_re_prompt = jnp.stack(p_re).astype(x_prompt.dtype)
    s5_im_prompt = jnp.stack(p_im).astype(x_prompt.dtype)
    hgrn_prompt = jnp.stack(p_hg).astype(x_prompt.dtype)
    s5_re_sample = jnp.stack(s_re).astype(state_s5_re.dtype)
    s5_im_sample = jnp.stack(s_im).astype(state_s5_im.dtype)
    hgrn_sample = jnp.stack(s_hg).astype(state_hgrn.dtype)
    return (y_prompt, y_sample, s5_re_prompt, s5_im_prompt, hgrn_prompt, s5_re_sample, s5_im_sample, hgrn_sample)
```

```python
import functools
import math

import jax
import jax.numpy as jnp
from jax import lax
from jax.experimental import pallas as pl
from jax.experimental.pallas import tpu as pltpu

F32 = jnp.float32
BF16 = jnp.bfloat16

D_MODEL = 1024
NORM_EPS = 1e-6
S5_WIDTH = 512
S5_GROUP = 16
S5_GROUPS = 32
S5_STATE = 64
S5_LANES = S5_GROUPS * S5_STATE
S5_DT_CLIP = -1e-4
HG_WIDTH = 512
HG_HEADS = 4
HG_DK = 128
N_EXPERTS = 32
TOP_K = 4
D_EXPERT = 1024
SWIGLU_ALPHA = 1.702
SWIGLU_LIMIT = 7.0

ROW_TILE = 256
SCAN_ROWS = 8
S5_CH_TILE = 128
S5_LANE_TILE = 512
EXPERT_ROWS = 256
COMBINE_ROWS = 128
VMEM_LIMIT = 56 * 1024 * 1024


def _cparams(n_axes):
    return pltpu.CompilerParams(dimension_semantics=("arbitrary",) * n_axes,
                                vmem_limit_bytes=VMEM_LIMIT)


def _rms(x, g):
    r = lax.rsqrt(jnp.mean(x * x, axis=-1, keepdims=True) + NORM_EPS)
    return x * r * g


def _dot(a, b):
    return jnp.dot(a, b, preferred_element_type=F32)


def _dot_nt(a, b):
    return lax.dot_general(a, b, (((1,), (1,)), ((), ())), preferred_element_type=F32)


def _dot_tn(a, b):
    return lax.dot_general(a, b, (((0,), (0,)), ((), ())), preferred_element_type=F32)


def _inproj_kernel(x_ref, g_ref, w_ref, u_ref, zh_ref, gab_ref):
    h = _rms(x_ref[...], g_ref[...]).astype(BF16)
    u_ref[...] = _dot(h, w_ref[:, :S5_WIDTH])
    zh_ref[...] = _dot(h, w_ref[:, S5_WIDTH:S5_WIDTH + 4 * HG_WIDTH])
    gab_ref[...] = _dot(h, w_ref[:, S5_WIDTH + 4 * HG_WIDTH:])


def _inproj(x, g_mix, w_in_bf):
    t = x.shape[0]
    ncols = w_in_bf.shape[1]
    row = lambda i: (i, 0)
    fixed = lambda i: (0, 0)
    return pl.pallas_call(
        _inproj_kernel,
        grid=(t // ROW_TILE,),
        in_specs=[pl.BlockSpec((ROW_TILE, D_MODEL), row),
                  pl.BlockSpec((1, D_MODEL), fixed),
                  pl.BlockSpec((D_MODEL, ncols), fixed)],
        out_specs=[pl.BlockSpec((ROW_TILE, S5_WIDTH), row),
                   pl.BlockSpec((ROW_TILE, 4 * HG_WIDTH), row),
                   pl.BlockSpec((ROW_TILE, 2 * D_MODEL), row)],
        out_shape=[jax.ShapeDtypeStruct((t, S5_WIDTH), F32),
                   jax.ShapeDtypeStruct((t, 4 * HG_WIDTH), F32),
                   jax.ShapeDtypeStruct((t, 2 * D_MODEL), F32)],
        compiler_params=_cparams(1),
        name="inproj",
    )(x, g_mix.reshape(1, D_MODEL), w_in_bf)


def _s5_kernel(u_ref, s0r_ref, s0i_ref, bmat_ref, cmat_ref, tab_ref, d_ref, wg_ref,
               ya_ref, sr_out, si_out, bur, bui, cre, cim, *, n_seg, seg_len):
    j = pl.program_id(1)

    @pl.when(j == 0)
    def _():
        cre[...] = s0r_ref[...]
        cim[...] = s0i_ref[...]

    u = u_ref[...]
    ub = u.astype(BF16)
    n_tiles = S5_WIDTH // S5_CH_TILE
    for n in range(n_tiles):
        bu = _dot(ub[:, n * S5_CH_TILE:(n + 1) * S5_CH_TILE], bmat_ref[n])
        bur[:, n * S5_LANE_TILE:(n + 1) * S5_LANE_TILE] = bu[:, :S5_LANE_TILE]
        bui[:, n * S5_LANE_TILE:(n + 1) * S5_LANE_TILE] = bu[:, S5_LANE_TILE:]

    for n in range(n_tiles):
        ls = slice(n * S5_LANE_TILE, (n + 1) * S5_LANE_TILE)
        for seg in range(n_seg):
            def body(r, carry, seg=seg, ls=ls):
                c_re, c_im = carry
                row = pl.multiple_of(seg * seg_len + r * SCAN_ROWS, SCAN_ROWS)
                xr = bur[pl.ds(row, SCAN_ROWS), ls]
                xi = bui[pl.ds(row, SCAN_ROWS), ls]
                for k in range(3):
                    rr = pltpu.roll(xr, 1 << k, axis=0)
                    ri = pltpu.roll(xi, 1 << k, axis=0)
                    a_re = tab_ref[k, :, ls]
                    a_im = tab_ref[3 + k, :, ls]
                    xr, xi = (xr + (a_re * rr - a_im * ri),
                              xi + (a_re * ri + a_im * rr))
                cbr = jnp.broadcast_to(c_re, xr.shape)
                cbi = jnp.broadcast_to(c_im, xi.shape)
                p_re = tab_ref[6, :, ls]
                p_im = tab_ref[7, :, ls]
                xr, xi = (xr + (p_re * cbr - p_im * cbi),
                          xi + (p_re * cbi + p_im * cbr))
                bur[pl.ds(row, SCAN_ROWS), ls] = xr
                bui[pl.ds(row, SCAN_ROWS), ls] = xi
                return xr[SCAN_ROWS - 1:, :], xi[SCAN_ROWS - 1:, :]

            c_re, c_im = lax.fori_loop(0, seg_len // SCAN_ROWS, body,
                                       (cre[seg, :, ls], cim[seg, :, ls]))
            cre[seg, :, ls] = c_re
            cim[seg, :, ls] = c_im

    sr_out[...] = cre[...]
    si_out[...] = cim[...]

    ys = []
    for n in range(n_tiles):
        ls = slice(n * S5_LANE_TILE, (n + 1) * S5_LANE_TILE)
        ys.append(_dot(bur[:, ls].astype(BF16), cmat_ref[n, :S5_LANE_TILE, :])
                  + _dot(bui[:, ls].astype(BF16), cmat_ref[n, S5_LANE_TILE:, :]))
    y = jnp.concatenate(ys, axis=1) + d_ref[...] * u
    y = jax.nn.gelu(y).astype(BF16)
    p = _dot(y, wg_ref[...])
    ya_ref[...] = p[:, :D_MODEL] * jax.nn.sigmoid(p[:, D_MODEL:])


def _s5(u, s0_re, s0_im, bmat, cmat, tabs, d_skip, wg, *, n_seq, seq_len, row_off):
    if seq_len >= ROW_TILE:
        n_seg, seg_len, chunks = 1, ROW_TILE, seq_len // ROW_TILE
    else:
        n_seg, seg_len, chunks = ROW_TILE // seq_len, seq_len, 1
    n_outer = n_seq // n_seg
    t = n_seq * seq_len
    row = lambda i, j: (row_off + i * chunks + j, 0)
    orow = lambda i, j: (i * chunks + j, 0)
    st = lambda i, j: (i, 0, 0)
    kern = functools.partial(_s5_kernel, n_seg=n_seg, seg_len=seg_len)
    return pl.pallas_call(
        kern,
        grid=(n_outer, chunks),
        in_specs=[pl.BlockSpec((ROW_TILE, S5_WIDTH), row),
                  pl.BlockSpec((n_seg, 1, S5_LANES), st),
                  pl.BlockSpec((n_seg, 1, S5_LANES), st),
                  pl.BlockSpec(bmat.shape, lambda i, j: (0, 0, 0)),
                  pl.BlockSpec(cmat.shape, lambda i, j: (0, 0, 0)),
                  pl.BlockSpec(tabs.shape, lambda i, j: (0, 0, 0)),
                  pl.BlockSpec((1, S5_WIDTH), lambda i, j: (0, 0)),
                  pl.BlockSpec(wg.shape, lambda i, j: (0, 0))],
        out_specs=[pl.BlockSpec((ROW_TILE, D_MODEL), orow),
                   pl.BlockSpec((n_seg, 1, S5_LANES), st),
                   pl.BlockSpec((n_seg, 1, S5_LANES), st)],
        out_shape=[jax.ShapeDtypeStruct((t, D_MODEL), F32),
                   jax.ShapeDtypeStruct((n_seq, 1, S5_LANES), F32),
                   jax.ShapeDtypeStruct((n_seq, 1, S5_LANES), F32)],
        scratch_shapes=[pltpu.VMEM((ROW_TILE, S5_LANES), F32),
                        pltpu.VMEM((ROW_TILE, S5_LANES), F32),
                        pltpu.VMEM((n_seg, 1, S5_LANES), F32),
                        pltpu.VMEM((n_seg, 1, S5_LANES), F32)],
        compiler_params=_cparams(2),
        name="s5",
    )(u, s0_re, s0_im, bmat, cmat, tabs, d_skip, wg)


def _s5_tables(lam_re, lam_im, log_dt, b_re, b_im, c_re, c_im):
    lr = jnp.minimum(lam_re.astype(F32), S5_DT_CLIP)
    li = lam_im.astype(F32)
    dt = jnp.exp(log_dt.astype(F32))[:, None]
    mag = jnp.exp(lr * dt)
    ar = mag * jnp.cos(li * dt)
    ai = mag * jnp.sin(li * dt)
    den = lr * lr + li * li
    zr = ((ar - 1.0) * lr + ai * li) / den
    zi = (ai * lr - (ar - 1.0) * li) / den
    br = b_re.astype(F32)
    bi = b_im.astype(F32)
    bbr = zr[..., None] * br - zi[..., None] * bi
    bbi = zr[..., None] * bi + zi[..., None] * br

    def cpow(n):
        m = jnp.exp(lr * dt * n)
        return (m * jnp.cos(li * dt * n)).reshape(-1), (m * jnp.sin(li * dt * n)).reshape(-1)

    rows = jnp.arange(SCAN_ROWS)[:, None]
    tabs_re, tabs_im = [], []
    for k in range(3):
        pr, pi = cpow(float(1 << k))
        keep = rows >= (1 << k)
        tabs_re.append(jnp.where(keep, pr[None, :], 0.0))
        tabs_im.append(jnp.where(keep, pi[None, :], 0.0))
    pw = [cpow(float(i + 1)) for i in range(SCAN_ROWS)]
    p_re = jnp.stack([p[0] for p in pw])
    p_im = jnp.stack([p[1] for p in pw])
    tabs = jnp.stack(tabs_re + tabs_im + [p_re, p_im]).astype(F32)

    g_per_tile = S5_CH_TILE // S5_GROUP
    n_tiles = S5_GROUPS // g_per_tile
    eye = jnp.eye(g_per_tile, dtype=F32)

    def b_tile(bb):
        x = bb.reshape(n_tiles, g_per_tile, S5_STATE, S5_GROUP)
        x = jnp.einsum('ngpc,gh->ngchp', x, eye)
        return x.reshape(n_tiles, S5_CH_TILE, S5_LANE_TILE)

    bmat = jnp.concatenate([b_tile(bbr), b_tile(bbi)], axis=2).astype(BF16)

    def c_tile(cc):
        x = cc.astype(F32).reshape(n_tiles, g_per_tile, S5_GROUP, S5_STATE)
        x = jnp.einsum('ngcp,gh->ngphc', x, eye)
        return x.reshape(n_tiles, S5_LANE_TILE, S5_CH_TILE)

    cmat = jnp.concatenate([c_tile(c_re), -c_tile(c_im)], axis=1).astype(BF16)
    return bmat, cmat, tabs


def _split2(x):
    hi = x.astype(BF16)
    lo = (x - hi.astype(F32)).astype(BF16)
    return hi, lo


def _split3(x):
    hi = x.astype(BF16)
    r = x - hi.astype(F32)
    mid = r.astype(BF16)
    lo = (r - mid.astype(F32)).astype(BF16)
    return hi, mid, lo


def _hgrn_kernel(zh_ref, lb_ref, gn_ref, wo_ref, s0_ref, yb_ref, s_out,
                 st, gc_ref, att_ref, o_ref, *, n_seg, seg_len, last_chunk):
    j = pl.program_id(1)
    rows = n_seg * seg_len

    @pl.when(j == 0)
    def _():
        for seg in range(n_seg):
            for h in range(HG_HEADS):
                st[seg, h] = s0_ref[seg, h].T

    zq = zh_ref[:, 0:HG_WIDTH]
    zf = zh_ref[:, HG_WIDTH:2 * HG_WIDTH]
    v = zh_ref[:, 2 * HG_WIDTH:3 * HG_WIDTH]
    lb = lb_ref[...]
    q = zq * jax.nn.sigmoid(zq)
    f = lb + (1.0 - lb) * jax.nn.sigmoid(zf)
    k = 1.0 - f
    g = jnp.log(f)
    vb = v.astype(BF16)

    ri = lax.broadcasted_iota(jnp.int32, (rows, rows), 0)
    ci = lax.broadcasted_iota(jnp.int32, (rows, rows), 1)
    seg_shift = int(math.log2(seg_len))
    same_seg = (ri >> seg_shift) == (ci >> seg_shift)

    tri = jnp.where(same_seg & (ci <= ri), 1.0, 0.0).astype(BF16)
    g3 = jnp.concatenate(_split3(g), axis=1)
    gcs = _dot(tri, g3)
    gc = gcs[:, :HG_WIDTH] + gcs[:, HG_WIDTH:2 * HG_WIDTH] + gcs[:, 2 * HG_WIDTH:]
    gc_ref[...] = gc

    xor = jnp.where(ci < ri, ri ^ ci, 0)
    qb = q.astype(BF16)
    kb = k.astype(BF16)
    for h in range(HG_HEADS):
        hs = slice(h * HG_DK, (h + 1) * HG_DK)
        att_ref[h] = jnp.where(ri == ci, _dot_nt(qb[:, hs], kb[:, hs]), 0.0)

    level = seg_len // 2
    while level >= 1:
        if 2 * level >= SCAN_ROWS:
            blocks = []
            for b in range(rows // (2 * level)):
                mid = b * 2 * level + level - 1
                blocks.append(jnp.broadcast_to(gc_ref[mid:mid + 1, :], (2 * level, HG_WIDTH)))
            gmid = blocks[0] if len(blocks) == 1 else jnp.concatenate(blocks, axis=0)
            dexp = -jnp.abs(gc - gmid)
        else:
            mid = (ri & ~(2 * level - 1)) + (level - 1)
            upper = (ri & level) != 0
            w = (ci > jnp.where(upper, mid, ri)) & (ci <= jnp.where(upper, ri, mid))
            g2 = jnp.concatenate(_split2(g), axis=1)
            ds = _dot(jnp.where(w, 1.0, 0.0).astype(BF16), g2)
            dexp = ds[:, :HG_WIDTH] + ds[:, HG_WIDTH:]
        e = jnp.exp(dexp)
        ql = (q * e).astype(BF16)
        kl = (k * e).astype(BF16)
        lshift = int(math.log2(level))
        sel = (xor >> lshift) == 1
        for h in range(HG_HEADS):
            hs = slice(h * HG_DK, (h + 1) * HG_DK)
            att_ref[h] += jnp.where(sel, _dot_nt(ql[:, hs], kl[:, hs]), 0.0)
        level //= 2

    qg = (q * jnp.exp(gc)).astype(BF16)
    for h in range(HG_HEADS):
        hs = slice(h * HG_DK, (h + 1) * HG_DK)
        o_ref[:, hs] = _dot(att_ref[h].astype(BF16), vb[:, hs])

    for seg in range(n_seg):
        rs = slice(seg * seg_len, (seg + 1) * seg_len)
        last = (seg + 1) * seg_len - 1
        g_last = gc_ref[last:last + 1, :]
        kd = (k[rs] * jnp.exp(g_last - gc[rs])).astype(BF16)
        decay = jnp.exp(g_last)
        for h in range(HG_HEADS):
            hs = slice(h * HG_DK, (h + 1) * HG_DK)
            s_t = st[seg, h]
            o_ref[rs, hs] += _dot_nt(qg[rs, hs], s_t.astype(BF16))
            st[seg, h] = s_t * decay[:, hs] + _dot_tn(vb[rs, hs], kd[:, hs])

    @pl.when(j == last_chunk)
    def _():
        for seg in range(n_seg):
            for h in range(HG_HEADS):
                s_out[seg, h] = st[seg, h].T

    zg = zh_ref[:, 3 * HG_WIDTH:]
    gate = zg * jax.nn.sigmoid(zg)
    outs = []
    for h in range(HG_HEADS):
        hs = slice(h * HG_DK, (h + 1) * HG_DK)
        outs.append(_rms(o_ref[:, hs], gn_ref[...]) * gate[:, hs])
    yb_ref[...] = _dot(jnp.concatenate(outs, axis=1).astype(BF16), wo_ref[...])


def _hgrn(zh, lb, gn, wo, s0, *, n_seq, seq_len, row_off):
    if seq_len >= ROW_TILE:
        n_seg, seg_len, chunks = 1, ROW_TILE, seq_len // ROW_TILE
    else:
        n_seg, seg_len, chunks = ROW_TILE // seq_len, seq_len, 1
    n_outer = n_seq // n_seg
    t = n_seq * seq_len
    row = lambda i, j: (row_off + i * chunks + j, 0)
    orow = lambda i, j: (i * chunks + j, 0)
    st = lambda i, j: (i, 0, 0, 0)
    kern = functools.partial(_hgrn_kernel, n_seg=n_seg, seg_len=seg_len, last_chunk=chunks - 1)
    return pl.pallas_call(
        kern,
        grid=(n_outer, chunks),
        in_specs=[pl.BlockSpec((ROW_TILE, 4 * HG_WIDTH), row),
                  pl.BlockSpec((1, HG_WIDTH), lambda i, j: (0, 0)),
                  pl.BlockSpec((1, HG_DK), lambda i, j: (0, 0)),
                  pl.BlockSpec(wo.shape, lambda i, j: (0, 0)),
                  pl.BlockSpec((n_seg, HG_HEADS, HG_DK, HG_DK), st)],
        out_specs=[pl.BlockSpec((ROW_TILE, D_MODEL), orow),
                   pl.BlockSpec((n_seg, HG_HEADS, HG_DK, HG_DK), st)],
        out_shape=[jax.ShapeDtypeStruct((t, D_MODEL), F32),
                   jax.ShapeDtypeStruct((n_seq, HG_HEADS, HG_DK, HG_DK), F32)],
        scratch_shapes=[pltpu.VMEM((n_seg, HG_HEADS, HG_DK, HG_DK), F32),
                        pltpu.VMEM((ROW_TILE, HG_WIDTH), F32),
                        pltpu.VMEM((HG_HEADS, ROW_TILE, ROW_TILE), F32),
                        pltpu.VMEM((ROW_TILE, HG_WIDTH), F32)],
        compiler_params=_cparams(2),
        name="hgrn2",
    )(zh, lb, gn, wo, s0)


def _merge_kernel(gab_ref, ya_ref, yb_ref, x_ref, wo_ref, gf_ref, rw_ref, rb_ref, cnt0_ref,
                  x1_ref, h2_ref, topi_ref, gate_ref, rank_ref, cnt_out, cnt):
    i = pl.program_id(0)

    @pl.when(i == 0)
    def _():
        cnt[...] = cnt0_ref[...]

    ga = gab_ref[:, :D_MODEL]
    gb = gab_ref[:, D_MODEL:]
    mixed = jax.nn.sigmoid(ga) * ya_ref[...] + jax.nn.sigmoid(gb) * yb_ref[...]
    x1 = x_ref[...] + _dot(mixed.astype(BF16), wo_ref[...])
    x1_ref[...] = x1
    h2 = _rms(x1, gf_ref[...])
    h2_ref[...] = h2

    h_hi, h_lo = _split2(h2)
    p_hi = _dot(h_hi, rw_ref[...])
    p_lo = _dot(h_lo, rw_ref[:, :N_EXPERTS])
    logits = p_hi[:, :N_EXPERTS] + p_hi[:, N_EXPERTS:] + p_lo + rb_ref[...]

    rows = logits.shape[0]
    lane = lax.broadcasted_iota(jnp.int32, (rows, N_EXPERTS), 1).astype(F32)
    lane4 = lax.broadcasted_iota(jnp.int32, (rows, TOP_K), 1)
    lg = logits
    sel = jnp.zeros((rows, N_EXPERTS), F32)
    vals, idxs = [], []
    for _ in range(TOP_K):
        m = jnp.max(lg, axis=-1, keepdims=True)
        idx = jnp.min(jnp.where(lg == m, lane, float(N_EXPERTS)), axis=-1, keepdims=True)
        hit = lane == idx
        vals.append(m)
        idxs.append(idx)
        lg = jnp.where(hit, -jnp.inf, lg)
        sel = sel + jnp.where(hit, 1.0, 0.0)

    es = [jnp.exp(vv - vals[0]) for vv in vals]
    tot = es[0] + es[1] + es[2] + es[3]

    ri = lax.broadcasted_iota(jnp.int32, (rows, rows), 0)
    ci = lax.broadcasted_iota(jnp.int32, (rows, rows), 1)
    before = jnp.where(ci < ri, 1.0, 0.0).astype(BF16)
    cnt_excl = _dot(before, sel.astype(BF16)) + cnt[...]
    ranks = [jnp.sum(jnp.where(lane == idx, cnt_excl, 0.0), axis=-1, keepdims=True) for idx in idxs]

    topi = jnp.zeros((rows, TOP_K), jnp.int32)
    gates = jnp.zeros((rows, TOP_K), F32)
    rank = jnp.zeros((rows, TOP_K), jnp.int32)
    for kk in range(TOP_K):
        topi = jnp.where(lane4 == kk, idxs[kk].astype(jnp.int32), topi)
        gates = jnp.where(lane4 == kk, es[kk] / tot, gates)
        rank = jnp.where(lane4 == kk, ranks[kk].astype(jnp.int32), rank)
    topi_ref[...] = topi
    gate_ref[...] = gates
    rank_ref[...] = rank

    cnt[...] = cnt[...] + jnp.sum(sel, axis=0, keepdims=True)
    cnt_out[...] = cnt[...]


def _merge(gab, ya, yb, x, wo, g_ffn, rw2, rb, cnt0):
    t = x.shape[0]
    row = lambda i: (i, 0)
    fixed = lambda i: (0, 0)
    return pl.pallas_call(
        _merge_kernel,
        grid=(t // ROW_TILE,),
        in_specs=[pl.BlockSpec((ROW_TILE, 2 * D_MODEL), row),
                  pl.BlockSpec((ROW_TILE, D_MODEL), row),
                  pl.BlockSpec((ROW_TILE, D_MODEL), row),
                  pl.BlockSpec((ROW_TILE, D_MODEL), row),
                  pl.BlockSpec((D_MODEL, D_MODEL), fixed),
                  pl.BlockSpec((1, D_MODEL), fixed),
                  pl.BlockSpec((D_MODEL, 2 * N_EXPERTS), fixed),
                  pl.BlockSpec((1, N_EXPERTS), fixed),
                  pl.BlockSpec((1, N_EXPERTS), fixed)],
        out_specs=[pl.BlockSpec((ROW_TILE, D_MODEL), row),
                   pl.BlockSpec((ROW_TILE, D_MODEL), row),
                   pl.BlockSpec((ROW_TILE, TOP_K), row),
                   pl.BlockSpec((ROW_TILE, TOP_K), row),
                   pl.BlockSpec((ROW_TILE, TOP_K), row),
                   pl.BlockSpec((1, N_EXPERTS), fixed)],
        out_shape=[jax.ShapeDtypeStruct((t, D_MODEL), F32),
                   jax.ShapeDtypeStruct((t, D_MODEL), F32),
                   jax.ShapeDtypeStruct((t, TOP_K), jnp.int32),
                   jax.ShapeDtypeStruct((t, TOP_K), F32),
                   jax.ShapeDtypeStruct((t, TOP_K), jnp.int32),
                   jax.ShapeDtypeStruct((1, N_EXPERTS), F32)],
        scratch_shapes=[pltpu.VMEM((1, N_EXPERTS), F32)],
        compiler_params=_cparams(1),
        name="merge_router",
    )(gab, ya, yb, x, wo, g_ffn, rw2, rb, cnt0)


def _expert_kernel(blk_e, n_used, rt0_ref, rtn_ref, h2_hbm, w1_ref, b1_ref, w2_ref, b2_ref,
                   out_ref, xbuf, sem, w1b, w2b):
    b = pl.program_id(0)
    nu = n_used[0]

    def gather_copy(tok, r, slot):
        return pltpu.make_async_copy(h2_hbm.at[pl.ds(tok, 1)], xbuf.at[slot, pl.ds(r, 1)],
                                     sem.at[slot])

    def start_gather(rt_ref, slot):
        def body(r, _):
            gather_copy(rt_ref[0, 0, r], r, slot).start()
            return 0
        lax.fori_loop(0, EXPERT_ROWS, body, 0, unroll=8)

    @pl.when(b == 0)
    def _():
        start_gather(rt0_ref, 0)

    @pl.when(b + 1 < nu)
    def _():
        start_gather(rtn_ref, (b + 1) % 2)

    @pl.when(b < nu)
    def _():
        slot = b % 2
        pltpu.make_async_copy(h2_hbm.at[pl.ds(0, EXPERT_ROWS)], xbuf.at[slot], sem.at[slot]).wait()

        @pl.when((b == 0) | (blk_e[b] != blk_e[jnp.maximum(b - 1, 0)]))
        def _():
            w1b[...] = w1_ref[0].astype(BF16)
            w2b[...] = w2_ref[0].astype(BF16)

        x = xbuf[slot].astype(BF16)
        u = _dot(x, w1b[...]) + b1_ref[0]
        a = jnp.minimum(u[:, :D_EXPERT], SWIGLU_LIMIT)
        lin = jnp.clip(u[:, D_EXPERT:], -SWIGLU_LIMIT, SWIGLU_LIMIT)
        hmid = a * jax.nn.sigmoid(SWIGLU_ALPHA * a) * (lin + 1.0)
        out_ref[...] = _dot(hmid.astype(BF16), w2b[...]) + b2_ref[0]

    @pl.when(b >= nu)
    def _():
        out_ref[...] = jnp.zeros_like(out_ref)


def _experts(h2, row_tok, blk_e, n_used, w1, b1, w2, b2):
    n_blocks = blk_e.shape[0]
    rt = row_tok.reshape(n_blocks, 1, EXPERT_ROWS)
    grid_spec = pltpu.PrefetchScalarGridSpec(
        num_scalar_prefetch=2,
        grid=(n_blocks,),
        in_specs=[pl.BlockSpec((1, 1, EXPERT_ROWS), lambda b, be, nu: (0, 0, 0),
                               memory_space=pltpu.SMEM),
                  pl.BlockSpec((1, 1, EXPERT_ROWS),
                               lambda b, be, nu: (jnp.minimum(b + 1, n_blocks - 1), 0, 0),
                               memory_space=pltpu.SMEM),
                  pl.BlockSpec(memory_space=pl.ANY),
                  pl.BlockSpec((1, D_MODEL, 2 * D_EXPERT), lambda b, be, nu: (be[b], 0, 0)),
                  pl.BlockSpec((1, 1, 2 * D_EXPERT), lambda b, be, nu: (be[b], 0, 0)),
                  pl.BlockSpec((1, D_EXPERT, D_MODEL), lambda b, be, nu: (be[b], 0, 0)),
                  pl.BlockSpec((1, 1, D_MODEL), lambda b, be, nu: (be[b], 0, 0))],
        out_specs=pl.BlockSpec((EXPERT_ROWS, D_MODEL), lambda b, be, nu: (b, 0)),
        scratch_shapes=[pltpu.VMEM((2, EXPERT_ROWS, D_MODEL), F32),
                        pltpu.SemaphoreType.DMA((2,)),
                        pltpu.VMEM((D_MODEL, 2 * D_EXPERT), BF16),
                        pltpu.VMEM((D_EXPERT, D_MODEL), BF16)],
    )
    return pl.pallas_call(
        _expert_kernel,
        grid_spec=grid_spec,
        out_shape=jax.ShapeDtypeStruct((n_blocks * EXPERT_ROWS, D_MODEL), F32),
        compiler_params=_cparams(1),
        name="experts",
    )(blk_e, n_used, rt, rt, h2, w1, b1.reshape(N_EXPERTS, 1, -1), w2, b2.reshape(N_EXPERTS, 1, -1))


def _combine_kernel(d0_ref, dn_ref, gate_ref, x1_ref, gf_ref, yb_hbm, y_ref, buf, sem,
                    *, n_steps):
    i = pl.program_id(0)

    def start_gather(d_ref, slot):
        def body(r, _):
            for kk in range(TOP_K):
                pltpu.make_async_copy(yb_hbm.at[pl.ds(d_ref[0, 0, r * TOP_K + kk], 1)],
                                      buf.at[slot, kk, pl.ds(r, 1)], sem.at[slot]).start()
            return 0
        lax.fori_loop(0, COMBINE_ROWS, body, 0, unroll=4)

    @pl.when(i == 0)
    def _():
        start_gather(d0_ref, 0)

    @pl.when(i + 1 < n_steps)
    def _():
        start_gather(dn_ref, (i + 1) % 2)

    slot = i % 2
    for kk in range(TOP_K):
        pltpu.make_async_copy(yb_hbm.at[pl.ds(0, COMBINE_ROWS)], buf.at[slot, kk],
                              sem.at[slot]).wait()
    gates = gate_ref[...]
    moe = gates[:, 0:1] * buf[slot, 0]
    for kk in range(1, TOP_K):
        moe = moe + gates[:, kk:kk + 1] * buf[slot, kk]
    y_ref[...] = _rms(x1_ref[...] + moe, gf_ref[...])


def _combine(dest, gates, x1, g_final, yb):
    t = x1.shape[0]
    n_steps = t // COMBINE_ROWS
    d3 = dest.reshape(n_steps, 1, COMBINE_ROWS * TOP_K)
    row = lambda i: (i, 0)
    kern = functools.partial(_combine_kernel, n_steps=n_steps)
    return pl.pallas_call(
        kern,
        grid=(n_steps,),
        in_specs=[pl.BlockSpec((1, 1, COMBINE_ROWS * TOP_K), lambda i: (0, 0, 0),
                               memory_space=pltpu.SMEM),
                  pl.BlockSpec((1, 1, COMBINE_ROWS * TOP_K),
                               lambda i: (jnp.minimum(i + 1, n_steps - 1), 0, 0),
                               memory_space=pltpu.SMEM),
                  pl.BlockSpec((COMBINE_ROWS, TOP_K), row),
                  pl.BlockSpec((COMBINE_ROWS, D_MODEL), row),
                  pl.BlockSpec((1, D_MODEL), lambda i: (0, 0)),
                  pl.BlockSpec(memory_space=pl.ANY)],
        out_specs=pl.BlockSpec((COMBINE_ROWS, D_MODEL), row),
        out_shape=jax.ShapeDtypeStruct((t, D_MODEL), F32),
        scratch_shapes=[pltpu.VMEM((2, TOP_K, COMBINE_ROWS, D_MODEL), F32),
                        pltpu.SemaphoreType.DMA((2,))],
        compiler_params=_cparams(1),
        name="combine",
    )(d3, d3, gates, x1, g_final.reshape(1, D_MODEL), yb)


def kernel(x_prompt, x_sample, state_s5_re, state_s5_im, state_hgrn, g_mix, w_in, s5_lambda_re,
           s5_lambda_im, s5_log_dt, s5_b_re, s5_b_im, s5_c_re, s5_c_im, s5_d, s5_w_glu, s5_w_gate,
           hgrn_lower_bound, hgrn_norm_g, hgrn_w_out, w_out, g_ffn, router_w, router_b, moe_w1,
           moe_b1, moe_w2, moe_b2, g_final):
    depth = g_mix.shape[0]
    assert depth == 1
    li = 0
    n_p, l_p, _ = x_prompt.shape
    n_s, l_s, _ = x_sample.shape
    t_p, t_s = n_p * l_p, n_s * l_s
    t = t_p + t_s

    lb = jnp.cumsum(jax.nn.softmax(hgrn_lower_bound.astype(F32), axis=0), axis=0)[li].reshape(1, -1)
    bmat, cmat, tabs = _s5_tables(s5_lambda_re[li], s5_lambda_im[li], s5_log_dt[li], s5_b_re[li],
                                  s5_b_im[li], s5_c_re[li], s5_c_im[li])
    wg = jnp.concatenate([s5_w_glu[li], s5_w_gate[li]], axis=1).astype(BF16)
    rw_hi = router_w[li].astype(BF16)
    rw_lo = (router_w[li] - rw_hi.astype(F32)).astype(BF16)
    rw2 = jnp.concatenate([rw_hi, rw_lo], axis=1)

    x = jnp.concatenate([x_prompt.reshape(t_p, D_MODEL), x_sample.reshape(t_s, D_MODEL)], axis=0)
    u, zh, gab = _inproj(x, g_mix[li], w_in[li].astype(BF16))

    streams = [
        dict(n_seq=n_p, seq_len=l_p, row_off=0,
             s5r=jnp.zeros((n_p, 1, S5_LANES), F32), s5i=jnp.zeros((n_p, 1, S5_LANES), F32),
             hg=jnp.zeros((n_p, HG_HEADS, HG_DK, HG_DK), F32)),
        dict(n_seq=n_s, seq_len=l_s, row_off=t_p // ROW_TILE,
             s5r=state_s5_re[li].reshape(n_s, 1, S5_LANES).astype(F32),
             s5i=state_s5_im[li].reshape(n_s, 1, S5_LANES).astype(F32),
             hg=state_hgrn[li].astype(F32)),
    ]
    ya, yb, s5r, s5i, hg = [], [], [], [], []
    for s in streams:
        y, a, b = _s5(u, s['s5r'], s['s5i'], bmat, cmat, tabs, s5_d[li].reshape(1, -1), wg,
                      n_seq=s['n_seq'], seq_len=s['seq_len'], row_off=s['row_off'])
        ya.append(y)
        s5r.append(a.reshape(1, s['n_seq'], S5_GROUPS, S5_STATE))
        s5i.append(b.reshape(1, s['n_seq'], S5_GROUPS, S5_STATE))
        y, c = _hgrn(zh, lb, hgrn_norm_g[li].reshape(1, -1), hgrn_w_out[li].astype(BF16), s['hg'],
                     n_seq=s['n_seq'], seq_len=s['seq_len'], row_off=s['row_off'])
        yb.append(y)
        hg.append(c[None])
    ya = jnp.concatenate(ya, axis=0)
    yb = jnp.concatenate(yb, axis=0)

    x1, h2, topi, gates, rank, counts = _merge(
        gab, ya, yb, x, w_out[li].astype(BF16), g_ffn[li].reshape(1, -1), rw2,
        router_b[li].reshape(1, -1).astype(F32), jnp.zeros((1, N_EXPERTS), F32))

    n_assign = t * TOP_K
    n_blocks = n_assign // EXPERT_ROWS + N_EXPERTS
    counts = counts.reshape(N_EXPERTS).astype(jnp.int32)
    padded = (counts + EXPERT_ROWS - 1) // EXPERT_ROWS * EXPERT_ROWS
    pad_end = jnp.cumsum(padded)
    pad_start = pad_end - padded
    dest = pad_start[topi] + rank
    tok = jnp.arange(n_assign, dtype=jnp.int32) // TOP_K
    row_tok = jnp.zeros((n_blocks * EXPERT_ROWS,), jnp.int32).at[dest.reshape(-1)].set(tok)
    blk_e = jnp.minimum(
        jnp.searchsorted(pad_end, jnp.arange(n_blocks, dtype=jnp.int32) * EXPERT_ROWS, side='right'),
        N_EXPERTS - 1).astype(jnp.int32)
    n_used = (pad_end[-1:] // EXPERT_ROWS).astype(jnp.int32)

    yexp = _experts(h2, row_tok, blk_e, n_used, moe_w1[li], moe_b1[li], moe_w2[li], moe_b2[li])
    y = _combine(dest, gates, x1, g_final, yexp)

    dt = x_prompt.dtype
    return (y[:t_p].reshape(n_p, l_p, D_MODEL), y[t_p:].reshape(n_s, l_s, D_MODEL),
            s5r[0].astype(dt), s5i[0].astype(dt), hg[0].astype(dt),
            s5r[1].astype(state_s5_re.dtype), s5i[1].astype(state_s5_im.dtype),
            hg[1].astype(state_hgrn.dtype))
```

```python
import functools
import math

import jax
import jax.numpy as jnp
from jax import lax
from jax.experimental import pallas as pl
from jax.experimental.pallas import tpu as pltpu

F32 = jnp.float32
BF16 = jnp.bfloat16

D_MODEL = 1024
NORM_EPS = 1e-6
S5_WIDTH = 512
S5_GROUP = 16
S5_GROUPS = 32
S5_STATE = 64
S5_LANES = S5_GROUPS * S5_STATE
S5_DT_CLIP = -1e-4
HG_WIDTH = 512
HG_HEADS = 4
HG_DK = 128
N_EXPERTS = 32
TOP_K = 4
D_EXPERT = 1024
SWIGLU_ALPHA = 1.702
SWIGLU_LIMIT = 7.0

ROW_TILE = 256
SCAN_ROWS = 8
S5_CH_TILE = 128
S5_LANE_TILE = 512
EXPERT_ROWS = 256
GRANULE = 8
TILE_SLOTS = ROW_TILE * TOP_K + N_EXPERTS * GRANULE
GRAN_PER_TILE = TILE_SLOTS // GRANULE
GRAN_PER_BLOCK = EXPERT_ROWS // GRANULE
VMEM_LIMIT = 56 * 1024 * 1024


def _cparams(n_axes):
    return pltpu.CompilerParams(dimension_semantics=("arbitrary",) * n_axes,
                                vmem_limit_bytes=VMEM_LIMIT)


def _rms(x, g):
    r = lax.rsqrt(jnp.mean(x * x, axis=-1, keepdims=True) + NORM_EPS)
    return x * r * g


def _dot(a, b):
    return jnp.dot(a, b, preferred_element_type=F32)


def _dot_nt(a, b):
    return lax.dot_general(a, b, (((1,), (1,)), ((), ())), preferred_element_type=F32)


def _dot_tn(a, b):
    return lax.dot_general(a, b, (((0,), (0,)), ((), ())), preferred_element_type=F32)


def _inproj_kernel(x_ref, g_ref, w_ref, u_ref, zh_ref, gab_ref):
    h = _rms(x_ref[...], g_ref[...]).astype(BF16)
    u_ref[...] = _dot(h, w_ref[:, :S5_WIDTH])
    zh_ref[...] = _dot(h, w_ref[:, S5_WIDTH:S5_WIDTH + 4 * HG_WIDTH])
    gab_ref[...] = _dot(h, w_ref[:, S5_WIDTH + 4 * HG_WIDTH:])


def _inproj(x, g_mix, w_in_bf):
    t = x.shape[0]
    ncols = w_in_bf.shape[1]
    row = lambda i: (i, 0)
    fixed = lambda i: (0, 0)
    return pl.pallas_call(
        _inproj_kernel,
        grid=(t // ROW_TILE,),
        in_specs=[pl.BlockSpec((ROW_TILE, D_MODEL), row),
                  pl.BlockSpec((1, D_MODEL), fixed),
                  pl.BlockSpec((D_MODEL, ncols), fixed)],
        out_specs=[pl.BlockSpec((ROW_TILE, S5_WIDTH), row),
                   pl.BlockSpec((ROW_TILE, 4 * HG_WIDTH), row),
                   pl.BlockSpec((ROW_TILE, 2 * D_MODEL), row)],
        out_shape=[jax.ShapeDtypeStruct((t, S5_WIDTH), F32),
                   jax.ShapeDtypeStruct((t, 4 * HG_WIDTH), F32),
                   jax.ShapeDtypeStruct((t, 2 * D_MODEL), F32)],
        compiler_params=_cparams(1),
        name="inproj",
    )(x, g_mix.reshape(1, D_MODEL), w_in_bf)


def _s5_kernel(u_ref, s0r_ref, s0i_ref, bmat_ref, cmat_ref, tab_ref, d_ref, wg_ref,
               ya_ref, sr_out, si_out, bur, bui, cre, cim, *, n_seg, seg_len):
    j = pl.program_id(1)

    @pl.when(j == 0)
    def _():
        cre[...] = s0r_ref[...]
        cim[...] = s0i_ref[...]

    u = u_ref[...]
    ub = u.astype(BF16)
    n_tiles = S5_WIDTH // S5_CH_TILE
    for n in range(n_tiles):
        bu = _dot(ub[:, n * S5_CH_TILE:(n + 1) * S5_CH_TILE], bmat_ref[n])
        bur[:, n * S5_LANE_TILE:(n + 1) * S5_LANE_TILE] = bu[:, :S5_LANE_TILE]
        bui[:, n * S5_LANE_TILE:(n + 1) * S5_LANE_TILE] = bu[:, S5_LANE_TILE:]

    for n in range(n_tiles):
        ls = slice(n * S5_LANE_TILE, (n + 1) * S5_LANE_TILE)
        for seg in range(n_seg):
            def body(r, carry, seg=seg, ls=ls):
                c_re, c_im = carry
                row = pl.multiple_of(seg * seg_len + r * SCAN_ROWS, SCAN_ROWS)
                xr = bur[pl.ds(row, SCAN_ROWS), ls]
                xi = bui[pl.ds(row, SCAN_ROWS), ls]
                for k in range(3):
                    rr = pltpu.roll(xr, 1 << k, axis=0)
                    ri = pltpu.roll(xi, 1 << k, axis=0)
                    a_re = tab_ref[k, :, ls]
                    a_im = tab_ref[3 + k, :, ls]
                    xr, xi = (xr + (a_re * rr - a_im * ri),
                              xi + (a_re * ri + a_im * rr))
                cbr = jnp.broadcast_to(c_re, xr.shape)
                cbi = jnp.broadcast_to(c_im, xi.shape)
                p_re = tab_ref[6, :, ls]
                p_im = tab_ref[7, :, ls]
                xr, xi = (xr + (p_re * cbr - p_im * cbi),
                          xi + (p_re * cbi + p_im * cbr))
                bur[pl.ds(row, SCAN_ROWS), ls] = xr
                bui[pl.ds(row, SCAN_ROWS), ls] = xi
                return xr[SCAN_ROWS - 1:, :], xi[SCAN_ROWS - 1:, :]

            c_re, c_im = lax.fori_loop(0, seg_len // SCAN_ROWS, body,
                                       (cre[seg, :, ls], cim[seg, :, ls]))
            cre[seg, :, ls] = c_re
            cim[seg, :, ls] = c_im

    sr_out[...] = cre[...]
    si_out[...] = cim[...]

    ys = []
    for n in range(n_tiles):
        ls = slice(n * S5_LANE_TILE, (n + 1) * S5_LANE_TILE)
        ys.append(_dot(bur[:, ls].astype(BF16), cmat_ref[n, :S5_LANE_TILE, :])
                  + _dot(bui[:, ls].astype(BF16), cmat_ref[n, S5_LANE_TILE:, :]))
    y = jnp.concatenate(ys, axis=1) + d_ref[...] * u
    y = jax.nn.gelu(y).astype(BF16)
    p = _dot(y, wg_ref[...])
    ya_ref[...] = p[:, :D_MODEL] * jax.nn.sigmoid(p[:, D_MODEL:])


def _s5(u, s0_re, s0_im, bmat, cmat, tabs, d_skip, wg, *, n_seq, seq_len, row_off):
    if seq_len >= ROW_TILE:
        n_seg, seg_len, chunks = 1, ROW_TILE, seq_len // ROW_TILE
    else:
        n_seg, seg_len, chunks = ROW_TILE // seq_len, seq_len, 1
    n_outer = n_seq // n_seg
    t = n_seq * seq_len
    row = lambda i, j: (row_off + i * chunks + j, 0)
    orow = lambda i, j: (i * chunks + j, 0)
    st = lambda i, j: (i, 0, 0)
    kern = functools.partial(_s5_kernel, n_seg=n_seg, seg_len=seg_len)
    return pl.pallas_call(
        kern,
        grid=(n_outer, chunks),
        in_specs=[pl.BlockSpec((ROW_TILE, S5_WIDTH), row),
                  pl.BlockSpec((n_seg, 1, S5_LANES), st),
                  pl.BlockSpec((n_seg, 1, S5_LANES), st),
                  pl.BlockSpec(bmat.shape, lambda i, j: (0, 0, 0)),
                  pl.BlockSpec(cmat.shape, lambda i, j: (0, 0, 0)),
                  pl.BlockSpec(tabs.shape, lambda i, j: (0, 0, 0)),
                  pl.BlockSpec((1, S5_WIDTH), lambda i, j: (0, 0)),
                  pl.BlockSpec(wg.shape, lambda i, j: (0, 0))],
        out_specs=[pl.BlockSpec((ROW_TILE, D_MODEL), orow),
                   pl.BlockSpec((n_seg, 1, S5_LANES), st),
                   pl.BlockSpec((n_seg, 1, S5_LANES), st)],
        out_shape=[jax.ShapeDtypeStruct((t, D_MODEL), F32),
                   jax.ShapeDtypeStruct((n_seq, 1, S5_LANES), F32),
                   jax.ShapeDtypeStruct((n_seq, 1, S5_LANES), F32)],
        scratch_shapes=[pltpu.VMEM((ROW_TILE, S5_LANES), F32),
                        pltpu.VMEM((ROW_TILE, S5_LANES), F32),
                        pltpu.VMEM((n_seg, 1, S5_LANES), F32),
                        pltpu.VMEM((n_seg, 1, S5_LANES), F32)],
        compiler_params=_cparams(2),
        name="s5",
    )(u, s0_re, s0_im, bmat, cmat, tabs, d_skip, wg)


def _s5_tables(lam_re, lam_im, log_dt, b_re, b_im, c_re, c_im):
    lr = jnp.minimum(lam_re.astype(F32), S5_DT_CLIP)
    li = lam_im.astype(F32)
    dt = jnp.exp(log_dt.astype(F32))[:, None]
    mag = jnp.exp(lr * dt)
    ar = mag * jnp.cos(li * dt)
    ai = mag * jnp.sin(li * dt)
    den = lr * lr + li * li
    zr = ((ar - 1.0) * lr + ai * li) / den
    zi = (ai * lr - (ar - 1.0) * li) / den
    br = b_re.astype(F32)
    bi = b_im.astype(F32)
    bbr = zr[..., None] * br - zi[..., None] * bi
    bbi = zr[..., None] * bi + zi[..., None] * br

    def cpow(n):
        m = jnp.exp(lr * dt * n)
        return (m * jnp.cos(li * dt * n)).reshape(-1), (m * jnp.sin(li * dt * n)).reshape(-1)

    rows = jnp.arange(SCAN_ROWS)[:, None]
    tabs_re, tabs_im = [], []
    for k in range(3):
        pr, pi = cpow(float(1 << k))
        keep = rows >= (1 << k)
        tabs_re.append(jnp.where(keep, pr[None, :], 0.0))
        tabs_im.append(jnp.where(keep, pi[None, :], 0.0))
    pw = [cpow(float(i + 1)) for i in range(SCAN_ROWS)]
    p_re = jnp.stack([p[0] for p in pw])
    p_im = jnp.stack([p[1] for p in pw])
    tabs = jnp.stack(tabs_re + tabs_im + [p_re, p_im]).astype(F32)

    g_per_tile = S5_CH_TILE // S5_GROUP
    n_tiles = S5_GROUPS // g_per_tile
    eye = jnp.eye(g_per_tile, dtype=F32)

    def b_tile(bb):
        x = bb.reshape(n_tiles, g_per_tile, S5_STATE, S5_GROUP)
        x = jnp.einsum('ngpc,gh->ngchp', x, eye)
        return x.reshape(n_tiles, S5_CH_TILE, S5_LANE_TILE)

    bmat = jnp.concatenate([b_tile(bbr), b_tile(bbi)], axis=2).astype(BF16)

    def c_tile(cc):
        x = cc.astype(F32).reshape(n_tiles, g_per_tile, S5_GROUP, S5_STATE)
        x = jnp.einsum('ngcp,gh->ngphc', x, eye)
        return x.reshape(n_tiles, S5_LANE_TILE, S5_CH_TILE)

    cmat = jnp.concatenate([c_tile(c_re), -c_tile(c_im)], axis=1).astype(BF16)
    return bmat, cmat, tabs


def _split2(x):
    hi = x.astype(BF16)
    lo = (x - hi.astype(F32)).astype(BF16)
    return hi, lo


def _split3(x):
    hi = x.astype(BF16)
    r = x - hi.astype(F32)
    mid = r.astype(BF16)
    lo = (r - mid.astype(F32)).astype(BF16)
    return hi, mid, lo


def _hgrn_kernel(zh_ref, lb_ref, gn_ref, wo_ref, s0_ref, yb_ref, s_out,
                 st, gc_ref, att_ref, o_ref, *, n_seg, seg_len, last_chunk):
    j = pl.program_id(1)
    rows = n_seg * seg_len

    @pl.when(j == 0)
    def _():
        for seg in range(n_seg):
            for h in range(HG_HEADS):
                st[seg, h] = s0_ref[seg, h].T

    zq = zh_ref[:, 0:HG_WIDTH]
    zf = zh_ref[:, HG_WIDTH:2 * HG_WIDTH]
    v = zh_ref[:, 2 * HG_WIDTH:3 * HG_WIDTH]
    lb = lb_ref[...]
    q = zq * jax.nn.sigmoid(zq)
    f = lb + (1.0 - lb) * jax.nn.sigmoid(zf)
    k = 1.0 - f
    g = jnp.log(f)
    vb = v.astype(BF16)

    ri = lax.broadcasted_iota(jnp.int32, (rows, rows), 0)
    ci = lax.broadcasted_iota(jnp.int32, (rows, rows), 1)
    seg_shift = int(math.log2(seg_len))
    same_seg = (ri >> seg_shift) == (ci >> seg_shift)

    tri = jnp.where(same_seg & (ci <= ri), 1.0, 0.0).astype(BF16)
    g3 = jnp.concatenate(_split3(g), axis=1)
    gcs = _dot(tri, g3)
    gc = gcs[:, :HG_WIDTH] + gcs[:, HG_WIDTH:2 * HG_WIDTH] + gcs[:, 2 * HG_WIDTH:]
    gc_ref[...] = gc

    xor = jnp.where(ci < ri, ri ^ ci, 0)
    qb = q.astype(BF16)
    kb = k.astype(BF16)
    for h in range(HG_HEADS):
        hs = slice(h * HG_DK, (h + 1) * HG_DK)
        att_ref[h] = jnp.where(ri == ci, _dot_nt(qb[:, hs], kb[:, hs]), 0.0)

    level = seg_len // 2
    while level >= 1:
        if 2 * level >= SCAN_ROWS:
            blocks = []
            for b in range(rows // (2 * level)):
                mid = b * 2 * level + level - 1
                blocks.append(jnp.broadcast_to(gc_ref[mid:mid + 1, :], (2 * level, HG_WIDTH)))
            gmid = blocks[0] if len(blocks) == 1 else jnp.concatenate(blocks, axis=0)
            dexp = -jnp.abs(gc - gmid)
        else:
            mid = (ri & ~(2 * level - 1)) + (level - 1)
            upper = (ri & level) != 0
            w = (ci > jnp.where(upper, mid, ri)) & (ci <= jnp.where(upper, ri, mid))
            g2 = jnp.concatenate(_split2(g), axis=1)
            ds = _dot(jnp.where(w, 1.0, 0.0).astype(BF16), g2)
            dexp = ds[:, :HG_WIDTH] + ds[:, HG_WIDTH:]
        e = jnp.exp(dexp)
        ql = (q * e).astype(BF16)
        kl = (k * e).astype(BF16)
        lshift = int(math.log2(level))
        sel = (xor >> lshift) == 1
        for h in range(HG_HEADS):
            hs = slice(h * HG_DK, (h + 1) * HG_DK)
            att_ref[h] += jnp.where(sel, _dot_nt(ql[:, hs], kl[:, hs]), 0.0)
        level //= 2

    qg = (q * jnp.exp(gc)).astype(BF16)
    for h in range(HG_HEADS):
        hs = slice(h * HG_DK, (h + 1) * HG_DK)
        o_ref[:, hs] = _dot(att_ref[h].astype(BF16), vb[:, hs])

    for seg in range(n_seg):
        rs = slice(seg * seg_len, (seg + 1) * seg_len)
        last = (seg + 1) * seg_len - 1
        g_last = gc_ref[last:last + 1, :]
        kd = (k[rs] * jnp.exp(g_last - gc[rs])).astype(BF16)
        decay = jnp.exp(g_last)
        for h in range(HG_HEADS):
            hs = slice(h * HG_DK, (h + 1) * HG_DK)
            s_t = st[seg, h]
            o_ref[rs, hs] += _dot_nt(qg[rs, hs], s_t.astype(BF16))
            st[seg, h] = s_t * decay[:, hs] + _dot_tn(vb[rs, hs], kd[:, hs])

    @pl.when(j == last_chunk)
    def _():
        for seg in range(n_seg):
            for h in range(HG_HEADS):
                s_out[seg, h] = st[seg, h].T

    zg = zh_ref[:, 3 * HG_WIDTH:]
    gate = zg * jax.nn.sigmoid(zg)
    outs = []
    for h in range(HG_HEADS):
        hs = slice(h * HG_DK, (h + 1) * HG_DK)
        outs.append(_rms(o_ref[:, hs], gn_ref[...]) * gate[:, hs])
    yb_ref[...] = _dot(jnp.concatenate(outs, axis=1).astype(BF16), wo_ref[...])


def _hgrn(zh, lb, gn, wo, s0, *, n_seq, seq_len, row_off):
    if seq_len >= ROW_TILE:
        n_seg, seg_len, chunks = 1, ROW_TILE, seq_len // ROW_TILE
    else:
        n_seg, seg_len, chunks = ROW_TILE // seq_len, seq_len, 1
    n_outer = n_seq // n_seg
    t = n_seq * seq_len
    row = lambda i, j: (row_off + i * chunks + j, 0)
    orow = lambda i, j: (i * chunks + j, 0)
    st = lambda i, j: (i, 0, 0, 0)
    kern = functools.partial(_hgrn_kernel, n_seg=n_seg, seg_len=seg_len, last_chunk=chunks - 1)
    return pl.pallas_call(
        kern,
        grid=(n_outer, chunks),
        in_specs=[pl.BlockSpec((ROW_TILE, 4 * HG_WIDTH), row),
                  pl.BlockSpec((1, HG_WIDTH), lambda i, j: (0, 0)),
                  pl.BlockSpec((1, HG_DK), lambda i, j: (0, 0)),
                  pl.BlockSpec(wo.shape, lambda i, j: (0, 0)),
                  pl.BlockSpec((n_seg, HG_HEADS, HG_DK, HG_DK), st)],
        out_specs=[pl.BlockSpec((ROW_TILE, D_MODEL), orow),
                   pl.BlockSpec((n_seg, HG_HEADS, HG_DK, HG_DK), st)],
        out_shape=[jax.ShapeDtypeStruct((t, D_MODEL), F32),
                   jax.ShapeDtypeStruct((n_seq, HG_HEADS, HG_DK, HG_DK), F32)],
        scratch_shapes=[pltpu.VMEM((n_seg, HG_HEADS, HG_DK, HG_DK), F32),
                        pltpu.VMEM((ROW_TILE, HG_WIDTH), F32),
                        pltpu.VMEM((HG_HEADS, ROW_TILE, ROW_TILE), F32),
                        pltpu.VMEM((ROW_TILE, HG_WIDTH), F32)],
        compiler_params=_cparams(2),
        name="hgrn2",
    )(zh, lb, gn, wo, s0)


def _merge_kernel(gab_ref, ya_ref, yb_ref, x_ref, wo_ref, gf_ref, rw_ref, rb_ref,
                  x1_ref, xg_ref, pos_ref, gate_ref, cnt_ref, off_ref):
    ga = gab_ref[:, :D_MODEL]
    gb = gab_ref[:, D_MODEL:]
    mixed = jax.nn.sigmoid(ga) * ya_ref[...] + jax.nn.sigmoid(gb) * yb_ref[...]
    x1 = x_ref[...] + _dot(mixed.astype(BF16), wo_ref[...])
    x1_ref[...] = x1
    h2 = _rms(x1, gf_ref[...])

    h_hi, h_lo = _split2(h2)
    p_hi = _dot(h_hi, rw_ref[...])
    p_lo = _dot(h_lo, rw_ref[:, :N_EXPERTS])
    logits = p_hi[:, :N_EXPERTS] + p_hi[:, N_EXPERTS:] + p_lo + rb_ref[...]

    rows = logits.shape[0]
    lane = lax.broadcasted_iota(jnp.int32, (rows, N_EXPERTS), 1).astype(F32)
    lane4 = lax.broadcasted_iota(jnp.int32, (rows, TOP_K), 1)
    lg = logits
    sel = jnp.zeros((rows, N_EXPERTS), F32)
    vals, idxs = [], []
    for _ in range(TOP_K):
        m = jnp.max(lg, axis=-1, keepdims=True)
        idx = jnp.min(jnp.where(lg == m, lane, float(N_EXPERTS)), axis=-1, keepdims=True)
        hit = lane == idx
        vals.append(m)
        idxs.append(idx)
        lg = jnp.where(hit, -jnp.inf, lg)
        sel = sel + jnp.where(hit, 1.0, 0.0)

    es = [jnp.exp(vv - vals[0]) for vv in vals]
    tot = es[0] + es[1] + es[2] + es[3]

    ri = lax.broadcasted_iota(jnp.int32, (rows, rows), 0)
    ci = lax.broadcasted_iota(jnp.int32, (rows, rows), 1)
    before = jnp.where(ci < ri, 1.0, 0.0).astype(BF16)
    rank = _dot(before, sel.astype(BF16))
    gran = jnp.floor((jnp.sum(sel, axis=0, keepdims=True) + (GRANULE - 1)) * (1.0 / GRANULE))
    ei = lax.broadcasted_iota(jnp.int32, (N_EXPERTS, N_EXPERTS), 0)
    ej = lax.broadcasted_iota(jnp.int32, (N_EXPERTS, N_EXPERTS), 1)
    off = _dot(gran.astype(BF16), jnp.where(ei < ej, 1.0, 0.0).astype(BF16))
    slot_of = rank + off * float(GRANULE)
    poss = [jnp.sum(jnp.where(lane == idx, slot_of, 0.0), axis=-1, keepdims=True) for idx in idxs]

    slot_iota = lax.broadcasted_iota(jnp.int32, (rows, TILE_SLOTS), 1).astype(F32)
    onehot = jnp.zeros((rows, TILE_SLOTS), F32)
    for kk in range(TOP_K):
        onehot = onehot + jnp.where(slot_iota == poss[kk], 1.0, 0.0)
    xg_ref[...] = _dot_tn(onehot.astype(BF16), h2.astype(BF16))

    pos = jnp.zeros((rows, TOP_K), jnp.int32)
    gates = jnp.zeros((rows, TOP_K), F32)
    for kk in range(TOP_K):
        pos = jnp.where(lane4 == kk, poss[kk].astype(jnp.int32), pos)
        gates = jnp.where(lane4 == kk, es[kk] / tot, gates)
    pos_ref[...] = pos
    gate_ref[...] = gates
    cnt_ref[0] = gran.astype(jnp.int32)
    off_ref[0] = off.astype(jnp.int32)


def _merge(gab, ya, yb, x, wo, g_ffn, rw2, rb):
    t = x.shape[0]
    n_tiles = t // ROW_TILE
    row = lambda i: (i, 0)
    fixed = lambda i: (0, 0)
    per_tile = lambda i: (i, 0, 0)
    return pl.pallas_call(
        _merge_kernel,
        grid=(n_tiles,),
        in_specs=[pl.BlockSpec((ROW_TILE, 2 * D_MODEL), row),
                  pl.BlockSpec((ROW_TILE, D_MODEL), row),
                  pl.BlockSpec((ROW_TILE, D_MODEL), row),
                  pl.BlockSpec((ROW_TILE, D_MODEL), row),
                  pl.BlockSpec((D_MODEL, D_MODEL), fixed),
                  pl.BlockSpec((1, D_MODEL), fixed),
                  pl.BlockSpec((D_MODEL, 2 * N_EXPERTS), fixed),
                  pl.BlockSpec((1, N_EXPERTS), fixed)],
        out_specs=[pl.BlockSpec((ROW_TILE, D_MODEL), row),
                   pl.BlockSpec((TILE_SLOTS, D_MODEL), row),
                   pl.BlockSpec((ROW_TILE, TOP_K), row),
                   pl.BlockSpec((ROW_TILE, TOP_K), row),
                   pl.BlockSpec((1, 1, N_EXPERTS), per_tile),
                   pl.BlockSpec((1, 1, N_EXPERTS), per_tile)],
        out_shape=[jax.ShapeDtypeStruct((t, D_MODEL), F32),
                   jax.ShapeDtypeStruct((n_tiles * TILE_SLOTS, D_MODEL), F32),
                   jax.ShapeDtypeStruct((t, TOP_K), jnp.int32),
                   jax.ShapeDtypeStruct((t, TOP_K), F32),
                   jax.ShapeDtypeStruct((n_tiles, 1, N_EXPERTS), jnp.int32),
                   jax.ShapeDtypeStruct((n_tiles, 1, N_EXPERTS), jnp.int32)],
        compiler_params=_cparams(1),
        name="merge_router",
    )(gab, ya, yb, x, wo, g_ffn, rw2, rb)


def _expert_kernel(blk_e, n_used, src0_ref, srcn_ref, dst_ref, xg_hbm, w1_ref, b1_ref, w2_ref,
                   b2_ref, yt_hbm, xbuf, obuf, gsem, ssem, w1b, w2b):
    b = pl.program_id(0)
    nu = n_used[0]

    def granule_rows(g):
        return pl.ds(pl.multiple_of(g * GRANULE, GRANULE), GRANULE)

    def start_gather(src_ref, slot):
        def body(j, _):
            pltpu.make_async_copy(xg_hbm.at[granule_rows(src_ref[0, 0, j])],
                                  xbuf.at[slot, granule_rows(j)], gsem.at[slot]).start()
            return 0
        lax.fori_loop(0, GRAN_PER_BLOCK, body, 0, unroll=4)

    def start_scatter(slot):
        def body(j, _):
            pltpu.make_async_copy(obuf.at[slot, granule_rows(j)],
                                  yt_hbm.at[granule_rows(dst_ref[0, 0, j])], ssem.at[slot]).start()
            return 0
        lax.fori_loop(0, GRAN_PER_BLOCK, body, 0, unroll=4)

    def wait_scatter(slot):
        pltpu.make_async_copy(obuf.at[slot], yt_hbm.at[pl.ds(0, EXPERT_ROWS)], ssem.at[slot]).wait()

    @pl.when(b == 0)
    def _():
        start_gather(src0_ref, 0)

    @pl.when(b + 1 < nu)
    def _():
        start_gather(srcn_ref, (b + 1) % 2)

    @pl.when((b >= 2) & (b <= nu))
    def _():
        wait_scatter(b % 2)

    @pl.when((b >= 1) & (b == nu))
    def _():
        wait_scatter((b - 1) % 2)

    @pl.when(b < nu)
    def _():
        slot = b % 2
        pltpu.make_async_copy(xg_hbm.at[pl.ds(0, EXPERT_ROWS)], xbuf.at[slot], gsem.at[slot]).wait()

        @pl.when((b == 0) | (blk_e[b] != blk_e[jnp.maximum(b - 1, 0)]))
        def _():
            w1b[...] = w1_ref[0].astype(BF16)
            w2b[...] = w2_ref[0].astype(BF16)

        x = xbuf[slot].astype(BF16)
        u = _dot(x, w1b[...]) + b1_ref[0]
        a = jnp.minimum(u[:, :D_EXPERT], SWIGLU_LIMIT)
        lin = jnp.clip(u[:, D_EXPERT:], -SWIGLU_LIMIT, SWIGLU_LIMIT)
        hmid = a * jax.nn.sigmoid(SWIGLU_ALPHA * a) * (lin + 1.0)
        obuf[slot] = _dot(hmid.astype(BF16), w2b[...]) + b2_ref[0]
        start_scatter(slot)


def _experts(xg, src, dst, blk_e, n_used, w1, b1, w2, b2):
    n_blocks = blk_e.shape[0]
    src3 = src.reshape(n_blocks, 1, GRAN_PER_BLOCK)
    dst3 = dst.reshape(n_blocks, 1, GRAN_PER_BLOCK)
    smem_blk = lambda imap: pl.BlockSpec((1, 1, GRAN_PER_BLOCK), imap, memory_space=pltpu.SMEM)
    grid_spec = pltpu.PrefetchScalarGridSpec(
        num_scalar_prefetch=2,
        grid=(n_blocks,),
        in_specs=[smem_blk(lambda b, be, nu: (0, 0, 0)),
                  smem_blk(lambda b, be, nu: (jnp.minimum(b + 1, n_blocks - 1), 0, 0)),
                  smem_blk(lambda b, be, nu: (b, 0, 0)),
                  pl.BlockSpec(memory_space=pl.ANY),
                  pl.BlockSpec((1, D_MODEL, 2 * D_EXPERT), lambda b, be, nu: (be[b], 0, 0)),
                  pl.BlockSpec((1, 1, 2 * D_EXPERT), lambda b, be, nu: (be[b], 0, 0)),
                  pl.BlockSpec((1, D_EXPERT, D_MODEL), lambda b, be, nu: (be[b], 0, 0)),
                  pl.BlockSpec((1, 1, D_MODEL), lambda b, be, nu: (be[b], 0, 0))],
        out_specs=pl.BlockSpec(memory_space=pl.ANY),
        scratch_shapes=[pltpu.VMEM((2, EXPERT_ROWS, D_MODEL), F32),
                        pltpu.VMEM((2, EXPERT_ROWS, D_MODEL), F32),
                        pltpu.SemaphoreType.DMA((2,)),
                        pltpu.SemaphoreType.DMA((2,)),
                        pltpu.VMEM((D_MODEL, 2 * D_EXPERT), BF16),
                        pltpu.VMEM((D_EXPERT, D_MODEL), BF16)],
    )
    return pl.pallas_call(
        _expert_kernel,
        grid_spec=grid_spec,
        out_shape=jax.ShapeDtypeStruct((xg.shape[0] + EXPERT_ROWS, D_MODEL), F32),
        compiler_params=_cparams(1),
        name="experts",
    )(blk_e, n_used, src3, src3, dst3, xg, w1, b1.reshape(N_EXPERTS, 1, -1), w2,
      b2.reshape(N_EXPERTS, 1, -1))


def _combine_kernel(n_rows, pos_ref, gate_ref, x1_ref, gf_ref, yt_ref, y_ref):
    i = pl.program_id(0)
    rows = pos_ref.shape[0]
    slot_iota = lax.broadcasted_iota(jnp.int32, (rows, TILE_SLOTS), 1)
    pos = pos_ref[...]
    gates = gate_ref[...]
    weights = jnp.zeros((rows, TILE_SLOTS), F32)
    for kk in range(TOP_K):
        weights = weights + jnp.where(slot_iota == pos[:, kk:kk + 1], gates[:, kk:kk + 1], 0.0)
    live = lax.broadcasted_iota(jnp.int32, (TILE_SLOTS, D_MODEL), 0) < n_rows[i]
    yt = jnp.where(live, yt_ref[...], 0.0).astype(BF16)
    y_ref[...] = _rms(x1_ref[...] + _dot(weights.astype(BF16), yt), gf_ref[...])


def _combine(n_rows, pos, gates, x1, g_final, yt):
    t = x1.shape[0]
    row = lambda i, nr: (i, 0)
    grid_spec = pltpu.PrefetchScalarGridSpec(
        num_scalar_prefetch=1,
        grid=(t // ROW_TILE,),
        in_specs=[pl.BlockSpec((ROW_TILE, TOP_K), row),
                  pl.BlockSpec((ROW_TILE, TOP_K), row),
                  pl.BlockSpec((ROW_TILE, D_MODEL), row),
                  pl.BlockSpec((1, D_MODEL), lambda i, nr: (0, 0)),
                  pl.BlockSpec((TILE_SLOTS, D_MODEL), row)],
        out_specs=pl.BlockSpec((ROW_TILE, D_MODEL), row),
    )
    return pl.pallas_call(
        _combine_kernel,
        grid_spec=grid_spec,
        out_shape=jax.ShapeDtypeStruct((t, D_MODEL), F32),
        compiler_params=_cparams(1),
        name="combine",
    )(n_rows, pos, gates, x1, g_final.reshape(1, D_MODEL), yt)


def _granule_plan(gran, goff):
    n_tiles = gran.shape[0]
    max_gran = (n_tiles * ROW_TILE * TOP_K + (GRANULE - 1) * n_tiles * N_EXPERTS) // GRANULE
    n_blocks = max_gran // GRAN_PER_BLOCK + N_EXPERTS + 1
    n_slots = n_blocks * GRAN_PER_BLOCK
    per_e = jnp.sum(gran, axis=0)
    region = (per_e + GRAN_PER_BLOCK - 1) // GRAN_PER_BLOCK * GRAN_PER_BLOCK
    region_end = jnp.cumsum(region)
    region_start = region_end - region
    run_end = jnp.cumsum(gran, axis=0).T
    run_start = run_end - gran.T
    slot = jnp.arange(n_slots, dtype=jnp.int32)
    e_of = jnp.minimum(jnp.sum(slot[:, None] >= region_end[None, :], axis=1), N_EXPERTS - 1)
    local = slot - region_start[e_of]
    valid = local < per_e[e_of]
    tile = jnp.minimum(jnp.sum(run_end[e_of] <= local[:, None], axis=1), n_tiles - 1)
    within = local - run_start[e_of, tile]
    src = tile * GRAN_PER_TILE + goff.T[e_of, tile] + within
    spare = n_tiles * GRAN_PER_TILE + slot % GRAN_PER_BLOCK
    dst = jnp.where(valid, src, spare).astype(jnp.int32)
    src = jnp.where(valid, src, 0).astype(jnp.int32)
    blk_e = e_of[::GRAN_PER_BLOCK].astype(jnp.int32)
    n_used = (region_end[-1:] // GRAN_PER_BLOCK).astype(jnp.int32)
    return src, dst, blk_e, n_used


def kernel(x_prompt, x_sample, state_s5_re, state_s5_im, state_hgrn, g_mix, w_in, s5_lambda_re,
           s5_lambda_im, s5_log_dt, s5_b_re, s5_b_im, s5_c_re, s5_c_im, s5_d, s5_w_glu, s5_w_gate,
           hgrn_lower_bound, hgrn_norm_g, hgrn_w_out, w_out, g_ffn, router_w, router_b, moe_w1,
           moe_b1, moe_w2, moe_b2, g_final):
    depth = g_mix.shape[0]
    assert depth == 1
    li = 0
    n_p, l_p, _ = x_prompt.shape
    n_s, l_s, _ = x_sample.shape
    t_p, t_s = n_p * l_p, n_s * l_s
    t = t_p + t_s

    lb = jnp.cumsum(jax.nn.softmax(hgrn_lower_bound.astype(F32), axis=0), axis=0)[li].reshape(1, -1)
    bmat, cmat, tabs = _s5_tables(s5_lambda_re[li], s5_lambda_im[li], s5_log_dt[li], s5_b_re[li],
                                  s5_b_im[li], s5_c_re[li], s5_c_im[li])
    wg = jnp.concatenate([s5_w_glu[li], s5_w_gate[li]], axis=1).astype(BF16)
    rw_hi = router_w[li].astype(BF16)
    rw_lo = (router_w[li] - rw_hi.astype(F32)).astype(BF16)
    rw2 = jnp.concatenate([rw_hi, rw_lo], axis=1)

    x = jnp.concatenate([x_prompt.reshape(t_p, D_MODEL), x_sample.reshape(t_s, D_MODEL)], axis=0)
    u, zh, gab = _inproj(x, g_mix[li], w_in[li].astype(BF16))

    streams = [
        dict(n_seq=n_p, seq_len=l_p, row_off=0,
             s5r=jnp.zeros((n_p, 1, S5_LANES), F32), s5i=jnp.zeros((n_p, 1, S5_LANES), F32),
             hg=jnp.zeros((n_p, HG_HEADS, HG_DK, HG_DK), F32)),
        dict(n_seq=n_s, seq_len=l_s, row_off=t_p // ROW_TILE,
             s5r=state_s5_re[li].reshape(n_s, 1, S5_LANES).astype(F32),
             s5i=state_s5_im[li].reshape(n_s, 1, S5_LANES).astype(F32),
             hg=state_hgrn[li].astype(F32)),
    ]
    ya, yb, s5r, s5i, hg = [], [], [], [], []
    for s in streams:
        y, a, b = _s5(u, s['s5r'], s['s5i'], bmat, cmat, tabs, s5_d[li].reshape(1, -1), wg,
                      n_seq=s['n_seq'], seq_len=s['seq_len'], row_off=s['row_off'])
        ya.append(y)
        s5r.append(a.reshape(1, s['n_seq'], S5_GROUPS, S5_STATE))
        s5i.append(b.reshape(1, s['n_seq'], S5_GROUPS, S5_STATE))
        y, c = _hgrn(zh, lb, hgrn_norm_g[li].reshape(1, -1), hgrn_w_out[li].astype(BF16), s['hg'],
                     n_seq=s['n_seq'], seq_len=s['seq_len'], row_off=s['row_off'])
        yb.append(y)
        hg.append(c[None])
    ya = jnp.concatenate(ya, axis=0)
    yb = jnp.concatenate(yb, axis=0)

    x1, xg, pos, gates, gran, goff = _merge(
        gab, ya, yb, x, w_out[li].astype(BF16), g_ffn[li].reshape(1, -1), rw2,
        router_b[li].reshape(1, -1).astype(F32))

    src, dst, blk_e, n_used = _granule_plan(gran.reshape(-1, N_EXPERTS), goff.reshape(-1, N_EXPERTS))
    yt = _experts(xg, src, dst, blk_e, n_used, moe_w1[li], moe_b1[li], moe_w2[li], moe_b2[li])
    n_rows = (goff[:, 0, N_EXPERTS - 1] + gran[:, 0, N_EXPERTS - 1]) * GRANULE
    y = _combine(n_rows, pos, gates, x1, g_final, yt)

    dt = x_prompt.dtype
    return (y[:t_p].reshape(n_p, l_p, D_MODEL), y[t_p:].reshape(n_s, l_s, D_MODEL),
            s5r[0].astype(dt), s5i[0].astype(dt), hg[0].astype(dt),
            s5r[1].astype(state_s5_re.dtype), s5i[1].astype(state_s5_im.dtype),
            hg[1].astype(state_hgrn.dtype))
```

```python
import functools
import math

import jax
import jax.numpy as jnp
from jax import lax
from jax.experimental import pallas as pl
from jax.experimental.pallas import tpu as pltpu

F32 = jnp.float32
BF16 = jnp.bfloat16

D_MODEL = 1024
NORM_EPS = 1e-6
S5_WIDTH = 512
S5_GROUP = 16
S5_GROUPS = 32
S5_STATE = 64
S5_LANES = S5_GROUPS * S5_STATE
S5_DT_CLIP = -1e-4
HG_WIDTH = 512
HG_HEADS = 4
HG_DK = 128
N_EXPERTS = 32
TOP_K = 4
D_EXPERT = 1024
SWIGLU_ALPHA = 1.702
SWIGLU_LIMIT = 7.0

ROW_TILE = 256
SCAN_ROWS = 8
S5_SEQS = 4
S5_STEPS = 64
S5_CH_TILE = 128
S5_LANE_TILE = 512
EXPERT_ROWS = 256
GRANULE = 8
TILE_SLOTS = ROW_TILE * TOP_K + N_EXPERTS * GRANULE
GRAN_PER_TILE = TILE_SLOTS // GRANULE
GRAN_PER_BLOCK = EXPERT_ROWS // GRANULE
VMEM_LIMIT = 56 * 1024 * 1024


def _cparams(n_axes):
    return pltpu.CompilerParams(dimension_semantics=("arbitrary",) * n_axes,
                                vmem_limit_bytes=VMEM_LIMIT)


def _rms(x, g):
    r = lax.rsqrt(jnp.mean(x * x, axis=-1, keepdims=True) + NORM_EPS)
    return x * r * g


def _dot(a, b):
    return jnp.dot(a, b, preferred_element_type=F32)


def _dot_nt(a, b):
    return lax.dot_general(a, b, (((1,), (1,)), ((), ())), preferred_element_type=F32)


def _dot_tn(a, b):
    return lax.dot_general(a, b, (((0,), (0,)), ((), ())), preferred_element_type=F32)


def _two_stream_specs(n_first):
    return (pl.BlockSpec((ROW_TILE, D_MODEL), lambda i: (jnp.minimum(i, n_first - 1), 0)),
            pl.BlockSpec((ROW_TILE, D_MODEL), lambda i: (jnp.maximum(i - n_first, 0), 0)))


def _inproj_kernel(xp_ref, xs_ref, g_ref, w_ref, u_ref, zh_ref, gab_ref, *, n_first):
    x = jnp.where(pl.program_id(0) < n_first, xp_ref[...], xs_ref[...])
    h = _rms(x, g_ref[...]).astype(BF16)
    u_ref[...] = _dot(h, w_ref[:, :S5_WIDTH])
    zh_ref[...] = _dot(h, w_ref[:, S5_WIDTH:S5_WIDTH + 4 * HG_WIDTH])
    gab_ref[...] = _dot(h, w_ref[:, S5_WIDTH + 4 * HG_WIDTH:])


def _inproj(xp, xs, g_mix, w_in_bf):
    n_first = xp.shape[0] // ROW_TILE
    t = xp.shape[0] + xs.shape[0]
    ncols = w_in_bf.shape[1]
    row = lambda i: (i, 0)
    fixed = lambda i: (0, 0)
    return pl.pallas_call(
        functools.partial(_inproj_kernel, n_first=n_first),
        grid=(t // ROW_TILE,),
        in_specs=[*_two_stream_specs(n_first),
                  pl.BlockSpec((1, D_MODEL), fixed),
                  pl.BlockSpec((D_MODEL, ncols), fixed)],
        out_specs=[pl.BlockSpec((ROW_TILE, S5_WIDTH), row),
                   pl.BlockSpec((ROW_TILE, 4 * HG_WIDTH), row),
                   pl.BlockSpec((ROW_TILE, 2 * D_MODEL), row)],
        out_shape=[jax.ShapeDtypeStruct((t, S5_WIDTH), F32),
                   jax.ShapeDtypeStruct((t, 4 * HG_WIDTH), F32),
                   jax.ShapeDtypeStruct((t, 2 * D_MODEL), F32)],
        compiler_params=_cparams(1),
        name="inproj",
    )(xp, xs, g_mix.reshape(1, D_MODEL), w_in_bf)


def _s5_kernel(u0_ref, u1_ref, u2_ref, u3_ref, s0_ref, perm_ref, pinv_ref, wb_ref, wc_ref,
               coef_ref, d_ref, wg_ref, ya_ref, sp_out, ss_out, buf, v_ref, *, n_prompt_steps):
    i = pl.program_id(0)

    @pl.when(i == 0)
    def _():
        v_ref[...] = jnp.zeros_like(v_ref)

    @pl.when(i >= n_prompt_steps)
    def _():
        v_ref[...] = s0_ref[0]

    u = jnp.concatenate([r[0] for r in (u0_ref, u1_ref, u2_ref, u3_ref)], axis=0)
    rows = 2 * u.shape[0]
    up = _dot(perm_ref[...], u.astype(BF16))
    re_half = (lax.broadcasted_iota(jnp.int32, (rows, S5_CH_TILE), 0) & S5_SEQS) == 0
    n_tiles = S5_WIDTH // S5_CH_TILE
    for n in range(n_tiles):
        un = up[:, n * S5_CH_TILE:(n + 1) * S5_CH_TILE]
        lhs = jnp.concatenate([jnp.where(re_half, un, 0.0), jnp.where(re_half, 0.0, un)], axis=1)
        buf[:, n * S5_LANE_TILE:(n + 1) * S5_LANE_TILE] = _dot(lhs.astype(BF16), wb_ref[n])

    def step(t, v):
        row = pl.multiple_of(t * SCAN_ROWS, SCAN_ROWS)
        v = (coef_ref[0] * v + coef_ref[1] * pltpu.roll(v, S5_SEQS, axis=0)
             + buf[pl.ds(row, SCAN_ROWS), :])
        buf[pl.ds(row, SCAN_ROWS), :] = v
        return v

    v = lax.fori_loop(0, S5_STEPS, step, v_ref[...], unroll=2)
    v_ref[...] = v

    @pl.when(i < n_prompt_steps)
    def _():
        sp_out[0] = v

    @pl.when(i >= n_prompt_steps)
    def _():
        ss_out[0] = v

    ws = []
    for n in range(n_tiles):
        z = _dot(buf[:, n * S5_LANE_TILE:(n + 1) * S5_LANE_TILE].astype(BF16), wc_ref[n])
        ws.append(jnp.where(re_half, z[:, :S5_CH_TILE], z[:, S5_CH_TILE:]))
    w_hi, w_lo = _split2(jnp.concatenate(ws, axis=1))
    y = _dot(pinv_ref[...], w_hi) + _dot(pinv_ref[...], w_lo) + d_ref[...] * u
    y = jax.nn.gelu(y).astype(BF16)
    p = _dot(y, wg_ref[...])
    ya = p[:, :D_MODEL] * jax.nn.sigmoid(p[:, D_MODEL:])
    ya_ref[0] = ya.reshape(S5_SEQS, S5_STEPS, D_MODEL)


def _s5(u, s0_sample, tables, d_skip, wg, *, prompt_len, n_sample_groups):
    perm, pinv, wb, wc, coef = tables
    n_prompt_steps = prompt_len // S5_STEPS
    n_steps = n_prompt_steps + n_sample_groups
    sample_base = S5_SEQS * n_prompt_steps
    u3 = u.reshape(-1, S5_STEPS, S5_WIDTH)

    def u_spec(s):
        def imap(i):
            prompt_blk = s * n_prompt_steps + i
            sample_blk = sample_base + S5_SEQS * (i - n_prompt_steps) + s
            return (jnp.where(i < n_prompt_steps, prompt_blk, sample_blk), 0, 0)
        return pl.BlockSpec((1, S5_STEPS, S5_WIDTH), imap)

    sample_grp = lambda i: (jnp.maximum(i - n_prompt_steps, 0), 0, 0)
    whole = lambda a: pl.BlockSpec(a.shape, lambda i: (0,) * a.ndim)
    state_blk = (1, 2 * S5_SEQS, S5_LANES)
    rows = S5_SEQS * S5_STEPS
    return pl.pallas_call(
        functools.partial(_s5_kernel, n_prompt_steps=n_prompt_steps),
        grid=(n_steps,),
        in_specs=[u_spec(0), u_spec(1), u_spec(2), u_spec(3),
                  pl.BlockSpec(state_blk, sample_grp),
                  whole(perm), whole(pinv), whole(wb), whole(wc), whole(coef), whole(d_skip), whole(wg)],
        out_specs=[pl.BlockSpec((1, S5_SEQS, S5_STEPS, D_MODEL), lambda i: (i, 0, 0, 0)),
                   pl.BlockSpec(state_blk, lambda i: (0, 0, 0)),
                   pl.BlockSpec(state_blk, sample_grp)],
        out_shape=[jax.ShapeDtypeStruct((n_steps, S5_SEQS, S5_STEPS, D_MODEL), F32),
                   jax.ShapeDtypeStruct(state_blk, F32),
                   jax.ShapeDtypeStruct((n_sample_groups,) + state_blk[1:], F32)],
        scratch_shapes=[pltpu.VMEM((2 * rows, S5_LANES), F32),
                        pltpu.VMEM((2 * S5_SEQS, S5_LANES), F32)],
        compiler_params=_cparams(1),
        name="s5",
    )(u3, u3, u3, u3, s0_sample, perm, pinv, wb, wc, coef, d_skip, wg)


def _s5_tables(lam_re, lam_im, log_dt, b_re, b_im, c_re, c_im):
    lr = jnp.minimum(lam_re.astype(F32), S5_DT_CLIP)
    li = lam_im.astype(F32)
    dt = jnp.exp(log_dt.astype(F32))[:, None]
    mag = jnp.exp(lr * dt)
    ar = mag * jnp.cos(li * dt)
    ai = mag * jnp.sin(li * dt)
    den = lr * lr + li * li
    zr = ((ar - 1.0) * lr + ai * li) / den
    zi = (ai * lr - (ar - 1.0) * li) / den
    br = b_re.astype(F32)
    bi = b_im.astype(F32)
    bbr = zr[..., None] * br - zi[..., None] * bi
    bbi = zr[..., None] * bi + zi[..., None] * br

    ar_f, ai_f = ar.reshape(1, -1), ai.reshape(1, -1)
    sign = jnp.where(jnp.arange(2 * S5_SEQS)[:, None] < S5_SEQS, -1.0, 1.0)
    coef = jnp.stack([jnp.broadcast_to(ar_f, (2 * S5_SEQS, S5_LANES)), sign * ai_f]).astype(F32)

    g_per_tile = S5_CH_TILE // S5_GROUP
    n_tiles = S5_GROUPS // g_per_tile
    eye = jnp.eye(g_per_tile, dtype=F32)

    def b_tile(bb):
        x = bb.reshape(n_tiles, g_per_tile, S5_STATE, S5_GROUP)
        x = x[:, :, :, :, None] * eye[None, :, None, None, :]
        return x.transpose(0, 1, 3, 4, 2).reshape(n_tiles, S5_CH_TILE, S5_LANE_TILE)

    wb = jnp.concatenate([b_tile(bbr), b_tile(bbi)], axis=1).astype(BF16)

    def c_tile(cc):
        x = cc.astype(F32).reshape(n_tiles, g_per_tile, S5_GROUP, S5_STATE)
        x = x[:, :, :, :, None] * eye[None, :, None, None, :]
        return x.transpose(0, 1, 3, 4, 2).reshape(n_tiles, S5_LANE_TILE, S5_CH_TILE)

    wc = jnp.concatenate([c_tile(c_re), -c_tile(c_im)], axis=2).astype(BF16)

    r = jnp.arange(2 * S5_SEQS * S5_STEPS)
    src = (r % S5_SEQS) * S5_STEPS + r // (2 * S5_SEQS)
    perm = (src[:, None] == jnp.arange(S5_SEQS * S5_STEPS)[None, :]).astype(BF16)
    return perm, perm.T, wb, wc, coef


def _split2(x):
    hi = x.astype(BF16)
    lo = (x - hi.astype(F32)).astype(BF16)
    return hi, lo


def _split3(x):
    hi = x.astype(BF16)
    r = x - hi.astype(F32)
    mid = r.astype(BF16)
    lo = (r - mid.astype(F32)).astype(BF16)
    return hi, mid, lo


def _hgrn_kernel(zh_ref, lb_ref, gn_ref, wo_ref, s0_ref, yb_ref, sp_out, ss_out,
                 st, gc_ref, att_ref, o_ref, *, n_prompt_tiles, chunks_per_seq, sample_len):
    i = pl.program_id(0)
    refs = (zh_ref, lb_ref, gn_ref, wo_ref, yb_ref, st, gc_ref, att_ref, o_ref)

    @pl.when(i < n_prompt_tiles)
    def _():
        j = i % chunks_per_seq

        @pl.when(j == 0)
        def _():
            st[0] = jnp.zeros((HG_HEADS, HG_DK, HG_DK), F32)

        _hgrn_tile(*refs, n_seg=1, seg_len=ROW_TILE)

        @pl.when(j == chunks_per_seq - 1)
        def _():
            for h in range(HG_HEADS):
                sp_out[0, h] = st[0, h].T

    @pl.when(i >= n_prompt_tiles)
    def _():
        n_seg = ROW_TILE // sample_len
        for seg in range(n_seg):
            for h in range(HG_HEADS):
                st[seg, h] = s0_ref[seg, h].T
        _hgrn_tile(*refs, n_seg=n_seg, seg_len=sample_len)
        for seg in range(n_seg):
            for h in range(HG_HEADS):
                ss_out[seg, h] = st[seg, h].T


def _hgrn_tile(zh_ref, lb_ref, gn_ref, wo_ref, yb_ref, st, gc_ref, att_ref, o_ref, *, n_seg, seg_len):
    rows = n_seg * seg_len
    zq = zh_ref[:, 0:HG_WIDTH]
    zf = zh_ref[:, HG_WIDTH:2 * HG_WIDTH]
    v = zh_ref[:, 2 * HG_WIDTH:3 * HG_WIDTH]
    lb = lb_ref[...]
    q = zq * jax.nn.sigmoid(zq)
    f = lb + (1.0 - lb) * jax.nn.sigmoid(zf)
    k = 1.0 - f
    g = jnp.log(f)
    vb = v.astype(BF16)

    ri = lax.broadcasted_iota(jnp.int32, (rows, rows), 0)
    ci = lax.broadcasted_iota(jnp.int32, (rows, rows), 1)
    seg_shift = int(math.log2(seg_len))
    same_seg = (ri >> seg_shift) == (ci >> seg_shift)

    tri = jnp.where(same_seg & (ci <= ri), 1.0, 0.0).astype(BF16)
    g3 = jnp.concatenate(_split3(g), axis=1)
    gcs = _dot(tri, g3)
    gc = gcs[:, :HG_WIDTH] + gcs[:, HG_WIDTH:2 * HG_WIDTH] + gcs[:, 2 * HG_WIDTH:]
    gc_ref[...] = gc

    xor = jnp.where(ci < ri, ri ^ ci, 0)
    qb = q.astype(BF16)
    kb = k.astype(BF16)
    for h in range(HG_HEADS):
        hs = slice(h * HG_DK, (h + 1) * HG_DK)
        att_ref[h] = jnp.where(ri == ci, _dot_nt(qb[:, hs], kb[:, hs]), 0.0)

    level = seg_len // 2
    while level >= 1:
        if 2 * level >= SCAN_ROWS:
            blocks = []
            for b in range(rows // (2 * level)):
                mid = b * 2 * level + level - 1
                blocks.append(jnp.broadcast_to(gc_ref[mid:mid + 1, :], (2 * level, HG_WIDTH)))
            gmid = blocks[0] if len(blocks) == 1 else jnp.concatenate(blocks, axis=0)
            dexp = -jnp.abs(gc - gmid)
        else:
            mid = (ri & ~(2 * level - 1)) + (level - 1)
            upper = (ri & level) != 0
            w = (ci > jnp.where(upper, mid, ri)) & (ci <= jnp.where(upper, ri, mid))
            g2 = jnp.concatenate(_split2(g), axis=1)
            ds = _dot(jnp.where(w, 1.0, 0.0).astype(BF16), g2)
            dexp = ds[:, :HG_WIDTH] + ds[:, HG_WIDTH:]
        e = jnp.exp(dexp)
        ql = (q * e).astype(BF16)
        kl = (k * e).astype(BF16)
        lshift = int(math.log2(level))
        sel = (xor >> lshift) == 1
        for h in range(HG_HEADS):
            hs = slice(h * HG_DK, (h + 1) * HG_DK)
            att_ref[h] += jnp.where(sel, _dot_nt(ql[:, hs], kl[:, hs]), 0.0)
        level //= 2

    qg = (q * jnp.exp(gc)).astype(BF16)
    for h in range(HG_HEADS):
        hs = slice(h * HG_DK, (h + 1) * HG_DK)
        o_ref[:, hs] = _dot(att_ref[h].astype(BF16), vb[:, hs])

    for seg in range(n_seg):
        rs = slice(seg * seg_len, (seg + 1) * seg_len)
        last = (seg + 1) * seg_len - 1
        g_last = gc_ref[last:last + 1, :]
        kd = (k[rs] * jnp.exp(g_last - gc[rs])).astype(BF16)
        decay = jnp.exp(g_last)
        for h in range(HG_HEADS):
            hs = slice(h * HG_DK, (h + 1) * HG_DK)
            s_t = st[seg, h]
            o_ref[rs, hs] += _dot_nt(qg[rs, hs], s_t.astype(BF16))
            st[seg, h] = s_t * decay[:, hs] + _dot_tn(vb[rs, hs], kd[:, hs])

    zg = zh_ref[:, 3 * HG_WIDTH:]
    gate = zg * jax.nn.sigmoid(zg)
    outs = []
    for h in range(HG_HEADS):
        hs = slice(h * HG_DK, (h + 1) * HG_DK)
        outs.append(_rms(o_ref[:, hs], gn_ref[...]) * gate[:, hs])
    yb_ref[...] = _dot(jnp.concatenate(outs, axis=1).astype(BF16), wo_ref[...])


def _hgrn(zh, lb, gn, wo, s0_sample, *, n_prompt, prompt_len, sample_len):
    t = zh.shape[0]
    chunks_per_seq = prompt_len // ROW_TILE
    n_prompt_tiles = n_prompt * chunks_per_seq
    n_seg = ROW_TILE // sample_len
    n_sample = s0_sample.shape[0]
    row = lambda i: (i, 0)
    fixed = lambda i: (0, 0)
    state = (HG_HEADS, HG_DK, HG_DK)
    prompt_seq = lambda i: (jnp.minimum(i // chunks_per_seq, n_prompt - 1), 0, 0, 0)
    sample_grp = lambda i: (jnp.maximum(i - n_prompt_tiles, 0), 0, 0, 0)
    kern = functools.partial(_hgrn_kernel, n_prompt_tiles=n_prompt_tiles,
                             chunks_per_seq=chunks_per_seq, sample_len=sample_len)
    return pl.pallas_call(
        kern,
        grid=(t // ROW_TILE,),
        in_specs=[pl.BlockSpec((ROW_TILE, 4 * HG_WIDTH), row),
                  pl.BlockSpec((1, HG_WIDTH), fixed),
                  pl.BlockSpec((1, HG_DK), fixed),
                  pl.BlockSpec(wo.shape, fixed),
                  pl.BlockSpec((n_seg,) + state, sample_grp)],
        out_specs=[pl.BlockSpec((ROW_TILE, D_MODEL), row),
                   pl.BlockSpec((1,) + state, prompt_seq),
                   pl.BlockSpec((n_seg,) + state, sample_grp)],
        out_shape=[jax.ShapeDtypeStruct((t, D_MODEL), F32),
                   jax.ShapeDtypeStruct((n_prompt,) + state, F32),
                   jax.ShapeDtypeStruct((n_sample,) + state, F32)],
        scratch_shapes=[pltpu.VMEM((n_seg,) + state, F32),
                        pltpu.VMEM((ROW_TILE, HG_WIDTH), F32),
                        pltpu.VMEM((HG_HEADS, ROW_TILE, ROW_TILE), F32),
                        pltpu.VMEM((ROW_TILE, HG_WIDTH), F32)],
        compiler_params=_cparams(1),
        name="hgrn2",
    )(zh, lb, gn, wo, s0_sample)


def _merge_kernel(gab_ref, ya0_ref, ya1_ref, ya2_ref, ya3_ref, yb_ref, xp_ref, xs_ref, wo_ref,
                  gf_ref, rw_ref, rb_ref, x1_ref, xg_ref, pos_ref, gate_ref, cnt_ref, off_ref,
                  *, n_first):
    ga = gab_ref[:, :D_MODEL]
    gb = gab_ref[:, D_MODEL:]
    ya = jnp.concatenate([r[0] for r in (ya0_ref, ya1_ref, ya2_ref, ya3_ref)], axis=0)
    mixed = jax.nn.sigmoid(ga) * ya + jax.nn.sigmoid(gb) * yb_ref[...]
    x = jnp.where(pl.program_id(0) < n_first, xp_ref[...], xs_ref[...])
    x1 = x + _dot(mixed.astype(BF16), wo_ref[...])
    x1_ref[...] = x1
    h2 = _rms(x1, gf_ref[...])

    h_hi, h_lo = _split2(h2)
    p_hi = _dot(h_hi, rw_ref[...])
    p_lo = _dot(h_lo, rw_ref[:, :N_EXPERTS])
    logits = p_hi[:, :N_EXPERTS] + p_hi[:, N_EXPERTS:] + p_lo + rb_ref[...]

    rows = logits.shape[0]
    lane = lax.broadcasted_iota(jnp.int32, (rows, N_EXPERTS), 1).astype(F32)
    lane4 = lax.broadcasted_iota(jnp.int32, (rows, TOP_K), 1)
    lg = logits
    sel = jnp.zeros((rows, N_EXPERTS), F32)
    vals, idxs = [], []
    for _ in range(TOP_K):
        m = jnp.max(lg, axis=-1, keepdims=True)
        idx = jnp.min(jnp.where(lg == m, lane, float(N_EXPERTS)), axis=-1, keepdims=True)
        hit = lane == idx
        vals.append(m)
        idxs.append(idx)
        lg = jnp.where(hit, -jnp.inf, lg)
        sel = sel + jnp.where(hit, 1.0, 0.0)

    es = [jnp.exp(vv - vals[0]) for vv in vals]
    tot = es[0] + es[1] + es[2] + es[3]

    ri = lax.broadcasted_iota(jnp.int32, (rows, rows), 0)
    ci = lax.broadcasted_iota(jnp.int32, (rows, rows), 1)
    before = jnp.where(ci < ri, 1.0, 0.0).astype(BF16)
    rank = _dot(before, sel.astype(BF16))
    gran = jnp.floor((jnp.sum(sel, axis=0, keepdims=True) + (GRANULE - 1)) * (1.0 / GRANULE))
    ei = lax.broadcasted_iota(jnp.int32, (N_EXPERTS, N_EXPERTS), 0)
    ej = lax.broadcasted_iota(jnp.int32, (N_EXPERTS, N_EXPERTS), 1)
    off = _dot(gran.astype(BF16), jnp.where(ei < ej, 1.0, 0.0).astype(BF16))
    slot_of = rank + off * float(GRANULE)
    poss = [jnp.sum(jnp.where(lane == idx, slot_of, 0.0), axis=-1, keepdims=True) for idx in idxs]

    slot_iota = lax.broadcasted_iota(jnp.int32, (rows, TILE_SLOTS), 1).astype(F32)
    onehot = jnp.zeros((rows, TILE_SLOTS), F32)
    for kk in range(TOP_K):
        onehot = onehot + jnp.where(slot_iota == poss[kk], 1.0, 0.0)
    xg_ref[...] = _dot_tn(onehot.astype(BF16), h2.astype(BF16))

    pos = jnp.zeros((rows, TOP_K), jnp.int32)
    gates = jnp.zeros((rows, TOP_K), F32)
    for kk in range(TOP_K):
        pos = jnp.where(lane4 == kk, poss[kk].astype(jnp.int32), pos)
        gates = jnp.where(lane4 == kk, es[kk] / tot, gates)
    pos_ref[...] = pos
    gate_ref[...] = gates
    cnt_ref[0] = gran.astype(jnp.int32)
    off_ref[0] = off.astype(jnp.int32)


def _merge(gab, ya_steps, yb, xp, xs, wo, g_ffn, rw2, rb, *, prompt_len):
    n_first = xp.shape[0] // ROW_TILE
    t = xp.shape[0] + xs.shape[0]
    n_tiles = t // ROW_TILE
    chunks_per_seq = prompt_len // ROW_TILE
    quarters = ROW_TILE // S5_STEPS
    ya3 = ya_steps.reshape(-1, S5_STEPS, D_MODEL)

    def ya_spec(q):
        def imap(i):
            seq, chunk = i // chunks_per_seq, i % chunks_per_seq
            prompt_blk = (chunk * quarters + q) * S5_SEQS + seq
            return (jnp.where(i < n_first, prompt_blk, i * quarters + q), 0, 0)
        return pl.BlockSpec((1, S5_STEPS, D_MODEL), imap)

    row = lambda i: (i, 0)
    fixed = lambda i: (0, 0)
    per_tile = lambda i: (i, 0, 0)
    return pl.pallas_call(
        functools.partial(_merge_kernel, n_first=n_first),
        grid=(n_tiles,),
        in_specs=[pl.BlockSpec((ROW_TILE, 2 * D_MODEL), row),
                  ya_spec(0), ya_spec(1), ya_spec(2), ya_spec(3),
                  pl.BlockSpec((ROW_TILE, D_MODEL), row),
                  *_two_stream_specs(n_first),
                  pl.BlockSpec((D_MODEL, D_MODEL), fixed),
                  pl.BlockSpec((1, D_MODEL), fixed),
                  pl.BlockSpec((D_MODEL, 2 * N_EXPERTS), fixed),
                  pl.BlockSpec((1, N_EXPERTS), fixed)],
        out_specs=[pl.BlockSpec((ROW_TILE, D_MODEL), row),
                   pl.BlockSpec((TILE_SLOTS, D_MODEL), row),
                   pl.BlockSpec((ROW_TILE, TOP_K), row),
                   pl.BlockSpec((ROW_TILE, TOP_K), row),
                   pl.BlockSpec((1, 1, N_EXPERTS), per_tile),
                   pl.BlockSpec((1, 1, N_EXPERTS), per_tile)],
        out_shape=[jax.ShapeDtypeStruct((t, D_MODEL), F32),
                   jax.ShapeDtypeStruct((n_tiles * TILE_SLOTS, D_MODEL), F32),
                   jax.ShapeDtypeStruct((t, TOP_K), jnp.int32),
                   jax.ShapeDtypeStruct((t, TOP_K), F32),
                   jax.ShapeDtypeStruct((n_tiles, 1, N_EXPERTS), jnp.int32),
                   jax.ShapeDtypeStruct((n_tiles, 1, N_EXPERTS), jnp.int32)],
        compiler_params=_cparams(1),
        name="merge_router",
    )(gab, ya3, ya3, ya3, ya3, yb, xp, xs, wo, g_ffn, rw2, rb)


def _expert_kernel(blk_e, nxt_e, n_used, n_real, src0_ref, srcn_ref, dst_ref, xg_hbm, zero_hbm,
                   w1_hbm, b1_ref, w2_hbm, b2_ref, yt_hbm, xbuf, obuf, gsem, ssem, w1s, w2s, wsem,
                   w1b, w2b):
    b = pl.program_id(0)
    nu = n_used[0]

    def granule_rows(g):
        return pl.ds(pl.multiple_of(g * GRANULE, GRANULE), GRANULE)

    def start_gather(src_ref, blk, slot):
        def real(j, _):
            pltpu.make_async_copy(xg_hbm.at[granule_rows(src_ref[0, 0, j])],
                                  xbuf.at[slot, granule_rows(j)], gsem.at[slot]).start()
            return 0

        def pad(j, _):
            pltpu.make_async_copy(zero_hbm, xbuf.at[slot, granule_rows(j)], gsem.at[slot]).start()
            return 0
        lax.fori_loop(0, n_real[blk], real, 0)
        lax.fori_loop(n_real[blk], GRAN_PER_BLOCK, pad, 0)

    def scatter_copy(slot, j, g):
        return pltpu.make_async_copy(obuf.at[slot, granule_rows(j)], yt_hbm.at[granule_rows(g)],
                                     ssem.at[slot])

    def start_scatter(blk, slot):
        def body(j, _):
            scatter_copy(slot, j, dst_ref[0, 0, j]).start()
            return 0
        lax.fori_loop(0, n_real[blk], body, 0)

    def wait_scatter(blk, slot):
        def body(j, _):
            scatter_copy(slot, j, 0).wait()
            return 0
        lax.fori_loop(0, n_real[blk], body, 0)

    def weight_copies(e):
        return (pltpu.make_async_copy(w1_hbm.at[e], w1s, wsem.at[0]),
                pltpu.make_async_copy(w2_hbm.at[e], w2s, wsem.at[1]))

    first = b == 0
    new_expert = first | (blk_e[b] != blk_e[jnp.maximum(b - 1, 0)])

    @pl.when(first)
    def _():
        start_gather(src0_ref, 0, 0)
        for c in weight_copies(blk_e[0]):
            c.start()

    @pl.when(b + 1 < nu)
    def _():
        start_gather(srcn_ref, b + 1, (b + 1) % 2)

    @pl.when((b >= 2) & (b <= nu))
    def _():
        wait_scatter(b - 2, b % 2)

    @pl.when((b >= 1) & (b == nu))
    def _():
        wait_scatter(b - 1, (b - 1) % 2)

    @pl.when(b < nu)
    def _():
        slot = b % 2
        pltpu.make_async_copy(xg_hbm.at[pl.ds(0, EXPERT_ROWS)], xbuf.at[slot], gsem.at[slot]).wait()

        @pl.when(new_expert)
        def _():
            for c in weight_copies(blk_e[b]):
                c.wait()
            w1b[...] = w1s[...].astype(BF16)
            w2b[...] = w2s[...].astype(BF16)

            @pl.when(nxt_e[b] >= 0)
            def _():
                for c in weight_copies(nxt_e[b]):
                    c.start()

        x = xbuf[slot].astype(BF16)
        u = _dot(x, w1b[...]) + b1_ref[0]
        a = jnp.minimum(u[:, :D_EXPERT], SWIGLU_LIMIT)
        lin = jnp.clip(u[:, D_EXPERT:], -SWIGLU_LIMIT, SWIGLU_LIMIT)
        hmid = a * jax.nn.sigmoid(SWIGLU_ALPHA * a) * (lin + 1.0)
        obuf[slot] = _dot(hmid.astype(BF16), w2b[...]) + b2_ref[0]
        start_scatter(b, slot)


def _experts(xg, plan, w1, b1, w2, b2):
    src, dst, blk_e, nxt_e, n_used, n_real = plan
    n_blocks = blk_e.shape[0]
    src3 = src.reshape(n_blocks, 1, GRAN_PER_BLOCK)
    dst3 = dst.reshape(n_blocks, 1, GRAN_PER_BLOCK)
    smem_blk = lambda imap: pl.BlockSpec((1, 1, GRAN_PER_BLOCK), imap, memory_space=pltpu.SMEM)
    by_expert = lambda b, be, *_: (be[b], 0, 0)
    grid_spec = pltpu.PrefetchScalarGridSpec(
        num_scalar_prefetch=4,
        grid=(n_blocks,),
        in_specs=[smem_blk(lambda b, *_: (0, 0, 0)),
                  smem_blk(lambda b, *_: (jnp.minimum(b + 1, n_blocks - 1), 0, 0)),
                  smem_blk(lambda b, *_: (b, 0, 0)),
                  pl.BlockSpec(memory_space=pl.ANY),
                  pl.BlockSpec(memory_space=pl.ANY),
                  pl.BlockSpec(memory_space=pl.ANY),
                  pl.BlockSpec((1, 1, 2 * D_EXPERT), by_expert),
                  pl.BlockSpec(memory_space=pl.ANY),
                  pl.BlockSpec((1, 1, D_MODEL), by_expert)],
        out_specs=pl.BlockSpec(memory_space=pl.ANY),
        scratch_shapes=[pltpu.VMEM((2, EXPERT_ROWS, D_MODEL), F32),
                        pltpu.VMEM((2, EXPERT_ROWS, D_MODEL), F32),
                        pltpu.SemaphoreType.DMA((2,)),
                        pltpu.SemaphoreType.DMA((2,)),
                        pltpu.VMEM((D_MODEL, 2 * D_EXPERT), F32),
                        pltpu.VMEM((D_EXPERT, D_MODEL), F32),
                        pltpu.SemaphoreType.DMA((2,)),
                        pltpu.VMEM((D_MODEL, 2 * D_EXPERT), BF16),
                        pltpu.VMEM((D_EXPERT, D_MODEL), BF16)],
    )
    return pl.pallas_call(
        _expert_kernel,
        grid_spec=grid_spec,
        out_shape=jax.ShapeDtypeStruct(xg.shape, F32),
        input_output_aliases={7: 0},
        compiler_params=_cparams(1),
        name="experts",
    )(blk_e, nxt_e, n_used, n_real, src3, src3, dst3, xg, jnp.zeros((GRANULE, D_MODEL), F32),
      w1, b1.reshape(N_EXPERTS, 1, -1), w2, b2.reshape(N_EXPERTS, 1, -1))


def _combine_kernel(pos_ref, gate_ref, x1_ref, gf_ref, yt_ref, y_ref):
    rows = pos_ref.shape[0]
    slot_iota = lax.broadcasted_iota(jnp.int32, (rows, TILE_SLOTS), 1)
    pos = pos_ref[...]
    gates = gate_ref[...]
    weights = jnp.zeros((rows, TILE_SLOTS), F32)
    for kk in range(TOP_K):
        weights = weights + jnp.where(slot_iota == pos[:, kk:kk + 1], gates[:, kk:kk + 1], 0.0)
    moe = _dot(weights.astype(BF16), yt_ref[...].astype(BF16))
    y_ref[...] = _rms(x1_ref[...] + moe, gf_ref[...])


def _combine(pos, gates, x1, g_final, yt, *, tile0, n_tiles):
    row = lambda i: (tile0 + i, 0)
    return pl.pallas_call(
        _combine_kernel,
        grid=(n_tiles,),
        in_specs=[pl.BlockSpec((ROW_TILE, TOP_K), row),
                  pl.BlockSpec((ROW_TILE, TOP_K), row),
                  pl.BlockSpec((ROW_TILE, D_MODEL), row),
                  pl.BlockSpec((1, D_MODEL), lambda i: (0, 0)),
                  pl.BlockSpec((TILE_SLOTS, D_MODEL), row)],
        out_specs=pl.BlockSpec((ROW_TILE, D_MODEL), lambda i: (i, 0)),
        out_shape=jax.ShapeDtypeStruct((n_tiles * ROW_TILE, D_MODEL), F32),
        compiler_params=_cparams(1),
        name="combine",
    )(pos, gates, x1, g_final.reshape(1, D_MODEL), yt)


def _granule_plan(gran, goff):
    n_tiles = gran.shape[0]
    max_gran = (n_tiles * ROW_TILE * TOP_K + (GRANULE - 1) * n_tiles * N_EXPERTS) // GRANULE
    n_blocks = max_gran // GRAN_PER_BLOCK + N_EXPERTS + 1
    n_slots = n_blocks * GRAN_PER_BLOCK
    i32 = jnp.int32
    per_e = jnp.sum(gran, axis=0)
    region = (per_e + GRAN_PER_BLOCK - 1) // GRAN_PER_BLOCK * GRAN_PER_BLOCK
    region_end = jnp.cumsum(region)
    region_start = region_end - region
    g_t = gran.T
    run_start = (region_start[:, None] + jnp.cumsum(g_t, axis=1) - g_t).reshape(-1)
    run_end = run_start + g_t.reshape(-1)
    first_gran = (jnp.arange(n_tiles, dtype=i32)[None, :] * GRAN_PER_TILE + goff.T).reshape(-1)
    shift = first_gran - run_start
    d_shift = jnp.diff(shift, prepend=0)
    d_end = jnp.diff(run_end, prepend=0)
    slot = jnp.arange(n_slots, dtype=i32)
    started = slot[:, None] >= run_start[None, :]
    granule = slot + jnp.sum(jnp.where(started, d_shift[None, :], 0), axis=1)
    valid = slot < jnp.sum(jnp.where(started, d_end[None, :], 0), axis=1)
    granule = jnp.where(valid, granule, 0).astype(i32)
    n_real = jnp.sum(valid.reshape(n_blocks, GRAN_PER_BLOCK), axis=1).astype(i32)
    blk = jnp.arange(n_blocks, dtype=i32) * GRAN_PER_BLOCK
    blk_e = jnp.minimum(jnp.sum(blk[:, None] >= region_end[None, :], axis=1), N_EXPERTS - 1).astype(i32)
    e = jnp.arange(N_EXPERTS, dtype=i32)
    later = jnp.where((e[None, :] > e[:, None]) & (per_e[None, :] > 0), e[None, :], N_EXPERTS)
    nxt = jnp.min(later, axis=1)
    nxt = jnp.where(nxt == N_EXPERTS, -1, nxt)
    nxt_e = jnp.sum(jnp.where(blk_e[:, None] == e[None, :], nxt[None, :], 0), axis=1).astype(i32)
    n_used = (region_end[-1:] // GRAN_PER_BLOCK).astype(i32)
    return granule, granule, blk_e, nxt_e, n_used, n_real


def kernel(x_prompt, x_sample, state_s5_re, state_s5_im, state_hgrn, g_mix, w_in, s5_lambda_re,
           s5_lambda_im, s5_log_dt, s5_b_re, s5_b_im, s5_c_re, s5_c_im, s5_d, s5_w_glu, s5_w_gate,
           hgrn_lower_bound, hgrn_norm_g, hgrn_w_out, w_out, g_ffn, router_w, router_b, moe_w1,
           moe_b1, moe_w2, moe_b2, g_final):
    depth = g_mix.shape[0]
    n_p, l_p, _ = x_prompt.shape
    n_s, l_s, _ = x_sample.shape
    assert depth == 1 and n_p == S5_SEQS and l_p % ROW_TILE == 0
    assert l_s == S5_STEPS and n_s % S5_SEQS == 0 and ROW_TILE % l_s == 0
    li = 0
    t_p, t_s = n_p * l_p, n_s * l_s
    n_groups = n_s // S5_SEQS

    lb = jnp.cumsum(jax.nn.softmax(hgrn_lower_bound.astype(F32), axis=0), axis=0)[li].reshape(1, -1)
    s5_tabs = _s5_tables(s5_lambda_re[li], s5_lambda_im[li], s5_log_dt[li], s5_b_re[li],
                         s5_b_im[li], s5_c_re[li], s5_c_im[li])
    wg = jnp.concatenate([s5_w_glu[li], s5_w_gate[li]], axis=1).astype(BF16)
    rw_hi = router_w[li].astype(BF16)
    rw_lo = (router_w[li] - rw_hi.astype(F32)).astype(BF16)
    rw2 = jnp.concatenate([rw_hi, rw_lo], axis=1)

    xp = x_prompt.reshape(t_p, D_MODEL)
    xs = x_sample.reshape(t_s, D_MODEL)
    u, zh, gab = _inproj(xp, xs, g_mix[li], w_in[li].astype(BF16))

    s0_s5 = jnp.concatenate([state_s5_re[li].reshape(n_groups, S5_SEQS, S5_LANES),
                             state_s5_im[li].reshape(n_groups, S5_SEQS, S5_LANES)], axis=1).astype(F32)
    ya, s5_p, s5_s = _s5(u, s0_s5, s5_tabs, s5_d[li].reshape(1, -1), wg,
                         prompt_len=l_p, n_sample_groups=n_groups)
    yb, hg_p, hg_s = _hgrn(zh, lb, hgrn_norm_g[li].reshape(1, -1), hgrn_w_out[li].astype(BF16),
                           state_hgrn[li].astype(F32), n_prompt=n_p, prompt_len=l_p, sample_len=l_s)

    x1, xg, pos, gates, gran, goff = _merge(
        gab, ya, yb, xp, xs, w_out[li].astype(BF16), g_ffn[li].reshape(1, -1), rw2,
        router_b[li].reshape(1, -1).astype(F32), prompt_len=l_p)

    plan = _granule_plan(gran.reshape(-1, N_EXPERTS), goff.reshape(-1, N_EXPERTS))
    yt = _experts(xg, plan, moe_w1[li], moe_b1[li], moe_w2[li], moe_b2[li])
    n_pt = t_p // ROW_TILE
    y_p = _combine(pos, gates, x1, g_final, yt, tile0=0, n_tiles=n_pt)
    y_s = _combine(pos, gates, x1, g_final, yt, tile0=n_pt, n_tiles=t_s // ROW_TILE)

    def s5_out(st, n, dtype):
        re = st[:, :S5_SEQS].reshape(1, n, S5_GROUPS, S5_STATE).astype(dtype)
        im = st[:, S5_SEQS:].reshape(1, n, S5_GROUPS, S5_STATE).astype(dtype)
        return re, im

    dt = x_prompt.dtype
    p_re, p_im = s5_out(s5_p, n_p, dt)
    s_re, _ = s5_out(s5_s, n_s, state_s5_re.dtype)
    _, s_im = s5_out(s5_s, n_s, state_s5_im.dtype)
    return (y_p.reshape(n_p, l_p, D_MODEL), y_s.reshape(n_s, l_s, D_MODEL),
            p_re, p_im, hg_p[None].astype(dt),
            s_re, s_im, hg_s[None].astype(state_hgrn.dtype))
```

```python
import functools
import math

import jax
import jax.numpy as jnp
from jax import lax
from jax.experimental import pallas as pl
from jax.experimental.pallas import tpu as pltpu

F32 = jnp.float32
BF16 = jnp.bfloat16

D_MODEL = 1024
NORM_EPS = 1e-6
S5_WIDTH = 512
S5_GROUP = 16
S5_GROUPS = 32
S5_STATE = 64
S5_LANES = S5_GROUPS * S5_STATE
S5_DT_CLIP = -1e-4
HG_WIDTH = 512
HG_HEADS = 4
HG_DK = 128
N_EXPERTS = 32
TOP_K = 4
D_EXPERT = 1024
SWIGLU_ALPHA = 1.702
SWIGLU_LIMIT = 7.0

ROW_TILE = 256
SCAN_ROWS = 8
S5_SEQS = 4
S5_STEPS = 64
S5_CH_TILE = 128
S5_LANE_TILE = 512
EXPERT_ROWS = 256
GRANULE = 8
TILE_SLOTS = ROW_TILE * TOP_K + N_EXPERTS * GRANULE
GRAN_PER_TILE = TILE_SLOTS // GRANULE
GRAN_PER_BLOCK = EXPERT_ROWS // GRANULE
VMEM_LIMIT = 56 * 1024 * 1024


def _cparams(n_axes):
    return pltpu.CompilerParams(dimension_semantics=("arbitrary",) * n_axes,
                                vmem_limit_bytes=VMEM_LIMIT)


def _rms(x, g):
    r = lax.rsqrt(jnp.mean(x * x, axis=-1, keepdims=True) + NORM_EPS)
    return x * r * g


def _dot(a, b):
    return jnp.dot(a, b, preferred_element_type=F32)


def _dot_nt(a, b):
    return lax.dot_general(a, b, (((1,), (1,)), ((), ())), preferred_element_type=F32)


def _dot_tn(a, b):
    return lax.dot_general(a, b, (((0,), (0,)), ((), ())), preferred_element_type=F32)


def _two_stream_specs(n_first):
    return (pl.BlockSpec((ROW_TILE, D_MODEL), lambda i: (jnp.minimum(i, n_first - 1), 0)),
            pl.BlockSpec((ROW_TILE, D_MODEL), lambda i: (jnp.maximum(i - n_first, 0), 0)))


def _inproj_kernel(xp_ref, xs_ref, g_ref, w_ref, u_ref, zh_ref, gab_ref, *, n_first):
    x = jnp.where(pl.program_id(0) < n_first, xp_ref[...], xs_ref[...])
    h = _rms(x, g_ref[...]).astype(BF16)
    u_ref[...] = _dot(h, w_ref[:, :S5_WIDTH])
    zh_ref[...] = _dot(h, w_ref[:, S5_WIDTH:S5_WIDTH + 4 * HG_WIDTH])
    gab_ref[...] = _dot(h, w_ref[:, S5_WIDTH + 4 * HG_WIDTH:])


def _inproj(xp, xs, g_mix, w_in_bf):
    n_first = xp.shape[0] // ROW_TILE
    t = xp.shape[0] + xs.shape[0]
    ncols = w_in_bf.shape[1]
    row = lambda i: (i, 0)
    fixed = lambda i: (0, 0)
    return pl.pallas_call(
        functools.partial(_inproj_kernel, n_first=n_first),
        grid=(t // ROW_TILE,),
        in_specs=[*_two_stream_specs(n_first),
                  pl.BlockSpec((1, D_MODEL), fixed),
                  pl.BlockSpec((D_MODEL, ncols), fixed)],
        out_specs=[pl.BlockSpec((ROW_TILE, S5_WIDTH), row),
                   pl.BlockSpec((ROW_TILE, 4 * HG_WIDTH), row),
                   pl.BlockSpec((ROW_TILE, 2 * D_MODEL), row)],
        out_shape=[jax.ShapeDtypeStruct((t, S5_WIDTH), F32),
                   jax.ShapeDtypeStruct((t, 4 * HG_WIDTH), F32),
                   jax.ShapeDtypeStruct((t, 2 * D_MODEL), F32)],
        compiler_params=_cparams(1),
        name="inproj",
    )(xp, xs, g_mix.reshape(1, D_MODEL), w_in_bf)


def _s5_kernel(u0_ref, u1_ref, u2_ref, u3_ref, s0_ref, perm_ref, pinv_ref, wb_ref, wc_ref,
               coef_ref, d_ref, wg_ref, ya_ref, sp_out, ss_out, buf, v_ref, *, n_prompt_steps):
    i = pl.program_id(0)

    @pl.when(i == 0)
    def _():
        v_ref[...] = jnp.zeros_like(v_ref)

    @pl.when(i >= n_prompt_steps)
    def _():
        v_ref[...] = s0_ref[0]

    u = jnp.concatenate([r[0] for r in (u0_ref, u1_ref, u2_ref, u3_ref)], axis=0)
    rows = 2 * u.shape[0]
    up = _dot(perm_ref[...], u.astype(BF16))
    re_half = (lax.broadcasted_iota(jnp.int32, (rows, S5_CH_TILE), 0) & S5_SEQS) == 0
    n_tiles = S5_WIDTH // S5_CH_TILE
    for n in range(n_tiles):
        un = up[:, n * S5_CH_TILE:(n + 1) * S5_CH_TILE]
        lhs = jnp.concatenate([jnp.where(re_half, un, 0.0), jnp.where(re_half, 0.0, un)], axis=1)
        buf[:, n * S5_LANE_TILE:(n + 1) * S5_LANE_TILE] = _dot(lhs.astype(BF16), wb_ref[n])

    def step(t, v):
        row = pl.multiple_of(t * SCAN_ROWS, SCAN_ROWS)
        v = (coef_ref[0] * v + coef_ref[1] * pltpu.roll(v, S5_SEQS, axis=0)
             + buf[pl.ds(row, SCAN_ROWS), :])
        buf[pl.ds(row, SCAN_ROWS), :] = v
        return v

    v = lax.fori_loop(0, S5_STEPS, step, v_ref[...], unroll=2)
    v_ref[...] = v

    @pl.when(i < n_prompt_steps)
    def _():
        sp_out[0] = v

    @pl.when(i >= n_prompt_steps)
    def _():
        ss_out[0] = v

    ws = []
    for n in range(n_tiles):
        z = _dot(buf[:, n * S5_LANE_TILE:(n + 1) * S5_LANE_TILE].astype(BF16), wc_ref[n])
        ws.append(jnp.where(re_half, z[:, :S5_CH_TILE], z[:, S5_CH_TILE:]))
    w_hi, w_lo = _split2(jnp.concatenate(ws, axis=1))
    y = _dot(pinv_ref[...], w_hi) + _dot(pinv_ref[...], w_lo) + d_ref[...] * u
    y = jax.nn.gelu(y).astype(BF16)
    p = _dot(y, wg_ref[...])
    ya = p[:, :D_MODEL] * jax.nn.sigmoid(p[:, D_MODEL:])
    ya_ref[0] = ya.reshape(S5_SEQS, S5_STEPS, D_MODEL)


def _s5(u, s0_sample, tables, d_skip, wg, *, prompt_len, n_sample_groups):
    perm, pinv, wb, wc, coef = tables
    n_prompt_steps = prompt_len // S5_STEPS
    n_steps = n_prompt_steps + n_sample_groups
    sample_base = S5_SEQS * n_prompt_steps
    u3 = u.reshape(-1, S5_STEPS, S5_WIDTH)

    def u_spec(s):
        def imap(i):
            prompt_blk = s * n_prompt_steps + i
            sample_blk = sample_base + S5_SEQS * (i - n_prompt_steps) + s
            return (jnp.where(i < n_prompt_steps, prompt_blk, sample_blk), 0, 0)
        return pl.BlockSpec((1, S5_STEPS, S5_WIDTH), imap)

    sample_grp = lambda i: (jnp.maximum(i - n_prompt_steps, 0), 0, 0)
    whole = lambda a: pl.BlockSpec(a.shape, lambda i: (0,) * a.ndim)
    state_blk = (1, 2 * S5_SEQS, S5_LANES)
    rows = S5_SEQS * S5_STEPS
    return pl.pallas_call(
        functools.partial(_s5_kernel, n_prompt_steps=n_prompt_steps),
        grid=(n_steps,),
        in_specs=[u_spec(0), u_spec(1), u_spec(2), u_spec(3),
                  pl.BlockSpec(state_blk, sample_grp),
                  whole(perm), whole(pinv), whole(wb), whole(wc), whole(coef), whole(d_skip), whole(wg)],
        out_specs=[pl.BlockSpec((1, S5_SEQS, S5_STEPS, D_MODEL), lambda i: (i, 0, 0, 0)),
                   pl.BlockSpec(state_blk, lambda i: (0, 0, 0)),
                   pl.BlockSpec(state_blk, sample_grp)],
        out_shape=[jax.ShapeDtypeStruct((n_steps, S5_SEQS, S5_STEPS, D_MODEL), F32),
                   jax.ShapeDtypeStruct(state_blk, F32),
                   jax.ShapeDtypeStruct((n_sample_groups,) + state_blk[1:], F32)],
        scratch_shapes=[pltpu.VMEM((2 * rows, S5_LANES), F32),
                        pltpu.VMEM((2 * S5_SEQS, S5_LANES), F32)],
        compiler_params=_cparams(1),
        name="s5",
    )(u3, u3, u3, u3, s0_sample, perm, pinv, wb, wc, coef, d_skip, wg)


def _s5_tables(lam_re, lam_im, log_dt, b_re, b_im, c_re, c_im):
    lr = jnp.minimum(lam_re.astype(F32), S5_DT_CLIP)
    li = lam_im.astype(F32)
    dt = jnp.exp(log_dt.astype(F32))[:, None]
    mag = jnp.exp(lr * dt)
    ar = mag * jnp.cos(li * dt)
    ai = mag * jnp.sin(li * dt)
    den = lr * lr + li * li
    zr = ((ar - 1.0) * lr + ai * li) / den
    zi = (ai * lr - (ar - 1.0) * li) / den
    br = b_re.astype(F32)
    bi = b_im.astype(F32)
    bbr = zr[..., None] * br - zi[..., None] * bi
    bbi = zr[..., None] * bi + zi[..., None] * br

    ar_f, ai_f = ar.reshape(1, -1), ai.reshape(1, -1)
    sign = jnp.where(jnp.arange(2 * S5_SEQS)[:, None] < S5_SEQS, -1.0, 1.0)
    coef = jnp.stack([jnp.broadcast_to(ar_f, (2 * S5_SEQS, S5_LANES)), sign * ai_f]).astype(F32)

    g_per_tile = S5_CH_TILE // S5_GROUP
    n_tiles = S5_GROUPS // g_per_tile
    eye = jnp.eye(g_per_tile, dtype=F32)

    def b_tile(bb):
        x = bb.reshape(n_tiles, g_per_tile, S5_STATE, S5_GROUP)
        x = x[:, :, :, :, None] * eye[None, :, None, None, :]
        return x.transpose(0, 1, 3, 4, 2).reshape(n_tiles, S5_CH_TILE, S5_LANE_TILE)

    wb = jnp.concatenate([b_tile(bbr), b_tile(bbi)], axis=1).astype(BF16)

    def c_tile(cc):
        x = cc.astype(F32).reshape(n_tiles, g_per_tile, S5_GROUP, S5_STATE)
        x = x[:, :, :, :, None] * eye[None, :, None, None, :]
        return x.transpose(0, 1, 3, 4, 2).reshape(n_tiles, S5_LANE_TILE, S5_CH_TILE)

    wc = jnp.concatenate([c_tile(c_re), -c_tile(c_im)], axis=2).astype(BF16)

    r = jnp.arange(2 * S5_SEQS * S5_STEPS)
    src = (r % S5_SEQS) * S5_STEPS + r // (2 * S5_SEQS)
    perm = (src[:, None] == jnp.arange(S5_SEQS * S5_STEPS)[None, :]).astype(BF16)
    return perm, perm.T, wb, wc, coef


def _split2(x):
    hi = x.astype(BF16)
    lo = (x - hi.astype(F32)).astype(BF16)
    return hi, lo


def _split3(x):
    hi = x.astype(BF16)
    r = x - hi.astype(F32)
    mid = r.astype(BF16)
    lo = (r - mid.astype(F32)).astype(BF16)
    return hi, mid, lo


def _hgrn_kernel(zh_ref, lb_ref, gn_ref, wo_ref, s0_ref, yb_ref, sp_out, ss_out,
                 st, gc_ref, att_ref, o_ref, *, n_prompt_tiles, chunks_per_seq, sample_len):
    i = pl.program_id(0)
    refs = (zh_ref, lb_ref, gn_ref, wo_ref, yb_ref, st, gc_ref, att_ref, o_ref)

    @pl.when(i < n_prompt_tiles)
    def _():
        j = i % chunks_per_seq

        @pl.when(j == 0)
        def _():
            st[0] = jnp.zeros((HG_HEADS, HG_DK, HG_DK), F32)

        _hgrn_tile(*refs, n_seg=1, seg_len=ROW_TILE)

        @pl.when(j == chunks_per_seq - 1)
        def _():
            for h in range(HG_HEADS):
                sp_out[0, h] = st[0, h].T

    @pl.when(i >= n_prompt_tiles)
    def _():
        n_seg = ROW_TILE // sample_len
        for seg in range(n_seg):
            for h in range(HG_HEADS):
                st[seg, h] = s0_ref[seg, h].T
        _hgrn_tile(*refs, n_seg=n_seg, seg_len=sample_len)
        for seg in range(n_seg):
            for h in range(HG_HEADS):
                ss_out[seg, h] = st[seg, h].T


def _hgrn_tile(zh_ref, lb_ref, gn_ref, wo_ref, yb_ref, st, gc_ref, att_ref, o_ref, *, n_seg, seg_len):
    rows = n_seg * seg_len
    zq = zh_ref[:, 0:HG_WIDTH]
    zf = zh_ref[:, HG_WIDTH:2 * HG_WIDTH]
    v = zh_ref[:, 2 * HG_WIDTH:3 * HG_WIDTH]
    lb = lb_ref[...]
    q = zq * jax.nn.sigmoid(zq)
    f = lb + (1.0 - lb) * jax.nn.sigmoid(zf)
    k = 1.0 - f
    g = jnp.log(f)
    vb = v.astype(BF16)

    ri = lax.broadcasted_iota(jnp.int32, (rows, rows), 0)
    ci = lax.broadcasted_iota(jnp.int32, (rows, rows), 1)
    seg_shift = int(math.log2(seg_len))
    same_seg = (ri >> seg_shift) == (ci >> seg_shift)

    tri = jnp.where(same_seg & (ci <= ri), 1.0, 0.0).astype(BF16)
    g3 = jnp.concatenate(_split3(g), axis=1)
    gcs = _dot(tri, g3)
    gc = gcs[:, :HG_WIDTH] + gcs[:, HG_WIDTH:2 * HG_WIDTH] + gcs[:, 2 * HG_WIDTH:]
    gc_ref[...] = gc

    xor = jnp.where(ci < ri, ri ^ ci, 0)
    qb = q.astype(BF16)
    kb = k.astype(BF16)
    for h in range(HG_HEADS):
        hs = slice(h * HG_DK, (h + 1) * HG_DK)
        att_ref[h] = jnp.where(ri == ci, _dot_nt(qb[:, hs], kb[:, hs]), 0.0)

    level = seg_len // 2
    while level >= 1:
        if 2 * level >= SCAN_ROWS:
            blocks = []
            for b in range(rows // (2 * level)):
                mid = b * 2 * level + level - 1
                blocks.append(jnp.broadcast_to(gc_ref[mid:mid + 1, :], (2 * level, HG_WIDTH)))
            gmid = blocks[0] if len(blocks) == 1 else jnp.concatenate(blocks, axis=0)
            dexp = -jnp.abs(gc - gmid)
        else:
            mid = (ri & ~(2 * level - 1)) + (level - 1)
            upper = (ri & level) != 0
            w = (ci > jnp.where(upper, mid, ri)) & (ci <= jnp.where(upper, ri, mid))
            g2 = jnp.concatenate(_split2(g), axis=1)
            ds = _dot(jnp.where(w, 1.0, 0.0).astype(BF16), g2)
            dexp = ds[:, :HG_WIDTH] + ds[:, HG_WIDTH:]
        e = jnp.exp(dexp)
        ql = (q * e).astype(BF16)
        kl = (k * e).astype(BF16)
        lshift = int(math.log2(level))
        sel = (xor >> lshift) == 1
        for h in range(HG_HEADS):
            hs = slice(h * HG_DK, (h + 1) * HG_DK)
            att_ref[h] += jnp.where(sel, _dot_nt(ql[:, hs], kl[:, hs]), 0.0)
        level //= 2

    qg = (q * jnp.exp(gc)).astype(BF16)
    for h in range(HG_HEADS):
        hs = slice(h * HG_DK, (h + 1) * HG_DK)
        o_ref[:, hs] = _dot(att_ref[h].astype(BF16), vb[:, hs])

    for seg in range(n_seg):
        rs = slice(seg * seg_len, (seg + 1) * seg_len)
        last = (seg + 1) * seg_len - 1
        g_last = gc_ref[last:last + 1, :]
        kd = (k[rs] * jnp.exp(g_last - gc[rs])).astype(BF16)
        decay = jnp.exp(g_last)
        for h in range(HG_HEADS):
            hs = slice(h * HG_DK, (h + 1) * HG_DK)
            s_t = st[seg, h]
            o_ref[rs, hs] += _dot_nt(qg[rs, hs], s_t.astype(BF16))
            st[seg, h] = s_t * decay[:, hs] + _dot_tn(vb[rs, hs], kd[:, hs])

    zg = zh_ref[:, 3 * HG_WIDTH:]
    gate = zg * jax.nn.sigmoid(zg)
    outs = []
    for h in range(HG_HEADS):
        hs = slice(h * HG_DK, (h + 1) * HG_DK)
        outs.append(_rms(o_ref[:, hs], gn_ref[...]) * gate[:, hs])
    yb_ref[...] = _dot(jnp.concatenate(outs, axis=1).astype(BF16), wo_ref[...])


def _hgrn(zh, lb, gn, wo, s0_sample, *, n_prompt, prompt_len, sample_len):
    t = zh.shape[0]
    chunks_per_seq = prompt_len // ROW_TILE
    n_prompt_tiles = n_prompt * chunks_per_seq
    n_seg = ROW_TILE // sample_len
    n_sample = s0_sample.shape[0]
    row = lambda i: (i, 0)
    fixed = lambda i: (0, 0)
    state = (HG_HEADS, HG_DK, HG_DK)
    prompt_seq = lambda i: (jnp.minimum(i // chunks_per_seq, n_prompt - 1), 0, 0, 0)
    sample_grp = lambda i: (jnp.maximum(i - n_prompt_tiles, 0), 0, 0, 0)
    kern = functools.partial(_hgrn_kernel, n_prompt_tiles=n_prompt_tiles,
                             chunks_per_seq=chunks_per_seq, sample_len=sample_len)
    return pl.pallas_call(
        kern,
        grid=(t // ROW_TILE,),
        in_specs=[pl.BlockSpec((ROW_TILE, 4 * HG_WIDTH), row),
                  pl.BlockSpec((1, HG_WIDTH), fixed),
                  pl.BlockSpec((1, HG_DK), fixed),
                  pl.BlockSpec(wo.shape, fixed),
                  pl.BlockSpec((n_seg,) + state, sample_grp)],
        out_specs=[pl.BlockSpec((ROW_TILE, D_MODEL), row),
                   pl.BlockSpec((1,) + state, prompt_seq),
                   pl.BlockSpec((n_seg,) + state, sample_grp)],
        out_shape=[jax.ShapeDtypeStruct((t, D_MODEL), F32),
                   jax.ShapeDtypeStruct((n_prompt,) + state, F32),
                   jax.ShapeDtypeStruct((n_sample,) + state, F32)],
        scratch_shapes=[pltpu.VMEM((n_seg,) + state, F32),
                        pltpu.VMEM((ROW_TILE, HG_WIDTH), F32),
                        pltpu.VMEM((HG_HEADS, ROW_TILE, ROW_TILE), F32),
                        pltpu.VMEM((ROW_TILE, HG_WIDTH), F32)],
        compiler_params=_cparams(1),
        name="hgrn2",
    )(zh, lb, gn, wo, s0_sample)


def _merge_kernel(*refs, n_first, n_tiles):
    i = pl.program_id(0)
    xg_ref = refs[13]

    @pl.when(i < n_tiles)
    def _():
        _merge_tile(*refs, n_first=n_first)

    @pl.when(i == n_tiles)
    def _():
        xg_ref[...] = jnp.zeros_like(xg_ref)


def _merge_tile(gab_ref, ya0_ref, ya1_ref, ya2_ref, ya3_ref, yb_ref, xp_ref, xs_ref, wo_ref,
                gf_ref, rw_ref, rb_ref, x1_ref, xg_ref, pos_ref, gate_ref, cnt_ref, off_ref,
                *, n_first):
    ga = gab_ref[:, :D_MODEL]
    gb = gab_ref[:, D_MODEL:]
    ya = jnp.concatenate([r[0] for r in (ya0_ref, ya1_ref, ya2_ref, ya3_ref)], axis=0)
    mixed = jax.nn.sigmoid(ga) * ya + jax.nn.sigmoid(gb) * yb_ref[...]
    x = jnp.where(pl.program_id(0) < n_first, xp_ref[...], xs_ref[...])
    x1 = x + _dot(mixed.astype(BF16), wo_ref[...])
    x1_ref[...] = x1
    h2 = _rms(x1, gf_ref[...])

    h_hi, h_lo = _split2(h2)
    p_hi = _dot(h_hi, rw_ref[...])
    p_lo = _dot(h_lo, rw_ref[:, :N_EXPERTS])
    logits = p_hi[:, :N_EXPERTS] + p_hi[:, N_EXPERTS:] + p_lo + rb_ref[...]

    rows = logits.shape[0]
    lane = lax.broadcasted_iota(jnp.int32, (rows, N_EXPERTS), 1).astype(F32)
    lane4 = lax.broadcasted_iota(jnp.int32, (rows, TOP_K), 1)
    lg = logits
    sel = jnp.zeros((rows, N_EXPERTS), F32)
    vals, idxs = [], []
    for _ in range(TOP_K):
        m = jnp.max(lg, axis=-1, keepdims=True)
        idx = jnp.min(jnp.where(lg == m, lane, float(N_EXPERTS)), axis=-1, keepdims=True)
        hit = lane == idx
        vals.append(m)
        idxs.append(idx)
        lg = jnp.where(hit, -jnp.inf, lg)
        sel = sel + jnp.where(hit, 1.0, 0.0)

    es = [jnp.exp(vv - vals[0]) for vv in vals]
    tot = es[0] + es[1] + es[2] + es[3]

    ri = lax.broadcasted_iota(jnp.int32, (rows, rows), 0)
    ci = lax.broadcasted_iota(jnp.int32, (rows, rows), 1)
    before = jnp.where(ci < ri, 1.0, 0.0).astype(BF16)
    rank = _dot(before, sel.astype(BF16))
    gran = jnp.floor((jnp.sum(sel, axis=0, keepdims=True) + (GRANULE - 1)) * (1.0 / GRANULE))
    ei = lax.broadcasted_iota(jnp.int32, (N_EXPERTS, N_EXPERTS), 0)
    ej = lax.broadcasted_iota(jnp.int32, (N_EXPERTS, N_EXPERTS), 1)
    off = _dot(gran.astype(BF16), jnp.where(ei < ej, 1.0, 0.0).astype(BF16))
    slot_of = rank + off * float(GRANULE)
    poss = [jnp.sum(jnp.where(lane == idx, slot_of, 0.0), axis=-1, keepdims=True) for idx in idxs]

    slot_iota = lax.broadcasted_iota(jnp.int32, (rows, TILE_SLOTS), 1).astype(F32)
    onehot = jnp.zeros((rows, TILE_SLOTS), F32)
    for kk in range(TOP_K):
        onehot = jnp.where(slot_iota == poss[kk], 1.0, onehot)
    xg_ref[...] = _dot_tn(onehot.astype(BF16), h2.astype(BF16))

    pos = jnp.zeros((rows, TOP_K), jnp.int32)
    gates = jnp.zeros((rows, TOP_K), F32)
    for kk in range(TOP_K):
        pos = jnp.where(lane4 == kk, poss[kk].astype(jnp.int32), pos)
        gates = jnp.where(lane4 == kk, es[kk] / tot, gates)
    pos_ref[...] = pos
    gate_ref[...] = gates
    cnt_ref[0] = gran.astype(jnp.int32)
    off_ref[0] = off.astype(jnp.int32)


def _merge(gab, ya_steps, yb, xp, xs, wo, g_ffn, rw2, rb, *, prompt_len):
    n_first = xp.shape[0] // ROW_TILE
    t = xp.shape[0] + xs.shape[0]
    n_tiles = t // ROW_TILE
    chunks_per_seq = prompt_len // ROW_TILE
    quarters = ROW_TILE // S5_STEPS
    ya3 = ya_steps.reshape(-1, S5_STEPS, D_MODEL)

    def ya_spec(q):
        def imap(i):
            i = jnp.minimum(i, n_tiles - 1)
            seq, chunk = i // chunks_per_seq, i % chunks_per_seq
            prompt_blk = (chunk * quarters + q) * S5_SEQS + seq
            return (jnp.where(i < n_first, prompt_blk, i * quarters + q), 0, 0)
        return pl.BlockSpec((1, S5_STEPS, D_MODEL), imap)

    tile = lambda i: jnp.minimum(i, n_tiles - 1)
    row = lambda i: (tile(i), 0)
    fixed = lambda i: (0, 0)
    per_tile = lambda i: (tile(i), 0, 0)
    xp_spec, xs_spec = _two_stream_specs(n_first)
    return pl.pallas_call(
        functools.partial(_merge_kernel, n_first=n_first, n_tiles=n_tiles),
        grid=(n_tiles + 1,),
        in_specs=[pl.BlockSpec((ROW_TILE, 2 * D_MODEL), row),
                  ya_spec(0), ya_spec(1), ya_spec(2), ya_spec(3),
                  pl.BlockSpec((ROW_TILE, D_MODEL), row),
                  pl.BlockSpec(xp_spec.block_shape, lambda i: xp_spec.index_map(tile(i))),
                  pl.BlockSpec(xs_spec.block_shape, lambda i: xs_spec.index_map(tile(i))),
                  pl.BlockSpec((D_MODEL, D_MODEL), fixed),
                  pl.BlockSpec((1, D_MODEL), fixed),
                  pl.BlockSpec((D_MODEL, 2 * N_EXPERTS), fixed),
                  pl.BlockSpec((1, N_EXPERTS), fixed)],
        out_specs=[pl.BlockSpec((ROW_TILE, D_MODEL), row),
                   pl.BlockSpec((TILE_SLOTS, D_MODEL), lambda i: (i, 0)),
                   pl.BlockSpec((ROW_TILE, TOP_K), row),
                   pl.BlockSpec((ROW_TILE, TOP_K), row),
                   pl.BlockSpec((1, 1, N_EXPERTS), per_tile),
                   pl.BlockSpec((1, 1, N_EXPERTS), per_tile)],
        out_shape=[jax.ShapeDtypeStruct((t, D_MODEL), F32),
                   jax.ShapeDtypeStruct(((n_tiles + 1) * TILE_SLOTS, D_MODEL), F32),
                   jax.ShapeDtypeStruct((t, TOP_K), jnp.int32),
                   jax.ShapeDtypeStruct((t, TOP_K), F32),
                   jax.ShapeDtypeStruct((n_tiles, 1, N_EXPERTS), jnp.int32),
                   jax.ShapeDtypeStruct((n_tiles, 1, N_EXPERTS), jnp.int32)],
        compiler_params=_cparams(1),
        name="merge_router",
    )(gab, ya3, ya3, ya3, ya3, yb, xp, xs, wo, g_ffn, rw2, rb)


def _expert_kernel(blk_e, nxt_e, n_used, src0_ref, srcn_ref, dst_ref, xg_hbm, w1_hbm, b1_ref, w2_hbm,
                   b2_ref, yt_hbm, xbuf, obuf, gsem, ssem, w1s, w2s, wsem, w1b, w2b):
    b = pl.program_id(0)
    nu = n_used[0]

    def granule_rows(g):
        return pl.ds(pl.multiple_of(g * GRANULE, GRANULE), GRANULE)

    def block_rows(j):
        return pl.ds(j * GRANULE, GRANULE)

    def start_gather(src_ref, slot):
        for j in range(GRAN_PER_BLOCK):
            pltpu.make_async_copy(xg_hbm.at[granule_rows(src_ref[0, 0, j])],
                                  xbuf.at[slot, block_rows(j)], gsem.at[slot]).start()

    def start_scatter(slot):
        for j in range(GRAN_PER_BLOCK):
            pltpu.make_async_copy(obuf.at[slot, block_rows(j)],
                                  yt_hbm.at[granule_rows(dst_ref[0, 0, j])], ssem.at[slot]).start()

    def wait_gather(slot):
        pltpu.make_async_copy(xg_hbm.at[pl.ds(0, EXPERT_ROWS)], xbuf.at[slot], gsem.at[slot]).wait()

    def wait_scatter(slot):
        pltpu.make_async_copy(obuf.at[slot], yt_hbm.at[pl.ds(0, EXPERT_ROWS)], ssem.at[slot]).wait()

    def weight_copies(e):
        return (pltpu.make_async_copy(w1_hbm.at[e], w1s, wsem.at[0]),
                pltpu.make_async_copy(w2_hbm.at[e], w2s, wsem.at[1]))

    first = b == 0
    new_expert = first | (blk_e[b] != blk_e[jnp.maximum(b - 1, 0)])

    @pl.when(first)
    def _():
        start_gather(src0_ref, 0)
        for c in weight_copies(blk_e[0]):
            c.start(priority=1)

    @pl.when((b >= 2) & (b <= nu))
    def _():
        wait_scatter(b % 2)

    @pl.when(b == nu)
    def _():
        wait_gather(b % 2)
        wait_scatter((b - 1) % 2)

    @pl.when(b < nu)
    def _():
        slot = b % 2
        wait_gather(slot)

        @pl.when(new_expert)
        def _():
            for c in weight_copies(blk_e[b]):
                c.wait()
            w1b[...] = w1s[...].astype(BF16)
            w2b[...] = w2s[...].astype(BF16)

            @pl.when(nxt_e[b] >= 0)
            def _():
                for c in weight_copies(nxt_e[b]):
                    c.start(priority=1)

        start_gather(srcn_ref, 1 - slot)
        x = xbuf[slot].astype(BF16)
        u = _dot(x, w1b[...]) + b1_ref[0]
        a = jnp.minimum(u[:, :D_EXPERT], SWIGLU_LIMIT)
        lin = jnp.clip(u[:, D_EXPERT:], -SWIGLU_LIMIT, SWIGLU_LIMIT)
        hmid = a * jax.nn.sigmoid(SWIGLU_ALPHA * a) * (lin + 1.0)
        obuf[slot] = _dot(hmid.astype(BF16), w2b[...]) + b2_ref[0]
        start_scatter(slot)


def _experts(xg, plan, w1, b1, w2, b2):
    granule_src, granule_dst, blk_e, nxt_e, n_used = plan
    n_blocks = blk_e.shape[0]
    src3 = granule_src.reshape(n_blocks, 1, GRAN_PER_BLOCK)
    dst3 = granule_dst.reshape(n_blocks, 1, GRAN_PER_BLOCK)
    smem_blk = lambda imap: pl.BlockSpec((1, 1, GRAN_PER_BLOCK), imap, memory_space=pltpu.SMEM)
    by_expert = lambda b, be, *_: (be[b], 0, 0)
    grid_spec = pltpu.PrefetchScalarGridSpec(
        num_scalar_prefetch=3,
        grid=(n_blocks,),
        in_specs=[smem_blk(lambda b, *_: (0, 0, 0)),
                  smem_blk(lambda b, *_: (jnp.minimum(b + 1, n_blocks - 1), 0, 0)),
                  smem_blk(lambda b, *_: (b, 0, 0)),
                  pl.BlockSpec(memory_space=pl.ANY),
                  pl.BlockSpec(memory_space=pl.ANY),
                  pl.BlockSpec((1, 1, 2 * D_EXPERT), by_expert),
                  pl.BlockSpec(memory_space=pl.ANY),
                  pl.BlockSpec((1, 1, D_MODEL), by_expert)],
        out_specs=pl.BlockSpec(memory_space=pl.ANY),
        scratch_shapes=[pltpu.VMEM((2, EXPERT_ROWS, D_MODEL), F32),
                        pltpu.VMEM((2, EXPERT_ROWS, D_MODEL), F32),
                        pltpu.SemaphoreType.DMA((2,)),
                        pltpu.SemaphoreType.DMA((2,)),
                        pltpu.VMEM((D_MODEL, 2 * D_EXPERT), F32),
                        pltpu.VMEM((D_EXPERT, D_MODEL), F32),
                        pltpu.SemaphoreType.DMA((2,)),
                        pltpu.VMEM((D_MODEL, 2 * D_EXPERT), BF16),
                        pltpu.VMEM((D_EXPERT, D_MODEL), BF16)],
    )
    return pl.pallas_call(
        _expert_kernel,
        grid_spec=grid_spec,
        out_shape=jax.ShapeDtypeStruct(xg.shape, F32),
        input_output_aliases={6: 0},
        compiler_params=_cparams(1),
        name="experts",
    )(blk_e, nxt_e, n_used, src3, src3, dst3, xg, w1, b1.reshape(N_EXPERTS, 1, -1), w2,
      b2.reshape(N_EXPERTS, 1, -1))


def _combine_kernel(pos_ref, gate_ref, x1_ref, gf_ref, yt_ref, y_ref):
    rows = pos_ref.shape[0]
    slot_iota = lax.broadcasted_iota(jnp.int32, (rows, TILE_SLOTS), 1)
    pos = pos_ref[...]
    gates = gate_ref[...]
    weights = jnp.zeros((rows, TILE_SLOTS), F32)
    for kk in range(TOP_K):
        weights = jnp.where(slot_iota == pos[:, kk:kk + 1], gates[:, kk:kk + 1], weights)
    moe = _dot(weights.astype(BF16), yt_ref[...].astype(BF16))
    y_ref[...] = _rms(x1_ref[...] + moe, gf_ref[...])


def _combine(pos, gates, x1, g_final, yt, *, tile0, n_tiles):
    row = lambda i: (tile0 + i, 0)
    return pl.pallas_call(
        _combine_kernel,
        grid=(n_tiles,),
        in_specs=[pl.BlockSpec((ROW_TILE, TOP_K), row),
                  pl.BlockSpec((ROW_TILE, TOP_K), row),
                  pl.BlockSpec((ROW_TILE, D_MODEL), row),
                  pl.BlockSpec((1, D_MODEL), lambda i: (0, 0)),
                  pl.BlockSpec((TILE_SLOTS, D_MODEL), row)],
        out_specs=pl.BlockSpec((ROW_TILE, D_MODEL), lambda i: (i, 0)),
        out_shape=jax.ShapeDtypeStruct((n_tiles * ROW_TILE, D_MODEL), F32),
        compiler_params=_cparams(1),
        name="combine",
    )(pos, gates, x1, g_final.reshape(1, D_MODEL), yt)


def _granule_plan(gran, goff):
    n_tiles = gran.shape[0]
    max_gran = (n_tiles * ROW_TILE * TOP_K + (GRANULE - 1) * n_tiles * N_EXPERTS) // GRANULE
    n_blocks = max_gran // GRAN_PER_BLOCK + N_EXPERTS + 1
    n_slots = n_blocks * GRAN_PER_BLOCK
    i32 = jnp.int32
    per_e = jnp.sum(gran, axis=0)
    region = (per_e + GRAN_PER_BLOCK - 1) // GRAN_PER_BLOCK * GRAN_PER_BLOCK
    region_end = jnp.cumsum(region)
    region_start = region_end - region
    g_t = gran.T
    run_start = (region_start[:, None] + jnp.cumsum(g_t, axis=1) - g_t).reshape(-1)
    run_end = run_start + g_t.reshape(-1)
    first_gran = (jnp.arange(n_tiles, dtype=i32)[None, :] * GRAN_PER_TILE + goff.T).reshape(-1)
    shift = first_gran - run_start
    d_shift = jnp.diff(shift, prepend=0)
    d_end = jnp.diff(run_end, prepend=0)
    slot = jnp.arange(n_slots, dtype=i32)
    started = slot[:, None] >= run_start[None, :]
    granule = slot + jnp.sum(jnp.where(started, d_shift[None, :], 0), axis=1)
    valid = slot < jnp.sum(jnp.where(started, d_end[None, :], 0), axis=1)
    spare = n_tiles * GRAN_PER_TILE
    j = slot % GRAN_PER_BLOCK
    odd_blk = (slot // GRAN_PER_BLOCK) % 2
    granule_src = jnp.where(valid, granule, spare + 2 * GRAN_PER_BLOCK + j).astype(i32)
    granule_dst = jnp.where(valid, granule, spare + odd_blk * GRAN_PER_BLOCK + j).astype(i32)
    blk = jnp.arange(n_blocks, dtype=i32) * GRAN_PER_BLOCK
    blk_e = jnp.minimum(jnp.sum(blk[:, None] >= region_end[None, :], axis=1), N_EXPERTS - 1).astype(i32)
    e = jnp.arange(N_EXPERTS, dtype=i32)
    later = jnp.where((e[None, :] > e[:, None]) & (per_e[None, :] > 0), e[None, :], N_EXPERTS)
    nxt = jnp.min(later, axis=1)
    nxt = jnp.where(nxt == N_EXPERTS, -1, nxt)
    nxt_e = jnp.sum(jnp.where(blk_e[:, None] == e[None, :], nxt[None, :], 0), axis=1).astype(i32)
    n_used = (region_end[-1:] // GRAN_PER_BLOCK).astype(i32)
    return granule_src, granule_dst, blk_e, nxt_e, n_used


def kernel(x_prompt, x_sample, state_s5_re, state_s5_im, state_hgrn, g_mix, w_in, s5_lambda_re,
           s5_lambda_im, s5_log_dt, s5_b_re, s5_b_im, s5_c_re, s5_c_im, s5_d, s5_w_glu, s5_w_gate,
           hgrn_lower_bound, hgrn_norm_g, hgrn_w_out, w_out, g_ffn, router_w, router_b, moe_w1,
           moe_b1, moe_w2, moe_b2, g_final):
    depth = g_mix.shape[0]
    n_p, l_p, _ = x_prompt.shape
    n_s, l_s, _ = x_sample.shape
    assert depth == 1 and n_p == S5_SEQS and l_p % ROW_TILE == 0
    assert l_s == S5_STEPS and n_s % S5_SEQS == 0 and ROW_TILE % l_s == 0
    li = 0
    t_p, t_s = n_p * l_p, n_s * l_s
    n_groups = n_s // S5_SEQS

    lb = jnp.cumsum(jax.nn.softmax(hgrn_lower_bound.astype(F32), axis=0), axis=0)[li].reshape(1, -1)
    s5_tabs = _s5_tables(s5_lambda_re[li], s5_lambda_im[li], s5_log_dt[li], s5_b_re[li],
                         s5_b_im[li], s5_c_re[li], s5_c_im[li])
    wg = jnp.concatenate([s5_w_glu[li], s5_w_gate[li]], axis=1).astype(BF16)
    rw_hi = router_w[li].astype(BF16)
    rw_lo = (router_w[li] - rw_hi.astype(F32)).astype(BF16)
    rw2 = jnp.concatenate([rw_hi, rw_lo], axis=1)

    xp = x_prompt.reshape(t_p, D_MODEL)
    xs = x_sample.reshape(t_s, D_MODEL)
    u, zh, gab = _inproj(xp, xs, g_mix[li], w_in[li].astype(BF16))

    s0_s5 = jnp.concatenate([state_s5_re[li].reshape(n_groups, S5_SEQS, S5_LANES),
                             state_s5_im[li].reshape(n_groups, S5_SEQS, S5_LANES)], axis=1).astype(F32)
    ya, s5_p, s5_s = _s5(u, s0_s5, s5_tabs, s5_d[li].reshape(1, -1), wg,
                         prompt_len=l_p, n_sample_groups=n_groups)
    yb, hg_p, hg_s = _hgrn(zh, lb, hgrn_norm_g[li].reshape(1, -1), hgrn_w_out[li].astype(BF16),
                           state_hgrn[li].astype(F32), n_prompt=n_p, prompt_len=l_p, sample_len=l_s)

    x1, xg, pos, gates, gran, goff = _merge(
        gab, ya, yb, xp, xs, w_out[li].astype(BF16), g_ffn[li].reshape(1, -1), rw2,
        router_b[li].reshape(1, -1).astype(F32), prompt_len=l_p)

    plan = _granule_plan(gran.reshape(-1, N_EXPERTS), goff.reshape(-1, N_EXPERTS))
    yt = _experts(xg, plan, moe_w1[li], moe_b1[li], moe_w2[li], moe_b2[li])
    n_pt = t_p // ROW_TILE
    y_p = _combine(pos, gates, x1, g_final, yt, tile0=0, n_tiles=n_pt)
    y_s = _combine(pos, gates, x1, g_final, yt, tile0=n_pt, n_tiles=t_s // ROW_TILE)

    def s5_out(st, n, dtype):
        re = st[:, :S5_SEQS].reshape(1, n, S5_GROUPS, S5_STATE).astype(dtype)
        im = st[:, S5_SEQS:].reshape(1, n, S5_GROUPS, S5_STATE).astype(dtype)
        return re, im

    dt = x_prompt.dtype
    p_re, p_im = s5_out(s5_p, n_p, dt)
    s_re, _ = s5_out(s5_s, n_s, state_s5_re.dtype)
    _, s_im = s5_out(s5_s, n_s, state_s5_im.dtype)
    return (y_p.reshape(n_p, l_p, D_MODEL), y_s.reshape(n_s, l_s, D_MODEL),
            p_re, p_im, hg_p[None].astype(dt),
            s_re, s_im, hg_s[None].astype(state_hgrn.dtype))
```

```python
import functools
import math

import jax
import jax.numpy as jnp
from jax import lax
from jax.experimental import pallas as pl
from jax.experimental.pallas import tpu as pltpu

F32 = jnp.float32
BF16 = jnp.bfloat16

D_MODEL = 1024
NORM_EPS = 1e-6
S5_WIDTH = 512
S5_GROUP = 16
S5_GROUPS = 32
S5_STATE = 64
S5_LANES = S5_GROUPS * S5_STATE
S5_DT_CLIP = -1e-4
HG_WIDTH = 512
HG_HEADS = 4
HG_DK = 128
HG_DIAG = 128
N_EXPERTS = 32
TOP_K = 4
ROUTER_LANES = 128
D_EXPERT = 1024
SWIGLU_ALPHA = 1.702
SWIGLU_LIMIT = 7.0

ROW_TILE = 256
SCAN_ROWS = 8
S5_SEQS = 4
S5_STEPS = 64
S5_CH_TILE = 128
S5_LANE_TILE = 512
EXPERT_ROWS = 256
GRANULE = 8
TILE_SLOTS = ROW_TILE * TOP_K + N_EXPERTS * GRANULE
GRAN_PER_TILE = TILE_SLOTS // GRANULE
GRAN_PER_BLOCK = EXPERT_ROWS // GRANULE
VMEM_LIMIT = 56 * 1024 * 1024


def _cparams(n_axes):
    return pltpu.CompilerParams(dimension_semantics=("arbitrary",) * n_axes,
                                vmem_limit_bytes=VMEM_LIMIT)


def _rms(x, g):
    r = lax.rsqrt(jnp.mean(x * x, axis=-1, keepdims=True) + NORM_EPS)
    return x * r * g


def _dot(a, b):
    return jnp.dot(a, b, preferred_element_type=F32)


def _dot_nt(a, b):
    return lax.dot_general(a, b, (((1,), (1,)), ((), ())), preferred_element_type=F32)


def _dot_tn(a, b):
    return lax.dot_general(a, b, (((0,), (0,)), ((), ())), preferred_element_type=F32)


def _two_stream_specs(n_first):
    return (pl.BlockSpec((ROW_TILE, D_MODEL), lambda i: (jnp.minimum(i, n_first - 1), 0)),
            pl.BlockSpec((ROW_TILE, D_MODEL), lambda i: (jnp.maximum(i - n_first, 0), 0)))


def _inproj_kernel(xp_ref, xs_ref, g_ref, w_ref, u_ref, zh_ref, gab_ref, *, n_first):
    x = jnp.where(pl.program_id(0) < n_first, xp_ref[...], xs_ref[...])
    h = _rms(x, g_ref[...]).astype(BF16)
    u_ref[...] = _dot(h, w_ref[:, :S5_WIDTH])
    zh_ref[...] = _dot(h, w_ref[:, S5_WIDTH:S5_WIDTH + 4 * HG_WIDTH])
    gab_ref[...] = _dot(h, w_ref[:, S5_WIDTH + 4 * HG_WIDTH:])


def _inproj(xp, xs, g_mix, w_in_bf):
    n_first = xp.shape[0] // ROW_TILE
    t = xp.shape[0] + xs.shape[0]
    ncols = w_in_bf.shape[1]
    row = lambda i: (i, 0)
    fixed = lambda i: (0, 0)
    return pl.pallas_call(
        functools.partial(_inproj_kernel, n_first=n_first),
        grid=(t // ROW_TILE,),
        in_specs=[*_two_stream_specs(n_first),
                  pl.BlockSpec((1, D_MODEL), fixed),
                  pl.BlockSpec((D_MODEL, ncols), fixed)],
        out_specs=[pl.BlockSpec((ROW_TILE, S5_WIDTH), row),
                   pl.BlockSpec((ROW_TILE, 4 * HG_WIDTH), row),
                   pl.BlockSpec((ROW_TILE, 2 * D_MODEL), row)],
        out_shape=[jax.ShapeDtypeStruct((t, S5_WIDTH), F32),
                   jax.ShapeDtypeStruct((t, 4 * HG_WIDTH), F32),
                   jax.ShapeDtypeStruct((t, 2 * D_MODEL), F32)],
        compiler_params=_cparams(1),
        name="inproj",
    )(xp, xs, g_mix.reshape(1, D_MODEL), w_in_bf)


def _s5_kernel(u0_ref, u1_ref, u2_ref, u3_ref, s0_ref, perm_ref, pinv_ref, wb_ref, wc_ref,
               coef_ref, d_ref, wg_ref, ya_ref, sp_out, ss_out, buf, v_ref, *, n_prompt_steps):
    i = pl.program_id(0)

    @pl.when(i == 0)
    def _():
        v_ref[...] = jnp.zeros_like(v_ref)

    @pl.when(i >= n_prompt_steps)
    def _():
        v_ref[...] = s0_ref[0]

    u = jnp.concatenate([r[0] for r in (u0_ref, u1_ref, u2_ref, u3_ref)], axis=0)
    rows = 2 * u.shape[0]
    up = _dot(perm_ref[...], u.astype(BF16))
    re_half = (lax.broadcasted_iota(jnp.int32, (rows, S5_CH_TILE), 0) & S5_SEQS) == 0
    n_tiles = S5_WIDTH // S5_CH_TILE
    for n in range(n_tiles):
        un = up[:, n * S5_CH_TILE:(n + 1) * S5_CH_TILE]
        lhs = jnp.concatenate([jnp.where(re_half, un, 0.0), jnp.where(re_half, 0.0, un)], axis=1)
        buf[:, n * S5_LANE_TILE:(n + 1) * S5_LANE_TILE] = _dot(lhs.astype(BF16), wb_ref[n])

    def step(t, v):
        row = pl.multiple_of(t * SCAN_ROWS, SCAN_ROWS)
        v = (coef_ref[0] * v + coef_ref[1] * pltpu.roll(v, S5_SEQS, axis=0)
             + buf[pl.ds(row, SCAN_ROWS), :])
        buf[pl.ds(row, SCAN_ROWS), :] = v
        return v

    v = lax.fori_loop(0, S5_STEPS, step, v_ref[...], unroll=2)
    v_ref[...] = v

    @pl.when(i < n_prompt_steps)
    def _():
        sp_out[0] = v

    @pl.when(i >= n_prompt_steps)
    def _():
        ss_out[0] = v

    ws = []
    for n in range(n_tiles):
        z = _dot(buf[:, n * S5_LANE_TILE:(n + 1) * S5_LANE_TILE].astype(BF16), wc_ref[n])
        ws.append(jnp.where(re_half, z[:, :S5_CH_TILE], z[:, S5_CH_TILE:]))
    w_hi, w_lo = _split2(jnp.concatenate(ws, axis=1))
    y = _dot(pinv_ref[...], w_hi) + _dot(pinv_ref[...], w_lo) + d_ref[...] * u
    y = jax.nn.gelu(y).astype(BF16)
    p = _dot(y, wg_ref[...])
    ya = p[:, :D_MODEL] * jax.nn.sigmoid(p[:, D_MODEL:])
    ya_ref[0] = ya.reshape(S5_SEQS, S5_STEPS, D_MODEL)


def _s5(u, s0_sample, tables, d_skip, wg, *, prompt_len, n_sample_groups):
    perm, pinv, wb, wc, coef = tables
    n_prompt_steps = prompt_len // S5_STEPS
    n_steps = n_prompt_steps + n_sample_groups
    sample_base = S5_SEQS * n_prompt_steps
    u3 = u.reshape(-1, S5_STEPS, S5_WIDTH)

    def u_spec(s):
        def imap(i):
            prompt_blk = s * n_prompt_steps + i
            sample_blk = sample_base + S5_SEQS * (i - n_prompt_steps) + s
            return (jnp.where(i < n_prompt_steps, prompt_blk, sample_blk), 0, 0)
        return pl.BlockSpec((1, S5_STEPS, S5_WIDTH), imap)

    sample_grp = lambda i: (jnp.maximum(i - n_prompt_steps, 0), 0, 0)
    whole = lambda a: pl.BlockSpec(a.shape, lambda i: (0,) * a.ndim)
    state_blk = (1, 2 * S5_SEQS, S5_LANES)
    rows = S5_SEQS * S5_STEPS
    return pl.pallas_call(
        functools.partial(_s5_kernel, n_prompt_steps=n_prompt_steps),
        grid=(n_steps,),
        in_specs=[u_spec(0), u_spec(1), u_spec(2), u_spec(3),
                  pl.BlockSpec(state_blk, sample_grp),
                  whole(perm), whole(pinv), whole(wb), whole(wc), whole(coef), whole(d_skip), whole(wg)],
        out_specs=[pl.BlockSpec((1, S5_SEQS, S5_STEPS, D_MODEL), lambda i: (i, 0, 0, 0)),
                   pl.BlockSpec(state_blk, lambda i: (0, 0, 0)),
                   pl.BlockSpec(state_blk, sample_grp)],
        out_shape=[jax.ShapeDtypeStruct((n_steps, S5_SEQS, S5_STEPS, D_MODEL), F32),
                   jax.ShapeDtypeStruct(state_blk, F32),
                   jax.ShapeDtypeStruct((n_sample_groups,) + state_blk[1:], F32)],
        scratch_shapes=[pltpu.VMEM((2 * rows, S5_LANES), F32),
                        pltpu.VMEM((2 * S5_SEQS, S5_LANES), F32)],
        compiler_params=_cparams(1),
        name="s5",
    )(u3, u3, u3, u3, s0_sample, perm, pinv, wb, wc, coef, d_skip, wg)


def _s5_tables(lam_re, lam_im, log_dt, b_re, b_im, c_re, c_im):
    lr = jnp.minimum(lam_re.astype(F32), S5_DT_CLIP)
    li = lam_im.astype(F32)
    dt = jnp.exp(log_dt.astype(F32))[:, None]
    mag = jnp.exp(lr * dt)
    ar = mag * jnp.cos(li * dt)
    ai = mag * jnp.sin(li * dt)
    den = lr * lr + li * li
    zr = ((ar - 1.0) * lr + ai * li) / den
    zi = (ai * lr - (ar - 1.0) * li) / den
    br = b_re.astype(F32)
    bi = b_im.astype(F32)
    bbr = zr[..., None] * br - zi[..., None] * bi
    bbi = zr[..., None] * bi + zi[..., None] * br

    ar_f, ai_f = ar.reshape(1, -1), ai.reshape(1, -1)
    sign = jnp.where(jnp.arange(2 * S5_SEQS)[:, None] < S5_SEQS, -1.0, 1.0)
    coef = jnp.stack([jnp.broadcast_to(ar_f, (2 * S5_SEQS, S5_LANES)), sign * ai_f]).astype(F32)

    g_per_tile = S5_CH_TILE // S5_GROUP
    n_tiles = S5_GROUPS // g_per_tile
    eye = jnp.eye(g_per_tile, dtype=F32)

    def b_tile(bb):
        x = bb.reshape(n_tiles, g_per_tile, S5_STATE, S5_GROUP)
        x = x[:, :, :, :, None] * eye[None, :, None, None, :]
        return x.transpose(0, 1, 3, 4, 2).reshape(n_tiles, S5_CH_TILE, S5_LANE_TILE)

    wb = jnp.concatenate([b_tile(bbr), b_tile(bbi)], axis=1).astype(BF16)

    def c_tile(cc):
        x = cc.astype(F32).reshape(n_tiles, g_per_tile, S5_GROUP, S5_STATE)
        x = x[:, :, :, :, None] * eye[None, :, None, None, :]
        return x.transpose(0, 1, 3, 4, 2).reshape(n_tiles, S5_LANE_TILE, S5_CH_TILE)

    wc = jnp.concatenate([c_tile(c_re), -c_tile(c_im)], axis=2).astype(BF16)

    r = jnp.arange(2 * S5_SEQS * S5_STEPS)
    src = (r % S5_SEQS) * S5_STEPS + r // (2 * S5_SEQS)
    perm = (src[:, None] == jnp.arange(S5_SEQS * S5_STEPS)[None, :]).astype(BF16)
    return perm, perm.T, wb, wc, coef


def _split2(x):
    hi = x.astype(BF16)
    lo = (x - hi.astype(F32)).astype(BF16)
    return hi, lo


def _split3(x):
    hi = x.astype(BF16)
    r = x - hi.astype(F32)
    mid = r.astype(BF16)
    lo = (r - mid.astype(F32)).astype(BF16)
    return hi, mid, lo


def _hgrn_kernel(zh_ref, lb_ref, gn_ref, wo_ref, s0_ref, yb_ref, sp_out, ss_out,
                 st, gc_ref, att_ref, cross_ref, o_ref, *, n_prompt_tiles, chunks_per_seq,
                 sample_len):
    i = pl.program_id(0)
    refs = (zh_ref, lb_ref, gn_ref, wo_ref, yb_ref, st, gc_ref, att_ref, cross_ref, o_ref)

    @pl.when(i < n_prompt_tiles)
    def _():
        j = i % chunks_per_seq

        @pl.when(j == 0)
        def _():
            st[0] = jnp.zeros((HG_HEADS, HG_DK, HG_DK), F32)

        _hgrn_tile(*refs, n_seg=1, seg_len=ROW_TILE)

        @pl.when(j == chunks_per_seq - 1)
        def _():
            for h in range(HG_HEADS):
                sp_out[0, h] = st[0, h].T

    @pl.when(i >= n_prompt_tiles)
    def _():
        n_seg = ROW_TILE // sample_len
        for seg in range(n_seg):
            for h in range(HG_HEADS):
                st[seg, h] = s0_ref[seg, h].T
        _hgrn_tile(*refs, n_seg=n_seg, seg_len=sample_len)
        for seg in range(n_seg):
            for h in range(HG_HEADS):
                ss_out[seg, h] = st[seg, h].T


def _hgrn_tile(zh_ref, lb_ref, gn_ref, wo_ref, yb_ref, st, gc_ref, att_ref, cross_ref, o_ref,
               *, n_seg, seg_len):
    rows = n_seg * seg_len
    zq = zh_ref[:, 0:HG_WIDTH]
    zf = zh_ref[:, HG_WIDTH:2 * HG_WIDTH]
    v = zh_ref[:, 2 * HG_WIDTH:3 * HG_WIDTH]
    lb = lb_ref[...]
    q = zq * jax.nn.sigmoid(zq)
    f = lb + (1.0 - lb) * jax.nn.sigmoid(zf)
    k = 1.0 - f
    g = jnp.log(f)
    vb = v.astype(BF16)

    ri = lax.broadcasted_iota(jnp.int32, (rows, rows), 0)
    ci = lax.broadcasted_iota(jnp.int32, (rows, rows), 1)
    seg_shift = int(math.log2(seg_len))
    same_seg = (ri >> seg_shift) == (ci >> seg_shift)

    tri = jnp.where(same_seg & (ci <= ri), 1.0, 0.0).astype(BF16)
    g3 = jnp.concatenate(_split3(g), axis=1)
    gcs = _dot(tri, g3)
    gc = gcs[:, :HG_WIDTH] + gcs[:, HG_WIDTH:2 * HG_WIDTH] + gcs[:, 2 * HG_WIDTH:]
    gc_ref[...] = gc

    n_diag = rows // HG_DIAG
    dr = lax.broadcasted_iota(jnp.int32, (HG_DIAG, HG_DIAG), 0)
    dc = lax.broadcasted_iota(jnp.int32, (HG_DIAG, HG_DIAG), 1)
    dxor = jnp.where(dc < dr, dr ^ dc, 0)
    row_id = lax.broadcasted_iota(jnp.int32, (rows, HG_WIDTH), 0)
    qb = q.astype(BF16)
    kb = k.astype(BF16)
    for h in range(HG_HEADS):
        hs = slice(h * HG_DK, (h + 1) * HG_DK)
        for d in range(n_diag):
            ds_ = slice(d * HG_DIAG, (d + 1) * HG_DIAG)
            att_ref[h, d] = jnp.where(dr == dc, _dot_nt(qb[ds_, hs], kb[ds_, hs]), 0.0)

    g2 = jnp.concatenate(_split2(g), axis=1)
    level = seg_len // 2
    while level >= 1:
        if 2 * level >= SCAN_ROWS:
            mids, picks = [], []
            for b in range(rows // (2 * level)):
                lo = b * 2 * level
                mids.append(jnp.broadcast_to(gc_ref[lo + level - 1:lo + level, :], (2 * level, HG_WIDTH)))
                picks += [k[lo:lo + level], q[lo + level:lo + 2 * level]]
            gmid = mids[0] if len(mids) == 1 else jnp.concatenate(mids, axis=0)
            qk = jnp.concatenate(picks, axis=0)
            dexp = -jnp.abs(gc - gmid)
        else:
            mid = (ri & ~(2 * level - 1)) + (level - 1)
            upper = (ri & level) != 0
            w = (ci > jnp.where(upper, mid, ri)) & (ci <= jnp.where(upper, ri, mid))
            dsum = _dot(jnp.where(w, 1.0, 0.0).astype(BF16), g2)
            dexp = dsum[:, :HG_WIDTH] + dsum[:, HG_WIDTH:]
            qk = jnp.where((row_id & level) != 0, q, k)
        m = (qk * jnp.exp(dexp)).astype(BF16)
        if level >= HG_DIAG:
            for h in range(HG_HEADS):
                hs = slice(h * HG_DK, (h + 1) * HG_DK)
                cross_ref[h] = _dot_nt(m[level:, hs], m[:level, hs])
        else:
            sel = (dxor >> int(math.log2(level))) == 1
            for h in range(HG_HEADS):
                hs = slice(h * HG_DK, (h + 1) * HG_DK)
                for d in range(n_diag):
                    ds_ = slice(d * HG_DIAG, (d + 1) * HG_DIAG)
                    att_ref[h, d] = jnp.where(sel, _dot_nt(m[ds_, hs], m[ds_, hs]), att_ref[h, d])
        level //= 2

    qg = (q * jnp.exp(gc)).astype(BF16)
    for h in range(HG_HEADS):
        hs = slice(h * HG_DK, (h + 1) * HG_DK)
        for d in range(n_diag):
            ds_ = slice(d * HG_DIAG, (d + 1) * HG_DIAG)
            o_ref[ds_, hs] = _dot(att_ref[h, d].astype(BF16), vb[ds_, hs])
        if seg_len > HG_DIAG:
            o_ref[HG_DIAG:, hs] += _dot(cross_ref[h].astype(BF16), vb[:HG_DIAG, hs])

    for seg in range(n_seg):
        rs = slice(seg * seg_len, (seg + 1) * seg_len)
        last = (seg + 1) * seg_len - 1
        g_last = gc_ref[last:last + 1, :]
        kd = (k[rs] * jnp.exp(g_last - gc[rs])).astype(BF16)
        decay = jnp.exp(g_last)
        for h in range(HG_HEADS):
            hs = slice(h * HG_DK, (h + 1) * HG_DK)
            s_t = st[seg, h]
            o_ref[rs, hs] += _dot_nt(qg[rs, hs], s_t.astype(BF16))
            st[seg, h] = s_t * decay[:, hs] + _dot_tn(vb[rs, hs], kd[:, hs])

    zg = zh_ref[:, 3 * HG_WIDTH:]
    gate = zg * jax.nn.sigmoid(zg)
    outs = []
    for h in range(HG_HEADS):
        hs = slice(h * HG_DK, (h + 1) * HG_DK)
        outs.append(_rms(o_ref[:, hs], gn_ref[...]) * gate[:, hs])
    yb_ref[...] = _dot(jnp.concatenate(outs, axis=1).astype(BF16), wo_ref[...])


def _hgrn(zh, lb, gn, wo, s0_sample, *, n_prompt, prompt_len, sample_len):
    t = zh.shape[0]
    chunks_per_seq = prompt_len // ROW_TILE
    n_prompt_tiles = n_prompt * chunks_per_seq
    n_seg = ROW_TILE // sample_len
    n_sample = s0_sample.shape[0]
    row = lambda i: (i, 0)
    fixed = lambda i: (0, 0)
    state = (HG_HEADS, HG_DK, HG_DK)
    prompt_seq = lambda i: (jnp.minimum(i // chunks_per_seq, n_prompt - 1), 0, 0, 0)
    sample_grp = lambda i: (jnp.maximum(i - n_prompt_tiles, 0), 0, 0, 0)
    kern = functools.partial(_hgrn_kernel, n_prompt_tiles=n_prompt_tiles,
                             chunks_per_seq=chunks_per_seq, sample_len=sample_len)
    return pl.pallas_call(
        kern,
        grid=(t // ROW_TILE,),
        in_specs=[pl.BlockSpec((ROW_TILE, 4 * HG_WIDTH), row),
                  pl.BlockSpec((1, HG_WIDTH), fixed),
                  pl.BlockSpec((1, HG_DK), fixed),
                  pl.BlockSpec(wo.shape, fixed),
                  pl.BlockSpec((n_seg,) + state, sample_grp)],
        out_specs=[pl.BlockSpec((ROW_TILE, D_MODEL), row),
                   pl.BlockSpec((1,) + state, prompt_seq),
                   pl.BlockSpec((n_seg,) + state, sample_grp)],
        out_shape=[jax.ShapeDtypeStruct((t, D_MODEL), F32),
                   jax.ShapeDtypeStruct((n_prompt,) + state, F32),
                   jax.ShapeDtypeStruct((n_sample,) + state, F32)],
        scratch_shapes=[pltpu.VMEM((n_seg,) + state, F32),
                        pltpu.VMEM((ROW_TILE, HG_WIDTH), F32),
                        pltpu.VMEM((HG_HEADS, ROW_TILE // HG_DIAG, HG_DIAG, HG_DIAG), F32),
                        pltpu.VMEM((HG_HEADS, HG_DIAG, HG_DIAG), F32),
                        pltpu.VMEM((ROW_TILE, HG_WIDTH), F32)],
        compiler_params=_cparams(1),
        name="hgrn2",
    )(zh, lb, gn, wo, s0_sample)


def _merge_kernel(gab_ref, ya0_ref, ya1_ref, ya2_ref, ya3_ref, yb_ref, xp_ref, xs_ref, wo_ref,
                  gf_ref, rw_ref, rb_ref, x1_ref, xg_ref, pos_ref, gate_ref, cnt_ref, off_ref,
                  hbuf, lbuf, *, n_first, n_tiles):
    i = pl.program_id(0)
    cur = i % 2
    prev = 1 - cur

    @pl.when(i == 0)
    def _():
        hbuf[1] = jnp.zeros(hbuf.shape[1:], hbuf.dtype)
        lbuf[1] = jnp.zeros(lbuf.shape[1:], lbuf.dtype)

    logits_tok = lbuf[prev]

    ga = gab_ref[:, :D_MODEL]
    gb = gab_ref[:, D_MODEL:]
    ya = jnp.concatenate([r[0] for r in (ya0_ref, ya1_ref, ya2_ref, ya3_ref)], axis=0)
    mixed = (jax.nn.sigmoid(ga) * ya + jax.nn.sigmoid(gb) * yb_ref[...]).astype(BF16)

    rows = logits_tok.shape[0]
    logits = logits_tok.T[:N_EXPERTS] + rb_ref[...]
    eid = lax.broadcasted_iota(jnp.int32, (N_EXPERTS, rows), 0).astype(F32)
    lg = logits
    sel = jnp.zeros((N_EXPERTS, rows), F32)
    vals, idxs = [], []
    for _ in range(TOP_K):
        m = jnp.max(lg, axis=0, keepdims=True)
        idx = jnp.min(jnp.where(lg == m, eid, float(N_EXPERTS)), axis=0, keepdims=True)
        hit = eid == idx
        vals.append(m)
        idxs.append(idx)
        lg = jnp.where(hit, -jnp.inf, lg)
        sel = jnp.where(hit, 1.0, sel)

    es = [jnp.exp(vv - vals[0]) for vv in vals]
    tot = es[0] + es[1] + es[2] + es[3]

    ri = lax.broadcasted_iota(jnp.int32, (rows, rows), 0)
    ci = lax.broadcasted_iota(jnp.int32, (rows, rows), 1)
    earlier = jnp.where(ri < ci, 1.0, 0.0).astype(BF16)
    rank = _dot(sel.astype(BF16), earlier)
    gran = jnp.floor((jnp.sum(sel, axis=1, keepdims=True) + (GRANULE - 1)) * (1.0 / GRANULE))
    ei = lax.broadcasted_iota(jnp.int32, (N_EXPERTS, N_EXPERTS), 0)
    ej = lax.broadcasted_iota(jnp.int32, (N_EXPERTS, N_EXPERTS), 1)
    gran_b = jnp.broadcast_to(gran, (N_EXPERTS, rows))
    off = _dot(jnp.where(ej < ei, 1.0, 0.0).astype(BF16), gran_b.astype(BF16))
    slot_of = rank + off * float(GRANULE)
    poss = [jnp.sum(jnp.where(eid == idx, slot_of, 0.0), axis=0, keepdims=True) for idx in idxs]

    per_tok = jnp.concatenate(poss + [e / tot for e in es], axis=0).T

    x = jnp.where(jnp.minimum(i, n_tiles - 1) < n_first, xp_ref[...], xs_ref[...])
    x1 = x + _dot(mixed, wo_ref[...])
    x1_ref[...] = x1

    slot_iota = lax.broadcasted_iota(jnp.int32, (rows, TILE_SLOTS), 1).astype(F32)
    onehot = jnp.zeros((rows, TILE_SLOTS), F32)
    for kk in range(TOP_K):
        onehot = jnp.where(slot_iota == per_tok[:, kk:kk + 1], 1.0, onehot)
    sorted_rows = _dot_tn(onehot.astype(BF16), hbuf[prev])
    xg_ref[...] = jnp.where(i > n_tiles, 0.0, sorted_rows)

    a_hi, a_lo = _split2(_rms(x1, gf_ref[...]))
    hbuf[cur] = a_hi
    lbuf[cur] = _dot(a_hi, rw_ref[0]) + _dot(a_hi, rw_ref[1]) + _dot(a_lo, rw_ref[0])

    pos_ref[...] = per_tok[:, :TOP_K].astype(jnp.int32)
    gate_ref[...] = per_tok[:, TOP_K:]
    cnt_ref[0] = gran.astype(jnp.int32)
    off_ref[0] = off[:, :1].astype(jnp.int32)


def _merge(gab, ya_steps, yb, xp, xs, wo, g_ffn, rw2, rb, *, prompt_len):
    n_first = xp.shape[0] // ROW_TILE
    t = xp.shape[0] + xs.shape[0]
    n_tiles = t // ROW_TILE
    chunks_per_seq = prompt_len // ROW_TILE
    quarters = ROW_TILE // S5_STEPS
    ya3 = ya_steps.reshape(-1, S5_STEPS, D_MODEL)

    def ya_spec(q):
        def imap(i):
            i = jnp.minimum(i, n_tiles - 1)
            seq, chunk = i // chunks_per_seq, i % chunks_per_seq
            prompt_blk = (chunk * quarters + q) * S5_SEQS + seq
            return (jnp.where(i < n_first, prompt_blk, i * quarters + q), 0, 0)
        return pl.BlockSpec((1, S5_STEPS, D_MODEL), imap)

    tile = lambda i: jnp.minimum(i, n_tiles - 1)
    prev_tile = lambda i: jnp.clip(i - 1, 0, n_tiles - 1)
    row = lambda i: (tile(i), 0)
    prev_row = lambda i: (prev_tile(i), 0)
    fixed = lambda i: (0, 0)
    per_prev_tile = lambda i: (prev_tile(i), 0, 0)
    xp_spec, xs_spec = _two_stream_specs(n_first)
    return pl.pallas_call(
        functools.partial(_merge_kernel, n_first=n_first, n_tiles=n_tiles),
        grid=(n_tiles + 2,),
        in_specs=[pl.BlockSpec((ROW_TILE, 2 * D_MODEL), row),
                  ya_spec(0), ya_spec(1), ya_spec(2), ya_spec(3),
                  pl.BlockSpec((ROW_TILE, D_MODEL), row),
                  pl.BlockSpec(xp_spec.block_shape, lambda i: xp_spec.index_map(tile(i))),
                  pl.BlockSpec(xs_spec.block_shape, lambda i: xs_spec.index_map(tile(i))),
                  pl.BlockSpec((D_MODEL, D_MODEL), fixed),
                  pl.BlockSpec((1, D_MODEL), fixed),
                  pl.BlockSpec((2, D_MODEL, ROUTER_LANES), lambda i: (0, 0, 0)),
                  pl.BlockSpec((N_EXPERTS, 1), fixed)],
        out_specs=[pl.BlockSpec((ROW_TILE, D_MODEL), row),
                   pl.BlockSpec((TILE_SLOTS, D_MODEL), lambda i: (jnp.maximum(i - 1, 0), 0)),
                   pl.BlockSpec((ROW_TILE, TOP_K), prev_row),
                   pl.BlockSpec((ROW_TILE, TOP_K), prev_row),
                   pl.BlockSpec((1, N_EXPERTS, 1), per_prev_tile),
                   pl.BlockSpec((1, N_EXPERTS, 1), per_prev_tile)],
        scratch_shapes=[pltpu.VMEM((2, ROW_TILE, D_MODEL), BF16),
                        pltpu.VMEM((2, ROW_TILE, ROUTER_LANES), F32)],
        out_shape=[jax.ShapeDtypeStruct((t, D_MODEL), F32),
                   jax.ShapeDtypeStruct(((n_tiles + 1) * TILE_SLOTS, D_MODEL), F32),
                   jax.ShapeDtypeStruct((t, TOP_K), jnp.int32),
                   jax.ShapeDtypeStruct((t, TOP_K), F32),
                   jax.ShapeDtypeStruct((n_tiles, N_EXPERTS, 1), jnp.int32),
                   jax.ShapeDtypeStruct((n_tiles, N_EXPERTS, 1), jnp.int32)],
        compiler_params=_cparams(1),
        name="merge_router",
    )(gab, ya3, ya3, ya3, ya3, yb, xp, xs, wo, g_ffn, rw2, rb)


def _expert_kernel(blk_e, nxt_e, n_used, src0_ref, srcn_ref, dst_ref, xg_hbm, w1_hbm, b1_ref, w2_hbm,
                   b2_ref, yt_hbm, xbuf, obuf, gsem, ssem, w1s, w2s, wsem, w1b, w2b):
    b = pl.program_id(0)
    nu = n_used[0]

    def granule_rows(g):
        return pl.ds(pl.multiple_of(g * GRANULE, GRANULE), GRANULE)

    def block_rows(j):
        return pl.ds(j * GRANULE, GRANULE)

    def start_gather(src_ref, slot):
        for j in range(GRAN_PER_BLOCK):
            pltpu.make_async_copy(xg_hbm.at[granule_rows(src_ref[0, 0, j])],
                                  xbuf.at[slot, block_rows(j)], gsem.at[slot]).start()

    def start_scatter(slot):
        for j in range(GRAN_PER_BLOCK):
            pltpu.make_async_copy(obuf.at[slot, block_rows(j)],
                                  yt_hbm.at[granule_rows(dst_ref[0, 0, j])], ssem.at[slot]).start()

    def wait_gather(slot):
        pltpu.make_async_copy(xg_hbm.at[pl.ds(0, EXPERT_ROWS)], xbuf.at[slot], gsem.at[slot]).wait()

    def wait_scatter(slot):
        pltpu.make_async_copy(obuf.at[slot], yt_hbm.at[pl.ds(0, EXPERT_ROWS)], ssem.at[slot]).wait()

    def weight_copies(e):
        return (pltpu.make_async_copy(w1_hbm.at[e], w1s, wsem.at[0]),
                pltpu.make_async_copy(w2_hbm.at[e], w2s, wsem.at[1]))

    first = b == 0
    new_expert = first | (blk_e[b] != blk_e[jnp.maximum(b - 1, 0)])

    @pl.when(first)
    def _():
        start_gather(src0_ref, 0)
        for c in weight_copies(blk_e[0]):
            c.start(priority=1)

    @pl.when((b >= 2) & (b <= nu))
    def _():
        wait_scatter(b % 2)

    @pl.when(b == nu)
    def _():
        wait_gather(b % 2)
        wait_scatter((b - 1) % 2)

    @pl.when(b < nu)
    def _():
        slot = b % 2
        wait_gather(slot)

        @pl.when(new_expert)
        def _():
            for c in weight_copies(blk_e[b]):
                c.wait()
            w1b[...] = w1s[...].astype(BF16)
            w2b[...] = w2s[...].astype(BF16)

            @pl.when(nxt_e[b] >= 0)
            def _():
                for c in weight_copies(nxt_e[b]):
                    c.start(priority=1)

        start_gather(srcn_ref, 1 - slot)
        x = xbuf[slot].astype(BF16)
        u = _dot(x, w1b[...]) + b1_ref[0]
        a = jnp.minimum(u[:, :D_EXPERT], SWIGLU_LIMIT)
        lin = jnp.clip(u[:, D_EXPERT:], -SWIGLU_LIMIT, SWIGLU_LIMIT)
        hmid = a * jax.nn.sigmoid(SWIGLU_ALPHA * a) * (lin + 1.0)
        obuf[slot] = _dot(hmid.astype(BF16), w2b[...]) + b2_ref[0]
        start_scatter(slot)


def _experts(xg, plan, w1, b1, w2, b2):
    granule_src, granule_dst, blk_e, nxt_e, n_used = plan
    n_blocks = blk_e.shape[0]
    src3 = granule_src.reshape(n_blocks, 1, GRAN_PER_BLOCK)
    dst3 = granule_dst.reshape(n_blocks, 1, GRAN_PER_BLOCK)
    smem_blk = lambda imap: pl.BlockSpec((1, 1, GRAN_PER_BLOCK), imap, memory_space=pltpu.SMEM)
    by_expert = lambda b, be, *_: (be[b], 0, 0)
    grid_spec = pltpu.PrefetchScalarGridSpec(
        num_scalar_prefetch=3,
        grid=(n_blocks,),
        in_specs=[smem_blk(lambda b, *_: (0, 0, 0)),
                  smem_blk(lambda b, *_: (jnp.minimum(b + 1, n_blocks - 1), 0, 0)),
                  smem_blk(lambda b, *_: (b, 0, 0)),
                  pl.BlockSpec(memory_space=pl.ANY),
                  pl.BlockSpec(memory_space=pl.ANY),
                  pl.BlockSpec((1, 1, 2 * D_EXPERT), by_expert),
                  pl.BlockSpec(memory_space=pl.ANY),
                  pl.BlockSpec((1, 1, D_MODEL), by_expert)],
        out_specs=pl.BlockSpec(memory_space=pl.ANY),
        scratch_shapes=[pltpu.VMEM((2, EXPERT_ROWS, D_MODEL), F32),
                        pltpu.VMEM((2, EXPERT_ROWS, D_MODEL), F32),
                        pltpu.SemaphoreType.DMA((2,)),
                        pltpu.SemaphoreType.DMA((2,)),
                        pltpu.VMEM((D_MODEL, 2 * D_EXPERT), F32),
                        pltpu.VMEM((D_EXPERT, D_MODEL), F32),
                        pltpu.SemaphoreType.DMA((2,)),
                        pltpu.VMEM((D_MODEL, 2 * D_EXPERT), BF16),
                        pltpu.VMEM((D_EXPERT, D_MODEL), BF16)],
    )
    return pl.pallas_call(
        _expert_kernel,
        grid_spec=grid_spec,
        out_shape=jax.ShapeDtypeStruct(xg.shape, F32),
        input_output_aliases={6: 0},
        compiler_params=_cparams(1),
        name="experts",
    )(blk_e, nxt_e, n_used, src3, src3, dst3, xg, w1, b1.reshape(N_EXPERTS, 1, -1), w2,
      b2.reshape(N_EXPERTS, 1, -1))


def _combine_kernel(pos_ref, gate_ref, x1_ref, gf_ref, yt_ref, y_ref):
    rows = pos_ref.shape[0]
    slot_iota = lax.broadcasted_iota(jnp.int32, (rows, TILE_SLOTS), 1)
    pos = pos_ref[...]
    gates = gate_ref[...]
    weights = jnp.zeros((rows, TILE_SLOTS), F32)
    for kk in range(TOP_K):
        weights = jnp.where(slot_iota == pos[:, kk:kk + 1], gates[:, kk:kk + 1], weights)
    moe = _dot(weights.astype(BF16), yt_ref[...].astype(BF16))
    y_ref[...] = _rms(x1_ref[...] + moe, gf_ref[...])


def _combine(pos, gates, x1, g_final, yt, *, tile0, n_tiles):
    row = lambda i: (tile0 + i, 0)
    return pl.pallas_call(
        _combine_kernel,
        grid=(n_tiles,),
        in_specs=[pl.BlockSpec((ROW_TILE, TOP_K), row),
                  pl.BlockSpec((ROW_TILE, TOP_K), row),
                  pl.BlockSpec((ROW_TILE, D_MODEL), row),
                  pl.BlockSpec((1, D_MODEL), lambda i: (0, 0)),
                  pl.BlockSpec((TILE_SLOTS, D_MODEL), row)],
        out_specs=pl.BlockSpec((ROW_TILE, D_MODEL), lambda i: (i, 0)),
        out_shape=jax.ShapeDtypeStruct((n_tiles * ROW_TILE, D_MODEL), F32),
        compiler_params=_cparams(1),
        name="combine",
    )(pos, gates, x1, g_final.reshape(1, D_MODEL), yt)


def _granule_plan(gran, goff):
    n_tiles = gran.shape[0]
    max_gran = (n_tiles * ROW_TILE * TOP_K + (GRANULE - 1) * n_tiles * N_EXPERTS) // GRANULE
    n_blocks = max_gran // GRAN_PER_BLOCK + N_EXPERTS + 1
    n_slots = n_blocks * GRAN_PER_BLOCK
    i32 = jnp.int32
    per_e = jnp.sum(gran, axis=0)
    region = (per_e + GRAN_PER_BLOCK - 1) // GRAN_PER_BLOCK * GRAN_PER_BLOCK
    region_end = jnp.cumsum(region)
    region_start = region_end - region
    g_t = gran.T
    run_start = (region_start[:, None] + jnp.cumsum(g_t, axis=1) - g_t).reshape(-1)
    run_end = run_start + g_t.reshape(-1)
    first_gran = (jnp.arange(n_tiles, dtype=i32)[None, :] * GRAN_PER_TILE + goff.T).reshape(-1)
    shift = first_gran - run_start
    d_shift = jnp.diff(shift, prepend=0)
    d_end = jnp.diff(run_end, prepend=0)
    slot = jnp.arange(n_slots, dtype=i32)
    started = slot[:, None] >= run_start[None, :]
    granule = slot + jnp.sum(jnp.where(started, d_shift[None, :], 0), axis=1)
    valid = slot < jnp.sum(jnp.where(started, d_end[None, :], 0), axis=1)
    spare = n_tiles * GRAN_PER_TILE
    j = slot % GRAN_PER_BLOCK
    odd_blk = (slot // GRAN_PER_BLOCK) % 2
    n_zero = GRAN_PER_TILE - 2 * GRAN_PER_BLOCK
    assert n_zero > 0
    granule_src = jnp.where(valid, granule, spare + 2 * GRAN_PER_BLOCK + j % n_zero).astype(i32)
    granule_dst = jnp.where(valid, granule, spare + odd_blk * GRAN_PER_BLOCK + j).astype(i32)
    blk = jnp.arange(n_blocks, dtype=i32) * GRAN_PER_BLOCK
    blk_e = jnp.minimum(jnp.sum(blk[:, None] >= region_end[None, :], axis=1), N_EXPERTS - 1).astype(i32)
    e = jnp.arange(N_EXPERTS, dtype=i32)
    later = jnp.where((e[None, :] > e[:, None]) & (per_e[None, :] > 0), e[None, :], N_EXPERTS)
    nxt = jnp.min(later, axis=1)
    nxt = jnp.where(nxt == N_EXPERTS, -1, nxt)
    nxt_e = jnp.sum(jnp.where(blk_e[:, None] == e[None, :], nxt[None, :], 0), axis=1).astype(i32)
    n_used = (region_end[-1:] // GRAN_PER_BLOCK).astype(i32)
    return granule_src, granule_dst, blk_e, nxt_e, n_used


def kernel(x_prompt, x_sample, state_s5_re, state_s5_im, state_hgrn, g_mix, w_in, s5_lambda_re,
           s5_lambda_im, s5_log_dt, s5_b_re, s5_b_im, s5_c_re, s5_c_im, s5_d, s5_w_glu, s5_w_gate,
           hgrn_lower_bound, hgrn_norm_g, hgrn_w_out, w_out, g_ffn, router_w, router_b, moe_w1,
           moe_b1, moe_w2, moe_b2, g_final):
    depth = g_mix.shape[0]
    n_p, l_p, _ = x_prompt.shape
    n_s, l_s, _ = x_sample.shape
    assert depth == 1 and n_p == S5_SEQS and l_p % ROW_TILE == 0
    assert l_s == S5_STEPS and n_s % S5_SEQS == 0 and ROW_TILE % l_s == 0
    li = 0
    t_p, t_s = n_p * l_p, n_s * l_s
    n_groups = n_s // S5_SEQS

    lb = jnp.cumsum(jax.nn.softmax(hgrn_lower_bound.astype(F32), axis=0), axis=0)[li].reshape(1, -1)
    s5_tabs = _s5_tables(s5_lambda_re[li], s5_lambda_im[li], s5_log_dt[li], s5_b_re[li],
                         s5_b_im[li], s5_c_re[li], s5_c_im[li])
    wg = jnp.concatenate([s5_w_glu[li], s5_w_gate[li]], axis=1).astype(BF16)
    rw_pad = jnp.pad(router_w[li].astype(F32), ((0, 0), (0, ROUTER_LANES - N_EXPERTS)))
    rw_hi = rw_pad.astype(BF16)
    rw2 = jnp.stack([rw_hi, (rw_pad - rw_hi.astype(F32)).astype(BF16)])

    xp = x_prompt.reshape(t_p, D_MODEL)
    xs = x_sample.reshape(t_s, D_MODEL)
    u, zh, gab = _inproj(xp, xs, g_mix[li], w_in[li].astype(BF16))

    s0_s5 = jnp.concatenate([state_s5_re[li].reshape(n_groups, S5_SEQS, S5_LANES),
                             state_s5_im[li].reshape(n_groups, S5_SEQS, S5_LANES)], axis=1).astype(F32)
    ya, s5_p, s5_s = _s5(u, s0_s5, s5_tabs, s5_d[li].reshape(1, -1), wg,
                         prompt_len=l_p, n_sample_groups=n_groups)
    yb, hg_p, hg_s = _hgrn(zh, lb, hgrn_norm_g[li].reshape(1, -1), hgrn_w_out[li].astype(BF16),
                           state_hgrn[li].astype(F32), n_prompt=n_p, prompt_len=l_p, sample_len=l_s)

    x1, xg, pos, gates, gran, goff = _merge(
        gab, ya, yb, xp, xs, w_out[li].astype(BF16), g_ffn[li].reshape(1, -1), rw2,
        router_b[li].reshape(-1, 1).astype(F32), prompt_len=l_p)

    plan = _granule_plan(gran.reshape(-1, N_EXPERTS), goff.reshape(-1, N_EXPERTS))
    yt = _experts(xg, plan, moe_w1[li], moe_b1[li], moe_w2[li], moe_b2[li])
    n_pt = t_p // ROW_TILE
    y_p = _combine(pos, gates, x1, g_final, yt, tile0=0, n_tiles=n_pt)
    y_s = _combine(pos, gates, x1, g_final, yt, tile0=n_pt, n_tiles=t_s // ROW_TILE)

    def s5_out(st, n, dtype):
        re = st[:, :S5_SEQS].reshape(1, n, S5_GROUPS, S5_STATE).astype(dtype)
        im = st[:, S5_SEQS:].reshape(1, n, S5_GROUPS, S5_STATE).astype(dtype)
        return re, im

    dt = x_prompt.dtype
    p_re, p_im = s5_out(s5_p, n_p, dt)
    s_re, _ = s5_out(s5_s, n_s, state_s5_re.dtype)
    _, s_im = s5_out(s5_s, n_s, state_s5_im.dtype)
    return (y_p.reshape(n_p, l_p, D_MODEL), y_s.reshape(n_s, l_s, D_MODEL),
            p_re, p_im, hg_p[None].astype(dt),
            s_re, s_im, hg_s[None].astype(state_hgrn.dtype))
```

```python
import functools
import math

import jax
import jax.numpy as jnp
from jax import lax
from jax.experimental import pallas as pl
from jax.experimental.pallas import tpu as pltpu

F32 = jnp.float32
BF16 = jnp.bfloat16

D_MODEL = 1024
NORM_EPS = 1e-6
S5_WIDTH = 512
S5_GROUP = 16
S5_GROUPS = 32
S5_STATE = 64
S5_LANES = S5_GROUPS * S5_STATE
S5_DT_CLIP = -1e-4
HG_WIDTH = 512
HG_HEADS = 4
HG_DK = 128
HG_DIAG = 128
N_EXPERTS = 32
TOP_K = 4
ROUTER_LANES = 128
D_EXPERT = 1024
SWIGLU_ALPHA = 1.702
SWIGLU_LIMIT = 7.0

ROW_TILE = 256
SCAN_ROWS = 8
S5_SEQS = 4
S5_STEPS = 64
S5_CH_TILE = 128
S5_LANE_TILE = 512
EXPERT_ROWS = 256
GRANULE = 8
TILE_SLOTS = ROW_TILE * TOP_K + N_EXPERTS * GRANULE
GRAN_PER_TILE = TILE_SLOTS // GRANULE
GRAN_PER_BLOCK = EXPERT_ROWS // GRANULE
PACK = 2
U32 = jnp.uint32
VMEM_LIMIT = 56 * 1024 * 1024


def _cparams(n_axes):
    return pltpu.CompilerParams(dimension_semantics=("arbitrary",) * n_axes,
                                vmem_limit_bytes=VMEM_LIMIT)


def _rms(x, g):
    r = lax.rsqrt(jnp.mean(x * x, axis=-1, keepdims=True) + NORM_EPS)
    return x * r * g


def _dot(a, b):
    return jnp.dot(a, b, preferred_element_type=F32)


def _dot_nt(a, b):
    return lax.dot_general(a, b, (((1,), (1,)), ((), ())), preferred_element_type=F32)


def _dot_tn(a, b):
    return lax.dot_general(a, b, (((0,), (0,)), ((), ())), preferred_element_type=F32)


def _two_stream_specs(n_first):
    return (pl.BlockSpec((ROW_TILE, D_MODEL), lambda i: (jnp.minimum(i, n_first - 1), 0)),
            pl.BlockSpec((ROW_TILE, D_MODEL), lambda i: (jnp.maximum(i - n_first, 0), 0)))


def _inproj_kernel(xp_ref, xs_ref, g_ref, w_ref, u_ref, zh_ref, gab_ref, *, n_first):
    x = jnp.where(pl.program_id(0) < n_first, xp_ref[...], xs_ref[...])
    h = _rms(x, g_ref[...]).astype(BF16)
    u_ref[...] = _dot(h, w_ref[:, :S5_WIDTH])
    zh_ref[...] = _dot(h, w_ref[:, S5_WIDTH:S5_WIDTH + 4 * HG_WIDTH])
    gab_ref[...] = _dot(h, w_ref[:, S5_WIDTH + 4 * HG_WIDTH:])


def _inproj(xp, xs, g_mix, w_in_bf):
    n_first = xp.shape[0] // ROW_TILE
    t = xp.shape[0] + xs.shape[0]
    ncols = w_in_bf.shape[1]
    row = lambda i: (i, 0)
    fixed = lambda i: (0, 0)
    return pl.pallas_call(
        functools.partial(_inproj_kernel, n_first=n_first),
        grid=(t // ROW_TILE,),
        in_specs=[*_two_stream_specs(n_first),
                  pl.BlockSpec((1, D_MODEL), fixed),
                  pl.BlockSpec((D_MODEL, ncols), fixed)],
        out_specs=[pl.BlockSpec((ROW_TILE, S5_WIDTH), row),
                   pl.BlockSpec((ROW_TILE, 4 * HG_WIDTH), row),
                   pl.BlockSpec((ROW_TILE, 2 * D_MODEL), row)],
        out_shape=[jax.ShapeDtypeStruct((t, S5_WIDTH), F32),
                   jax.ShapeDtypeStruct((t, 4 * HG_WIDTH), F32),
                   jax.ShapeDtypeStruct((t, 2 * D_MODEL), F32)],
        compiler_params=_cparams(1),
        name="inproj",
    )(xp, xs, g_mix.reshape(1, D_MODEL), w_in_bf)


def _s5_kernel(u0_ref, u1_ref, u2_ref, u3_ref, s0_ref, perm_ref, pinv_ref, wb_ref, wc_ref,
               coef_ref, d_ref, wg_ref, ya_ref, sp_out, ss_out, buf, v_ref, *, n_prompt_steps):
    i = pl.program_id(0)

    @pl.when(i == 0)
    def _():
        v_ref[...] = jnp.zeros_like(v_ref)

    @pl.when(i >= n_prompt_steps)
    def _():
        v_ref[...] = s0_ref[0]

    u = jnp.concatenate([r[0] for r in (u0_ref, u1_ref, u2_ref, u3_ref)], axis=0)
    rows = 2 * u.shape[0]
    up = _dot(perm_ref[...], u.astype(BF16))
    re_half = (lax.broadcasted_iota(jnp.int32, (rows, S5_CH_TILE), 0) & S5_SEQS) == 0
    n_tiles = S5_WIDTH // S5_CH_TILE
    for n in range(n_tiles):
        un = up[:, n * S5_CH_TILE:(n + 1) * S5_CH_TILE]
        lhs = jnp.concatenate([jnp.where(re_half, un, 0.0), jnp.where(re_half, 0.0, un)], axis=1)
        buf[:, n * S5_LANE_TILE:(n + 1) * S5_LANE_TILE] = _dot(lhs.astype(BF16), wb_ref[n])

    def step(t, v):
        row = pl.multiple_of(t * SCAN_ROWS, SCAN_ROWS)
        v = (coef_ref[0] * v + coef_ref[1] * pltpu.roll(v, S5_SEQS, axis=0)
             + buf[pl.ds(row, SCAN_ROWS), :])
        buf[pl.ds(row, SCAN_ROWS), :] = v
        return v

    v = lax.fori_loop(0, S5_STEPS, step, v_ref[...], unroll=2)
    v_ref[...] = v

    @pl.when(i < n_prompt_steps)
    def _():
        sp_out[0] = v

    @pl.when(i >= n_prompt_steps)
    def _():
        ss_out[0] = v

    ws = []
    for n in range(n_tiles):
        z = _dot(buf[:, n * S5_LANE_TILE:(n + 1) * S5_LANE_TILE].astype(BF16), wc_ref[n])
        ws.append(jnp.where(re_half, z[:, :S5_CH_TILE], z[:, S5_CH_TILE:]))
    w_hi, w_lo = _split2(jnp.concatenate(ws, axis=1))
    y = _dot(pinv_ref[...], w_hi) + _dot(pinv_ref[...], w_lo) + d_ref[...] * u
    y = jax.nn.gelu(y).astype(BF16)
    p = _dot(y, wg_ref[...])
    ya = p[:, :D_MODEL] * jax.nn.sigmoid(p[:, D_MODEL:])
    ya_ref[0] = ya.reshape(S5_SEQS, S5_STEPS, D_MODEL)


def _s5(u, s0_sample, tables, d_skip, wg, *, prompt_len, n_sample_groups):
    perm, pinv, wb, wc, coef = tables
    n_prompt_steps = prompt_len // S5_STEPS
    n_steps = n_prompt_steps + n_sample_groups
    sample_base = S5_SEQS * n_prompt_steps
    u3 = u.reshape(-1, S5_STEPS, S5_WIDTH)

    def u_spec(s):
        def imap(i):
            prompt_blk = s * n_prompt_steps + i
            sample_blk = sample_base + S5_SEQS * (i - n_prompt_steps) + s
            return (jnp.where(i < n_prompt_steps, prompt_blk, sample_blk), 0, 0)
        return pl.BlockSpec((1, S5_STEPS, S5_WIDTH), imap)

    sample_grp = lambda i: (jnp.maximum(i - n_prompt_steps, 0), 0, 0)
    whole = lambda a: pl.BlockSpec(a.shape, lambda i: (0,) * a.ndim)
    state_blk = (1, 2 * S5_SEQS, S5_LANES)
    rows = S5_SEQS * S5_STEPS
    return pl.pallas_call(
        functools.partial(_s5_kernel, n_prompt_steps=n_prompt_steps),
        grid=(n_steps,),
        in_specs=[u_spec(0), u_spec(1), u_spec(2), u_spec(3),
                  pl.BlockSpec(state_blk, sample_grp),
                  whole(perm), whole(pinv), whole(wb), whole(wc), whole(coef), whole(d_skip), whole(wg)],
        out_specs=[pl.BlockSpec((1, S5_SEQS, S5_STEPS, D_MODEL), lambda i: (i, 0, 0, 0)),
                   pl.BlockSpec(state_blk, lambda i: (0, 0, 0)),
                   pl.BlockSpec(state_blk, sample_grp)],
        out_shape=[jax.ShapeDtypeStruct((n_steps, S5_SEQS, S5_STEPS, D_MODEL), F32),
                   jax.ShapeDtypeStruct(state_blk, F32),
                   jax.ShapeDtypeStruct((n_sample_groups,) + state_blk[1:], F32)],
        scratch_shapes=[pltpu.VMEM((2 * rows, S5_LANES), F32),
                        pltpu.VMEM((2 * S5_SEQS, S5_LANES), F32)],
        compiler_params=_cparams(1),
        name="s5",
    )(u3, u3, u3, u3, s0_sample, perm, pinv, wb, wc, coef, d_skip, wg)


def _s5_tables(lam_re, lam_im, log_dt, b_re, b_im, c_re, c_im):
    lr = jnp.minimum(lam_re.astype(F32), S5_DT_CLIP)
    li = lam_im.astype(F32)
    dt = jnp.exp(log_dt.astype(F32))[:, None]
    mag = jnp.exp(lr * dt)
    ar = mag * jnp.cos(li * dt)
    ai = mag * jnp.sin(li * dt)
    den = lr * lr + li * li
    zr = ((ar - 1.0) * lr + ai * li) / den
    zi = (ai * lr - (ar - 1.0) * li) / den
    br = b_re.astype(F32)
    bi = b_im.astype(F32)
    bbr = zr[..., None] * br - zi[..., None] * bi
    bbi = zr[..., None] * bi + zi[..., None] * br

    ar_f, ai_f = ar.reshape(1, -1), ai.reshape(1, -1)
    sign = jnp.where(jnp.arange(2 * S5_SEQS)[:, None] < S5_SEQS, -1.0, 1.0)
    coef = jnp.stack([jnp.broadcast_to(ar_f, (2 * S5_SEQS, S5_LANES)), sign * ai_f]).astype(F32)

    g_per_tile = S5_CH_TILE // S5_GROUP
    n_tiles = S5_GROUPS // g_per_tile
    eye = jnp.eye(g_per_tile, dtype=F32)

    def b_tile(bb):
        x = bb.reshape(n_tiles, g_per_tile, S5_STATE, S5_GROUP)
        x = x[:, :, :, :, None] * eye[None, :, None, None, :]
        return x.transpose(0, 1, 3, 4, 2).reshape(n_tiles, S5_CH_TILE, S5_LANE_TILE)

    wb = jnp.concatenate([b_tile(bbr), b_tile(bbi)], axis=1).astype(BF16)

    def c_tile(cc):
        x = cc.astype(F32).reshape(n_tiles, g_per_tile, S5_GROUP, S5_STATE)
        x = x[:, :, :, :, None] * eye[None, :, None, None, :]
        return x.transpose(0, 1, 3, 4, 2).reshape(n_tiles, S5_LANE_TILE, S5_CH_TILE)

    wc = jnp.concatenate([c_tile(c_re), -c_tile(c_im)], axis=2).astype(BF16)

    r = jnp.arange(2 * S5_SEQS * S5_STEPS)
    src = (r % S5_SEQS) * S5_STEPS + r // (2 * S5_SEQS)
    perm = (src[:, None] == jnp.arange(S5_SEQS * S5_STEPS)[None, :]).astype(BF16)
    return perm, perm.T, wb, wc, coef


def _split2(x):
    hi = x.astype(BF16)
    lo = (x - hi.astype(F32)).astype(BF16)
    return hi, lo


def _split3(x):
    hi = x.astype(BF16)
    r = x - hi.astype(F32)
    mid = r.astype(BF16)
    lo = (r - mid.astype(F32)).astype(BF16)
    return hi, mid, lo


def _hgrn_kernel(zh_ref, lb_ref, gn_ref, wo_ref, s0_ref, yb_ref, sp_out, ss_out,
                 st, gc_ref, att_ref, cross_ref, o_ref, *, n_prompt_tiles, chunks_per_seq,
                 sample_len):
    i = pl.program_id(0)
    refs = (zh_ref, lb_ref, gn_ref, wo_ref, yb_ref, st, gc_ref, att_ref, cross_ref, o_ref)

    @pl.when(i < n_prompt_tiles)
    def _():
        j = i % chunks_per_seq

        @pl.when(j == 0)
        def _():
            st[0] = jnp.zeros((HG_HEADS, HG_DK, HG_DK), F32)

        _hgrn_tile(*refs, n_seg=1, seg_len=ROW_TILE)

        @pl.when(j == chunks_per_seq - 1)
        def _():
            for h in range(HG_HEADS):
                sp_out[0, h] = st[0, h].T

    @pl.when(i >= n_prompt_tiles)
    def _():
        n_seg = ROW_TILE // sample_len
        for seg in range(n_seg):
            for h in range(HG_HEADS):
                st[seg, h] = s0_ref[seg, h].T
        _hgrn_tile(*refs, n_seg=n_seg, seg_len=sample_len)
        for seg in range(n_seg):
            for h in range(HG_HEADS):
                ss_out[seg, h] = st[seg, h].T


def _hgrn_tile(zh_ref, lb_ref, gn_ref, wo_ref, yb_ref, st, gc_ref, att_ref, cross_ref, o_ref,
               *, n_seg, seg_len):
    rows = n_seg * seg_len
    zq = zh_ref[:, 0:HG_WIDTH]
    zf = zh_ref[:, HG_WIDTH:2 * HG_WIDTH]
    v = zh_ref[:, 2 * HG_WIDTH:3 * HG_WIDTH]
    lb = lb_ref[...]
    q = zq * jax.nn.sigmoid(zq)
    f = lb + (1.0 - lb) * jax.nn.sigmoid(zf)
    k = 1.0 - f
    g = jnp.log(f)
    vb = v.astype(BF16)

    ri = lax.broadcasted_iota(jnp.int32, (rows, rows), 0)
    ci = lax.broadcasted_iota(jnp.int32, (rows, rows), 1)
    seg_shift = int(math.log2(seg_len))
    same_seg = (ri >> seg_shift) == (ci >> seg_shift)

    tri = jnp.where(same_seg & (ci <= ri), 1.0, 0.0).astype(BF16)
    g3 = jnp.concatenate(_split3(g), axis=1)
    gcs = _dot(tri, g3)
    gc = gcs[:, :HG_WIDTH] + gcs[:, HG_WIDTH:2 * HG_WIDTH] + gcs[:, 2 * HG_WIDTH:]
    gc_ref[...] = gc

    n_diag = rows // HG_DIAG
    dr = lax.broadcasted_iota(jnp.int32, (HG_DIAG, HG_DIAG), 0)
    dc = lax.broadcasted_iota(jnp.int32, (HG_DIAG, HG_DIAG), 1)
    dxor = jnp.where(dc < dr, dr ^ dc, 0)
    row_id = lax.broadcasted_iota(jnp.int32, (rows, HG_WIDTH), 0)
    qb = q.astype(BF16)
    kb = k.astype(BF16)
    for h in range(HG_HEADS):
        hs = slice(h * HG_DK, (h + 1) * HG_DK)
        for d in range(n_diag):
            ds_ = slice(d * HG_DIAG, (d + 1) * HG_DIAG)
            att_ref[h, d] = jnp.where(dr == dc, _dot_nt(qb[ds_, hs], kb[ds_, hs]), 0.0)

    g2 = jnp.concatenate(_split2(g), axis=1)
    level = seg_len // 2
    while level >= 1:
        if 2 * level >= SCAN_ROWS:
            mids, picks = [], []
            for b in range(rows // (2 * level)):
                lo = b * 2 * level
                mids.append(jnp.broadcast_to(gc_ref[lo + level - 1:lo + level, :], (2 * level, HG_WIDTH)))
                picks += [k[lo:lo + level], q[lo + level:lo + 2 * level]]
            gmid = mids[0] if len(mids) == 1 else jnp.concatenate(mids, axis=0)
            qk = jnp.concatenate(picks, axis=0)
            dexp = -jnp.abs(gc - gmid)
        else:
            mid = (ri & ~(2 * level - 1)) + (level - 1)
            upper = (ri & level) != 0
            w = (ci > jnp.where(upper, mid, ri)) & (ci <= jnp.where(upper, ri, mid))
            dsum = _dot(jnp.where(w, 1.0, 0.0).astype(BF16), g2)
            dexp = dsum[:, :HG_WIDTH] + dsum[:, HG_WIDTH:]
            qk = jnp.where((row_id & level) != 0, q, k)
        m = (qk * jnp.exp(dexp)).astype(BF16)
        if level >= HG_DIAG:
            for h in range(HG_HEADS):
                hs = slice(h * HG_DK, (h + 1) * HG_DK)
                cross_ref[h] = _dot_nt(m[level:, hs], m[:level, hs])
        else:
            sel = (dxor >> int(math.log2(level))) == 1
            for h in range(HG_HEADS):
                hs = slice(h * HG_DK, (h + 1) * HG_DK)
                for d in range(n_diag):
                    ds_ = slice(d * HG_DIAG, (d + 1) * HG_DIAG)
                    att_ref[h, d] = jnp.where(sel, _dot_nt(m[ds_, hs], m[ds_, hs]), att_ref[h, d])
        level //= 2

    qg = (q * jnp.exp(gc)).astype(BF16)
    for h in range(HG_HEADS):
        hs = slice(h * HG_DK, (h + 1) * HG_DK)
        for d in range(n_diag):
            ds_ = slice(d * HG_DIAG, (d + 1) * HG_DIAG)
            o_ref[ds_, hs] = _dot(att_ref[h, d].astype(BF16), vb[ds_, hs])
        if seg_len > HG_DIAG:
            o_ref[HG_DIAG:, hs] += _dot(cross_ref[h].astype(BF16), vb[:HG_DIAG, hs])

    for seg in range(n_seg):
        rs = slice(seg * seg_len, (seg + 1) * seg_len)
        last = (seg + 1) * seg_len - 1
        g_last = gc_ref[last:last + 1, :]
        kd = (k[rs] * jnp.exp(g_last - gc[rs])).astype(BF16)
        decay = jnp.exp(g_last)
        for h in range(HG_HEADS):
            hs = slice(h * HG_DK, (h + 1) * HG_DK)
            s_t = st[seg, h]
            o_ref[rs, hs] += _dot_nt(qg[rs, hs], s_t.astype(BF16))
            st[seg, h] = s_t * decay[:, hs] + _dot_tn(vb[rs, hs], kd[:, hs])

    zg = zh_ref[:, 3 * HG_WIDTH:]
    gate = zg * jax.nn.sigmoid(zg)
    outs = []
    for h in range(HG_HEADS):
        hs = slice(h * HG_DK, (h + 1) * HG_DK)
        outs.append(_rms(o_ref[:, hs], gn_ref[...]) * gate[:, hs])
    yb_ref[...] = _dot(jnp.concatenate(outs, axis=1).astype(BF16), wo_ref[...])


def _hgrn(zh, lb, gn, wo, s0_sample, *, n_prompt, prompt_len, sample_len):
    t = zh.shape[0]
    chunks_per_seq = prompt_len // ROW_TILE
    n_prompt_tiles = n_prompt * chunks_per_seq
    n_seg = ROW_TILE // sample_len
    n_sample = s0_sample.shape[0]
    row = lambda i: (i, 0)
    fixed = lambda i: (0, 0)
    state = (HG_HEADS, HG_DK, HG_DK)
    prompt_seq = lambda i: (jnp.minimum(i // chunks_per_seq, n_prompt - 1), 0, 0, 0)
    sample_grp = lambda i: (jnp.maximum(i - n_prompt_tiles, 0), 0, 0, 0)
    kern = functools.partial(_hgrn_kernel, n_prompt_tiles=n_prompt_tiles,
                             chunks_per_seq=chunks_per_seq, sample_len=sample_len)
    return pl.pallas_call(
        kern,
        grid=(t // ROW_TILE,),
        in_specs=[pl.BlockSpec((ROW_TILE, 4 * HG_WIDTH), row),
                  pl.BlockSpec((1, HG_WIDTH), fixed),
                  pl.BlockSpec((1, HG_DK), fixed),
                  pl.BlockSpec(wo.shape, fixed),
                  pl.BlockSpec((n_seg,) + state, sample_grp)],
        out_specs=[pl.BlockSpec((ROW_TILE, D_MODEL), row),
                   pl.BlockSpec((1,) + state, prompt_seq),
                   pl.BlockSpec((n_seg,) + state, sample_grp)],
        out_shape=[jax.ShapeDtypeStruct((t, D_MODEL), F32),
                   jax.ShapeDtypeStruct((n_prompt,) + state, F32),
                   jax.ShapeDtypeStruct((n_sample,) + state, F32)],
        scratch_shapes=[pltpu.VMEM((n_seg,) + state, F32),
                        pltpu.VMEM((ROW_TILE, HG_WIDTH), F32),
                        pltpu.VMEM((HG_HEADS, ROW_TILE // HG_DIAG, HG_DIAG, HG_DIAG), F32),
                        pltpu.VMEM((HG_HEADS, HG_DIAG, HG_DIAG), F32),
                        pltpu.VMEM((ROW_TILE, HG_WIDTH), F32)],
        compiler_params=_cparams(1),
        name="hgrn2",
    )(zh, lb, gn, wo, s0_sample)


def _merge_kernel(gab_ref, ya0_ref, ya1_ref, ya2_ref, ya3_ref, yb_ref, xp_ref, xs_ref, wo_ref,
                  gf_ref, rw_ref, rb_ref, x1_ref, xg_ref, pos_ref, gate_ref, cnt_ref, off_ref,
                  hbuf, lbuf, *, n_first, n_tiles):
    i = pl.program_id(0)
    cur = i % 2
    prev = 1 - cur

    @pl.when(i == 0)
    def _():
        hbuf[1] = jnp.zeros(hbuf.shape[1:], hbuf.dtype)
        lbuf[1] = jnp.zeros(lbuf.shape[1:], lbuf.dtype)

    logits_tok = lbuf[prev]

    ga = gab_ref[:, :D_MODEL]
    gb = gab_ref[:, D_MODEL:]
    ya = jnp.concatenate([r[0] for r in (ya0_ref, ya1_ref, ya2_ref, ya3_ref)], axis=0)
    mixed = (jax.nn.sigmoid(ga) * ya + jax.nn.sigmoid(gb) * yb_ref[...]).astype(BF16)

    rows = logits_tok.shape[0]
    logits = logits_tok.T[:N_EXPERTS] + rb_ref[...]
    eid = lax.broadcasted_iota(jnp.int32, (N_EXPERTS, rows), 0).astype(F32)
    lg = logits
    sel = jnp.zeros((N_EXPERTS, rows), F32)
    vals, idxs = [], []
    for _ in range(TOP_K):
        m = jnp.max(lg, axis=0, keepdims=True)
        idx = jnp.min(jnp.where(lg == m, eid, float(N_EXPERTS)), axis=0, keepdims=True)
        hit = eid == idx
        vals.append(m)
        idxs.append(idx)
        lg = jnp.where(hit, -jnp.inf, lg)
        sel = jnp.where(hit, 1.0, sel)

    es = [jnp.exp(vv - vals[0]) for vv in vals]
    tot = es[0] + es[1] + es[2] + es[3]

    ri = lax.broadcasted_iota(jnp.int32, (rows, rows), 0)
    ci = lax.broadcasted_iota(jnp.int32, (rows, rows), 1)
    earlier = jnp.where(ri < ci, 1.0, 0.0).astype(BF16)
    rank = _dot(sel.astype(BF16), earlier)
    gran = jnp.floor((jnp.sum(sel, axis=1, keepdims=True) + (GRANULE - 1)) * (1.0 / GRANULE))
    ei = lax.broadcasted_iota(jnp.int32, (N_EXPERTS, N_EXPERTS), 0)
    ej = lax.broadcasted_iota(jnp.int32, (N_EXPERTS, N_EXPERTS), 1)
    gran_b = jnp.broadcast_to(gran, (N_EXPERTS, rows))
    off = _dot(jnp.where(ej < ei, 1.0, 0.0).astype(BF16), gran_b.astype(BF16))
    slot_of = rank + off * float(GRANULE)
    poss = [jnp.sum(jnp.where(eid == idx, slot_of, 0.0), axis=0, keepdims=True) for idx in idxs]

    per_tok = jnp.concatenate(poss + [e / tot for e in es], axis=0).T

    x = jnp.where(jnp.minimum(i, n_tiles - 1) < n_first, xp_ref[...], xs_ref[...])
    x1 = x + _dot(mixed, wo_ref[...])
    x1_ref[...] = x1

    slot_iota = lax.broadcasted_iota(jnp.int32, (rows, TILE_SLOTS), 1).astype(F32)
    onehot = jnp.zeros((rows, TILE_SLOTS), F32)
    for kk in range(TOP_K):
        onehot = jnp.where(slot_iota == per_tok[:, kk:kk + 1], 1.0, onehot)
    sorted_rows = _dot_tn(onehot.astype(BF16), hbuf[prev])
    xg_ref[...] = pltpu.bitcast(jnp.where(i > n_tiles, 0.0, sorted_rows).astype(BF16), U32)

    a_hi, a_lo = _split2(_rms(x1, gf_ref[...]))
    hbuf[cur] = a_hi
    lbuf[cur] = _dot(a_hi, rw_ref[0]) + _dot(a_hi, rw_ref[1]) + _dot(a_lo, rw_ref[0])

    pos_ref[...] = per_tok[:, :TOP_K].astype(jnp.int32)
    gate_ref[...] = per_tok[:, TOP_K:]
    cnt_ref[0] = gran.astype(jnp.int32)
    off_ref[0] = off[:, :1].astype(jnp.int32)


def _merge(gab, ya_steps, yb, xp, xs, wo, g_ffn, rw2, rb, *, prompt_len):
    n_first = xp.shape[0] // ROW_TILE
    t = xp.shape[0] + xs.shape[0]
    n_tiles = t // ROW_TILE
    chunks_per_seq = prompt_len // ROW_TILE
    quarters = ROW_TILE // S5_STEPS
    ya3 = ya_steps.reshape(-1, S5_STEPS, D_MODEL)

    def ya_spec(q):
        def imap(i):
            i = jnp.minimum(i, n_tiles - 1)
            seq, chunk = i // chunks_per_seq, i % chunks_per_seq
            prompt_blk = (chunk * quarters + q) * S5_SEQS + seq
            return (jnp.where(i < n_first, prompt_blk, i * quarters + q), 0, 0)
        return pl.BlockSpec((1, S5_STEPS, D_MODEL), imap)

    tile = lambda i: jnp.minimum(i, n_tiles - 1)
    prev_tile = lambda i: jnp.clip(i - 1, 0, n_tiles - 1)
    row = lambda i: (tile(i), 0)
    prev_row = lambda i: (prev_tile(i), 0)
    fixed = lambda i: (0, 0)
    per_prev_tile = lambda i: (prev_tile(i), 0, 0)
    xp_spec, xs_spec = _two_stream_specs(n_first)
    return pl.pallas_call(
        functools.partial(_merge_kernel, n_first=n_first, n_tiles=n_tiles),
        grid=(n_tiles + 2,),
        in_specs=[pl.BlockSpec((ROW_TILE, 2 * D_MODEL), row),
                  ya_spec(0), ya_spec(1), ya_spec(2), ya_spec(3),
                  pl.BlockSpec((ROW_TILE, D_MODEL), row),
                  pl.BlockSpec(xp_spec.block_shape, lambda i: xp_spec.index_map(tile(i))),
                  pl.BlockSpec(xs_spec.block_shape, lambda i: xs_spec.index_map(tile(i))),
                  pl.BlockSpec((D_MODEL, D_MODEL), fixed),
                  pl.BlockSpec((1, D_MODEL), fixed),
                  pl.BlockSpec((2, D_MODEL, ROUTER_LANES), lambda i: (0, 0, 0)),
                  pl.BlockSpec((N_EXPERTS, 1), fixed)],
        out_specs=[pl.BlockSpec((ROW_TILE, D_MODEL), row),
                   pl.BlockSpec((TILE_SLOTS // PACK, D_MODEL), lambda i: (jnp.maximum(i - 1, 0), 0)),
                   pl.BlockSpec((ROW_TILE, TOP_K), prev_row),
                   pl.BlockSpec((ROW_TILE, TOP_K), prev_row),
                   pl.BlockSpec((1, N_EXPERTS, 1), per_prev_tile),
                   pl.BlockSpec((1, N_EXPERTS, 1), per_prev_tile)],
        scratch_shapes=[pltpu.VMEM((2, ROW_TILE, D_MODEL), BF16),
                        pltpu.VMEM((2, ROW_TILE, ROUTER_LANES), F32)],
        out_shape=[jax.ShapeDtypeStruct((t, D_MODEL), F32),
                   jax.ShapeDtypeStruct(((n_tiles + 1) * TILE_SLOTS // PACK, D_MODEL), U32),
                   jax.ShapeDtypeStruct((t, TOP_K), jnp.int32),
                   jax.ShapeDtypeStruct((t, TOP_K), F32),
                   jax.ShapeDtypeStruct((n_tiles, N_EXPERTS, 1), jnp.int32),
                   jax.ShapeDtypeStruct((n_tiles, N_EXPERTS, 1), jnp.int32)],
        compiler_params=_cparams(1),
        name="merge_router",
    )(gab, ya3, ya3, ya3, ya3, yb, xp, xs, wo, g_ffn, rw2, rb)


def _expert_kernel(blk_e, nxt_e, n_used, src0_ref, srcn_ref, dst_ref, xg_hbm, w1_hbm, b1_ref, w2_hbm,
                   b2_ref, yt_hbm, xbuf, obuf, gsem, ssem, w1s, w2s, wsem, w1b, w2b):
    b = pl.program_id(0)
    nu = n_used[0]

    gran_rows = GRANULE // PACK
    blk_rows = EXPERT_ROWS // PACK

    def granule_rows(g):
        return pl.ds(pl.multiple_of(g * gran_rows, gran_rows), gran_rows)

    def block_rows(j):
        return pl.ds(j * gran_rows, gran_rows)

    def start_gather(src_ref, slot):
        for j in range(GRAN_PER_BLOCK):
            pltpu.make_async_copy(xg_hbm.at[granule_rows(src_ref[0, 0, j])],
                                  xbuf.at[slot, block_rows(j)], gsem.at[slot]).start()

    def start_scatter(slot):
        for j in range(GRAN_PER_BLOCK):
            pltpu.make_async_copy(obuf.at[slot, block_rows(j)],
                                  yt_hbm.at[granule_rows(dst_ref[0, 0, j])], ssem.at[slot]).start()

    def wait_gather(slot):
        pltpu.make_async_copy(xg_hbm.at[pl.ds(0, blk_rows)], xbuf.at[slot], gsem.at[slot]).wait()

    def wait_scatter(slot):
        pltpu.make_async_copy(obuf.at[slot], yt_hbm.at[pl.ds(0, blk_rows)], ssem.at[slot]).wait()

    def weight_copies(e):
        return (pltpu.make_async_copy(w1_hbm.at[e], w1s, wsem.at[0]),
                pltpu.make_async_copy(w2_hbm.at[e], w2s, wsem.at[1]))

    first = b == 0
    new_expert = first | (blk_e[b] != blk_e[jnp.maximum(b - 1, 0)])

    @pl.when(first)
    def _():
        start_gather(src0_ref, 0)
        for c in weight_copies(blk_e[0]):
            c.start(priority=1)

    @pl.when((b >= 2) & (b <= nu))
    def _():
        wait_scatter(b % 2)

    @pl.when(b == nu)
    def _():
        wait_gather(b % 2)
        wait_scatter((b - 1) % 2)

    @pl.when(b < nu)
    def _():
        slot = b % 2
        wait_gather(slot)

        @pl.when(new_expert)
        def _():
            for c in weight_copies(blk_e[b]):
                c.wait()
            w1b[...] = w1s[...].astype(BF16)
            w2b[...] = w2s[...].astype(BF16)

            @pl.when(nxt_e[b] >= 0)
            def _():
                for c in weight_copies(nxt_e[b]):
                    c.start(priority=1)

        start_gather(srcn_ref, 1 - slot)
        x = pltpu.bitcast(xbuf[slot], BF16)
        u = _dot(x, w1b[...]) + b1_ref[0]
        a = jnp.minimum(u[:, :D_EXPERT], SWIGLU_LIMIT)
        lin = jnp.clip(u[:, D_EXPERT:], -SWIGLU_LIMIT, SWIGLU_LIMIT)
        hmid = a * jax.nn.sigmoid(SWIGLU_ALPHA * a) * (lin + 1.0)
        y = _dot(hmid.astype(BF16), w2b[...]) + b2_ref[0]
        obuf[slot] = pltpu.bitcast(y.astype(BF16), U32)
        start_scatter(slot)


def _experts(xg, plan, w1, b1, w2, b2):
    granule_src, granule_dst, blk_e, nxt_e, n_used = plan
    n_blocks = blk_e.shape[0]
    src3 = granule_src.reshape(n_blocks, 1, GRAN_PER_BLOCK)
    dst3 = granule_dst.reshape(n_blocks, 1, GRAN_PER_BLOCK)
    smem_blk = lambda imap: pl.BlockSpec((1, 1, GRAN_PER_BLOCK), imap, memory_space=pltpu.SMEM)
    by_expert = lambda b, be, *_: (be[b], 0, 0)
    grid_spec = pltpu.PrefetchScalarGridSpec(
        num_scalar_prefetch=3,
        grid=(n_blocks,),
        in_specs=[smem_blk(lambda b, *_: (0, 0, 0)),
                  smem_blk(lambda b, *_: (jnp.minimum(b + 1, n_blocks - 1), 0, 0)),
                  smem_blk(lambda b, *_: (b, 0, 0)),
                  pl.BlockSpec(memory_space=pl.ANY),
                  pl.BlockSpec(memory_space=pl.ANY),
                  pl.BlockSpec((1, 1, 2 * D_EXPERT), by_expert),
                  pl.BlockSpec(memory_space=pl.ANY),
                  pl.BlockSpec((1, 1, D_MODEL), by_expert)],
        out_specs=pl.BlockSpec(memory_space=pl.ANY),
        scratch_shapes=[pltpu.VMEM((2, EXPERT_ROWS // PACK, D_MODEL), U32),
                        pltpu.VMEM((2, EXPERT_ROWS // PACK, D_MODEL), U32),
                        pltpu.SemaphoreType.DMA((2,)),
                        pltpu.SemaphoreType.DMA((2,)),
                        pltpu.VMEM((D_MODEL, 2 * D_EXPERT), F32),
                        pltpu.VMEM((D_EXPERT, D_MODEL), F32),
                        pltpu.SemaphoreType.DMA((2,)),
                        pltpu.VMEM((D_MODEL, 2 * D_EXPERT), BF16),
                        pltpu.VMEM((D_EXPERT, D_MODEL), BF16)],
    )
    return pl.pallas_call(
        _expert_kernel,
        grid_spec=grid_spec,
        out_shape=jax.ShapeDtypeStruct(xg.shape, xg.dtype),
        input_output_aliases={6: 0},
        compiler_params=_cparams(1),
        name="experts",
    )(blk_e, nxt_e, n_used, src3, src3, dst3, xg, w1, b1.reshape(N_EXPERTS, 1, -1), w2,
      b2.reshape(N_EXPERTS, 1, -1))


def _combine_kernel(pos_ref, gate_ref, x1_ref, gf_ref, yt_ref, y_ref):
    rows = pos_ref.shape[0]
    slot_iota = lax.broadcasted_iota(jnp.int32, (rows, TILE_SLOTS), 1)
    pos = pos_ref[...]
    gates = gate_ref[...]
    weights = jnp.zeros((rows, TILE_SLOTS), F32)
    for kk in range(TOP_K):
        weights = jnp.where(slot_iota == pos[:, kk:kk + 1], gates[:, kk:kk + 1], weights)
    moe = _dot(weights.astype(BF16), pltpu.bitcast(yt_ref[...], BF16))
    y_ref[...] = _rms(x1_ref[...] + moe, gf_ref[...])


def _combine(pos, gates, x1, g_final, yt, *, tile0, n_tiles):
    row = lambda i: (tile0 + i, 0)
    return pl.pallas_call(
        _combine_kernel,
        grid=(n_tiles,),
        in_specs=[pl.BlockSpec((ROW_TILE, TOP_K), row),
                  pl.BlockSpec((ROW_TILE, TOP_K), row),
                  pl.BlockSpec((ROW_TILE, D_MODEL), row),
                  pl.BlockSpec((1, D_MODEL), lambda i: (0, 0)),
                  pl.BlockSpec((TILE_SLOTS // PACK, D_MODEL), row)],
        out_specs=pl.BlockSpec((ROW_TILE, D_MODEL), lambda i: (i, 0)),
        out_shape=jax.ShapeDtypeStruct((n_tiles * ROW_TILE, D_MODEL), F32),
        compiler_params=_cparams(1),
        name="combine",
    )(pos, gates, x1, g_final.reshape(1, D_MODEL), yt)


def _granule_plan(gran, goff):
    n_tiles = gran.shape[0]
    max_gran = (n_tiles * ROW_TILE * TOP_K + (GRANULE - 1) * n_tiles * N_EXPERTS) // GRANULE
    n_blocks = max_gran // GRAN_PER_BLOCK + N_EXPERTS + 1
    n_slots = n_blocks * GRAN_PER_BLOCK
    i32 = jnp.int32
    per_e = jnp.sum(gran, axis=0)
    region = (per_e + GRAN_PER_BLOCK - 1) // GRAN_PER_BLOCK * GRAN_PER_BLOCK
    region_end = jnp.cumsum(region)
    region_start = region_end - region
    g_t = gran.T
    run_start = (region_start[:, None] + jnp.cumsum(g_t, axis=1) - g_t).reshape(-1)
    run_end = run_start + g_t.reshape(-1)
    first_gran = (jnp.arange(n_tiles, dtype=i32)[None, :] * GRAN_PER_TILE + goff.T).reshape(-1)
    shift = first_gran - run_start
    d_shift = jnp.diff(shift, prepend=0)
    d_end = jnp.diff(run_end, prepend=0)
    slot = jnp.arange(n_slots, dtype=i32)
    started = slot[:, None] >= run_start[None, :]
    granule = slot + jnp.sum(jnp.where(started, d_shift[None, :], 0), axis=1)
    valid = slot < jnp.sum(jnp.where(started, d_end[None, :], 0), axis=1)
    spare = n_tiles * GRAN_PER_TILE
    j = slot % GRAN_PER_BLOCK
    odd_blk = (slot // GRAN_PER_BLOCK) % 2
    n_zero = GRAN_PER_TILE - 2 * GRAN_PER_BLOCK
    assert n_zero > 0
    granule_src = jnp.where(valid, granule, spare + 2 * GRAN_PER_BLOCK + j % n_zero).astype(i32)
    granule_dst = jnp.where(valid, granule, spare + odd_blk * GRAN_PER_BLOCK + j).astype(i32)
    blk = jnp.arange(n_blocks, dtype=i32) * GRAN_PER_BLOCK
    blk_e = jnp.minimum(jnp.sum(blk[:, None] >= region_end[None, :], axis=1), N_EXPERTS - 1).astype(i32)
    e = jnp.arange(N_EXPERTS, dtype=i32)
    later = jnp.where((e[None, :] > e[:, None]) & (per_e[None, :] > 0), e[None, :], N_EXPERTS)
    nxt = jnp.min(later, axis=1)
    nxt = jnp.where(nxt == N_EXPERTS, -1, nxt)
    nxt_e = jnp.sum(jnp.where(blk_e[:, None] == e[None, :], nxt[None, :], 0), axis=1).astype(i32)
    n_used = (region_end[-1:] // GRAN_PER_BLOCK).astype(i32)
    return granule_src, granule_dst, blk_e, nxt_e, n_used


def kernel(x_prompt, x_sample, state_s5_re, state_s5_im, state_hgrn, g_mix, w_in, s5_lambda_re,
           s5_lambda_im, s5_log_dt, s5_b_re, s5_b_im, s5_c_re, s5_c_im, s5_d, s5_w_glu, s5_w_gate,
           hgrn_lower_bound, hgrn_norm_g, hgrn_w_out, w_out, g_ffn, router_w, router_b, moe_w1,
           moe_b1, moe_w2, moe_b2, g_final):
    depth = g_mix.shape[0]
    n_p, l_p, _ = x_prompt.shape
    n_s, l_s, _ = x_sample.shape
    assert depth == 1 and n_p == S5_SEQS and l_p % ROW_TILE == 0
    assert l_s == S5_STEPS and n_s % S5_SEQS == 0 and ROW_TILE % l_s == 0
    li = 0
    t_p, t_s = n_p * l_p, n_s * l_s
    n_groups = n_s // S5_SEQS

    lb = jnp.cumsum(jax.nn.softmax(hgrn_lower_bound.astype(F32), axis=0), axis=0)[li].reshape(1, -1)
    s5_tabs = _s5_tables(s5_lambda_re[li], s5_lambda_im[li], s5_log_dt[li], s5_b_re[li],
                         s5_b_im[li], s5_c_re[li], s5_c_im[li])
    wg = jnp.concatenate([s5_w_glu[li], s5_w_gate[li]], axis=1).astype(BF16)
    rw_pad = jnp.pad(router_w[li].astype(F32), ((0, 0), (0, ROUTER_LANES - N_EXPERTS)))
    rw_hi = rw_pad.astype(BF16)
    rw2 = jnp.stack([rw_hi, (rw_pad - rw_hi.astype(F32)).astype(BF16)])

    xp = x_prompt.reshape(t_p, D_MODEL)
    xs = x_sample.reshape(t_s, D_MODEL)
    u, zh, gab = _inproj(xp, xs, g_mix[li], w_in[li].astype(BF16))

    s0_s5 = jnp.concatenate([state_s5_re[li].reshape(n_groups, S5_SEQS, S5_LANES),
                             state_s5_im[li].reshape(n_groups, S5_SEQS, S5_LANES)], axis=1).astype(F32)
    ya, s5_p, s5_s = _s5(u, s0_s5, s5_tabs, s5_d[li].reshape(1, -1), wg,
                         prompt_len=l_p, n_sample_groups=n_groups)
    yb, hg_p, hg_s = _hgrn(zh, lb, hgrn_norm_g[li].reshape(1, -1), hgrn_w_out[li].astype(BF16),
                           state_hgrn[li].astype(F32), n_prompt=n_p, prompt_len=l_p, sample_len=l_s)

    x1, xg, pos, gates, gran, goff = _merge(
        gab, ya, yb, xp, xs, w_out[li].astype(BF16), g_ffn[li].reshape(1, -1), rw2,
        router_b[li].reshape(-1, 1).astype(F32), prompt_len=l_p)

    plan = _granule_plan(gran.reshape(-1, N_EXPERTS), goff.reshape(-1, N_EXPERTS))
    yt = _experts(xg, plan, moe_w1[li], moe_b1[li], moe_w2[li], moe_b2[li])
    n_pt = t_p // ROW_TILE
    y_p = _combine(pos, gates, x1, g_final, yt, tile0=0, n_tiles=n_pt)
    y_s = _combine(pos, gates, x1, g_final, yt, tile0=n_pt, n_tiles=t_s // ROW_TILE)

    def s5_out(st, n, dtype):
        re = st[:, :S5_SEQS].reshape(1, n, S5_GROUPS, S5_STATE).astype(dtype)
        im = st[:, S5_SEQS:].reshape(1, n, S5_GROUPS, S5_STATE).astype(dtype)
        return re, im

    dt = x_prompt.dtype
    p_re, p_im = s5_out(s5_p, n_p, dt)
    s_re, _ = s5_out(s5_s, n_s, state_s5_re.dtype)
    _, s_im = s5_out(s5_s, n_s, state_s5_im.dtype)
    return (y_p.reshape(n_p, l_p, D_MODEL), y_s.reshape(n_s, l_s, D_MODEL),
            p_re, p_im, hg_p[None].astype(dt),
            s_re, s_im, hg_s[None].astype(state_hgrn.dtype))
```

```python
import functools
import math

import jax
import jax.numpy as jnp
from jax import lax
from jax.experimental import pallas as pl
from jax.experimental.pallas import tpu as pltpu

F32 = jnp.float32
BF16 = jnp.bfloat16

D_MODEL = 1024
NORM_EPS = 1e-6
S5_WIDTH = 512
S5_GROUP = 16
S5_GROUPS = 32
S5_STATE = 64
S5_LANES = S5_GROUPS * S5_STATE
S5_DT_CLIP = -1e-4
HG_WIDTH = 512
HG_HEADS = 4
HG_DK = 128
HG_DIAG = 128
N_EXPERTS = 32
TOP_K = 4
ROUTER_LANES = 128
D_EXPERT = 1024
SWIGLU_ALPHA = 1.702
SWIGLU_LIMIT = 7.0

ROW_TILE = 256
SCAN_ROWS = 8
S5_SEQS = 4
S5_STEPS = 64
S5_CH_TILE = 128
S5_LANE_TILE = 512
EXPERT_ROWS = 256
GRANULE = 8
TILE_SLOTS = ROW_TILE * TOP_K + N_EXPERTS * GRANULE
GRAN_PER_TILE = TILE_SLOTS // GRANULE
GRAN_PER_BLOCK = EXPERT_ROWS // GRANULE
PACK = 2
U32 = jnp.uint32
VMEM_LIMIT = 56 * 1024 * 1024


def _cparams(n_axes):
    return pltpu.CompilerParams(dimension_semantics=("arbitrary",) * n_axes,
                                vmem_limit_bytes=VMEM_LIMIT)


def _rms(x, g):
    r = lax.rsqrt(jnp.mean(x * x, axis=-1, keepdims=True) + NORM_EPS)
    return x * r * g


def _dot(a, b):
    return jnp.dot(a, b, preferred_element_type=F32)


def _dot_nt(a, b):
    return lax.dot_general(a, b, (((1,), (1,)), ((), ())), preferred_element_type=F32)


def _dot_tn(a, b):
    return lax.dot_general(a, b, (((0,), (0,)), ((), ())), preferred_element_type=F32)


def _two_stream_specs(n_first):
    return (pl.BlockSpec((ROW_TILE, D_MODEL), lambda i: (jnp.minimum(i, n_first - 1), 0)),
            pl.BlockSpec((ROW_TILE, D_MODEL), lambda i: (jnp.maximum(i - n_first, 0), 0)))


def _inproj_kernel(xp_ref, xs_ref, g_ref, w_ref, u_ref, zh_ref, gab_ref, *, n_first):
    x = jnp.where(pl.program_id(0) < n_first, xp_ref[...], xs_ref[...])
    h = _rms(x, g_ref[...]).astype(BF16)
    u_ref[...] = _dot(h, w_ref[:, :S5_WIDTH])
    zh_ref[...] = _dot(h, w_ref[:, S5_WIDTH:S5_WIDTH + 4 * HG_WIDTH])
    gab_ref[...] = _dot(h, w_ref[:, S5_WIDTH + 4 * HG_WIDTH:])


def _inproj(xp, xs, g_mix, w_in_bf):
    n_first = xp.shape[0] // ROW_TILE
    t = xp.shape[0] + xs.shape[0]
    ncols = w_in_bf.shape[1]
    row = lambda i: (i, 0)
    fixed = lambda i: (0, 0)
    return pl.pallas_call(
        functools.partial(_inproj_kernel, n_first=n_first),
        grid=(t // ROW_TILE,),
        in_specs=[*_two_stream_specs(n_first),
                  pl.BlockSpec((1, D_MODEL), fixed),
                  pl.BlockSpec((D_MODEL, ncols), fixed)],
        out_specs=[pl.BlockSpec((ROW_TILE, S5_WIDTH), row),
                   pl.BlockSpec((ROW_TILE, 4 * HG_WIDTH), row),
                   pl.BlockSpec((ROW_TILE, 2 * D_MODEL), row)],
        out_shape=[jax.ShapeDtypeStruct((t, S5_WIDTH), F32),
                   jax.ShapeDtypeStruct((t, 4 * HG_WIDTH), F32),
                   jax.ShapeDtypeStruct((t, 2 * D_MODEL), F32)],
        compiler_params=_cparams(1),
        name="inproj",
    )(xp, xs, g_mix.reshape(1, D_MODEL), w_in_bf)


def _s5_kernel(u0_ref, u1_ref, u2_ref, u3_ref, s0_ref, perm_ref, pinv_ref, wb_ref, wc_ref,
               coef_ref, d_ref, wg_ref, ya_ref, sp_out, ss_out, buf0, buf1, ub0, ub1, v_ref,
               *, n_prompt_steps, n_steps):
    i = pl.program_id(0)
    ia = jnp.minimum(i, n_steps - 1)

    @pl.when(i == 0)
    def _():
        v_ref[...] = jnp.zeros_like(v_ref)
        buf1[...] = jnp.zeros_like(buf1)
        ub1[...] = jnp.zeros_like(ub1)

    @pl.when(ia >= n_prompt_steps)
    def _():
        v_ref[...] = s0_ref[0]

    u_refs = (u0_ref, u1_ref, u2_ref, u3_ref)
    consts = (perm_ref, pinv_ref, wb_ref, wc_ref, coef_ref, d_ref, wg_ref)

    @pl.when(i % 2 == 0)
    def _():
        _s5_body(u_refs, consts, ya_ref, v_ref, cur=(buf0, ub0), prev=(buf1, ub1))

    @pl.when(i % 2 == 1)
    def _():
        _s5_body(u_refs, consts, ya_ref, v_ref, cur=(buf1, ub1), prev=(buf0, ub0))

    @pl.when(ia < n_prompt_steps)
    def _():
        sp_out[0] = v_ref[...]

    @pl.when(ia >= n_prompt_steps)
    def _():
        ss_out[0] = v_ref[...]


def _s5_body(u_refs, consts, ya_ref, v_ref, *, cur, prev):
    perm_ref, pinv_ref, wb_ref, wc_ref, coef_ref, d_ref, wg_ref = consts
    buf_a, ub_a = cur
    buf_b, ub_b = prev
    n_tiles = S5_WIDTH // S5_CH_TILE

    u = jnp.concatenate([r[0] for r in u_refs], axis=0)
    ub_a[...] = u
    rows = 2 * u.shape[0]
    up = _dot(perm_ref[...], u.astype(BF16))
    re_half = (lax.broadcasted_iota(jnp.int32, (rows, S5_CH_TILE), 0) & S5_SEQS) == 0
    ws = []
    for n in range(n_tiles):
        ls = slice(n * S5_LANE_TILE, (n + 1) * S5_LANE_TILE)
        un = up[:, n * S5_CH_TILE:(n + 1) * S5_CH_TILE]
        lhs = jnp.concatenate([jnp.where(re_half, un, 0.0), jnp.where(re_half, 0.0, un)], axis=1)
        buf_a[:, ls] = _dot(lhs.astype(BF16), wb_ref[n])
        z = _dot(buf_b[:, ls].astype(BF16), wc_ref[n])
        ws.append(jnp.where(re_half, z[:, :S5_CH_TILE], z[:, S5_CH_TILE:]))
    w_hi, w_lo = _split2(jnp.concatenate(ws, axis=1))

    def scan(v, t0, t1):
        for t in range(t0, t1):
            rs = slice(t * SCAN_ROWS, (t + 1) * SCAN_ROWS)
            v = coef_ref[0] * v + coef_ref[1] * pltpu.roll(v, S5_SEQS, axis=0) + buf_a[rs, :]
            buf_a[rs, :] = v
        return v

    quarter = S5_STEPS // 4
    v = scan(v_ref[...], 0, quarter)
    y = _dot(pinv_ref[...], w_hi) + _dot(pinv_ref[...], w_lo) + d_ref[...] * ub_b[...]
    y = jax.nn.gelu(y).astype(BF16)
    v = scan(v, quarter, 2 * quarter)
    p_glu = _dot(y, wg_ref[:, :D_MODEL])
    v = scan(v, 2 * quarter, 3 * quarter)
    p_gate = _dot(y, wg_ref[:, D_MODEL:])
    v = scan(v, 3 * quarter, S5_STEPS)
    ya_ref[0] = (p_glu * jax.nn.sigmoid(p_gate)).reshape(S5_SEQS, S5_STEPS, D_MODEL)
    v_ref[...] = v


def _s5(u, s0_sample, tables, d_skip, wg, *, prompt_len, n_sample_groups):
    perm, pinv, wb, wc, coef = tables
    n_prompt_steps = prompt_len // S5_STEPS
    n_steps = n_prompt_steps + n_sample_groups
    sample_base = S5_SEQS * n_prompt_steps
    u3 = u.reshape(-1, S5_STEPS, S5_WIDTH)

    def u_spec(s):
        def imap(i):
            i = jnp.minimum(i, n_steps - 1)
            prompt_blk = s * n_prompt_steps + i
            sample_blk = sample_base + S5_SEQS * (i - n_prompt_steps) + s
            return (jnp.where(i < n_prompt_steps, prompt_blk, sample_blk), 0, 0)
        return pl.BlockSpec((1, S5_STEPS, S5_WIDTH), imap)

    sample_grp = lambda i: (jnp.clip(i - n_prompt_steps, 0, n_sample_groups - 1), 0, 0)
    whole = lambda a: pl.BlockSpec(a.shape, lambda i: (0,) * a.ndim)
    state_blk = (1, 2 * S5_SEQS, S5_LANES)
    rows = S5_SEQS * S5_STEPS
    return pl.pallas_call(
        functools.partial(_s5_kernel, n_prompt_steps=n_prompt_steps, n_steps=n_steps),
        grid=(n_steps + 1,),
        in_specs=[u_spec(0), u_spec(1), u_spec(2), u_spec(3),
                  pl.BlockSpec(state_blk, sample_grp),
                  whole(perm), whole(pinv), whole(wb), whole(wc), whole(coef), whole(d_skip), whole(wg)],
        out_specs=[pl.BlockSpec((1, S5_SEQS, S5_STEPS, D_MODEL),
                                lambda i: (jnp.maximum(i - 1, 0), 0, 0, 0)),
                   pl.BlockSpec(state_blk, lambda i: (0, 0, 0)),
                   pl.BlockSpec(state_blk, sample_grp)],
        out_shape=[jax.ShapeDtypeStruct((n_steps, S5_SEQS, S5_STEPS, D_MODEL), F32),
                   jax.ShapeDtypeStruct(state_blk, F32),
                   jax.ShapeDtypeStruct((n_sample_groups,) + state_blk[1:], F32)],
        scratch_shapes=[pltpu.VMEM((2 * rows, S5_LANES), F32),
                        pltpu.VMEM((2 * rows, S5_LANES), F32),
                        pltpu.VMEM((rows, S5_WIDTH), F32),
                        pltpu.VMEM((rows, S5_WIDTH), F32),
                        pltpu.VMEM((2 * S5_SEQS, S5_LANES), F32)],
        compiler_params=_cparams(1),
        name="s5",
    )(u3, u3, u3, u3, s0_sample, perm, pinv, wb, wc, coef, d_skip, wg)


def _s5_tables(lam_re, lam_im, log_dt, b_re, b_im, c_re, c_im):
    lr = jnp.minimum(lam_re.astype(F32), S5_DT_CLIP)
    li = lam_im.astype(F32)
    dt = jnp.exp(log_dt.astype(F32))[:, None]
    mag = jnp.exp(lr * dt)
    ar = mag * jnp.cos(li * dt)
    ai = mag * jnp.sin(li * dt)
    den = lr * lr + li * li
    zr = ((ar - 1.0) * lr + ai * li) / den
    zi = (ai * lr - (ar - 1.0) * li) / den
    br = b_re.astype(F32)
    bi = b_im.astype(F32)
    bbr = zr[..., None] * br - zi[..., None] * bi
    bbi = zr[..., None] * bi + zi[..., None] * br

    ar_f, ai_f = ar.reshape(1, -1), ai.reshape(1, -1)
    sign = jnp.where(jnp.arange(2 * S5_SEQS)[:, None] < S5_SEQS, -1.0, 1.0)
    coef = jnp.stack([jnp.broadcast_to(ar_f, (2 * S5_SEQS, S5_LANES)), sign * ai_f]).astype(F32)

    g_per_tile = S5_CH_TILE // S5_GROUP
    n_tiles = S5_GROUPS // g_per_tile
    eye = jnp.eye(g_per_tile, dtype=F32)

    def b_tile(bb):
        x = bb.reshape(n_tiles, g_per_tile, S5_STATE, S5_GROUP)
        x = x[:, :, :, :, None] * eye[None, :, None, None, :]
        return x.transpose(0, 1, 3, 4, 2).reshape(n_tiles, S5_CH_TILE, S5_LANE_TILE)

    wb = jnp.concatenate([b_tile(bbr), b_tile(bbi)], axis=1).astype(BF16)

    def c_tile(cc):
        x = cc.astype(F32).reshape(n_tiles, g_per_tile, S5_GROUP, S5_STATE)
        x = x[:, :, :, :, None] * eye[None, :, None, None, :]
        return x.transpose(0, 1, 3, 4, 2).reshape(n_tiles, S5_LANE_TILE, S5_CH_TILE)

    wc = jnp.concatenate([c_tile(c_re), -c_tile(c_im)], axis=2).astype(BF16)

    r = jnp.arange(2 * S5_SEQS * S5_STEPS)
    src = (r % S5_SEQS) * S5_STEPS + r // (2 * S5_SEQS)
    perm = (src[:, None] == jnp.arange(S5_SEQS * S5_STEPS)[None, :]).astype(BF16)
    return perm, perm.T, wb, wc, coef


def _split2(x):
    hi = x.astype(BF16)
    lo = (x - hi.astype(F32)).astype(BF16)
    return hi, lo


def _split3(x):
    hi = x.astype(BF16)
    r = x - hi.astype(F32)
    mid = r.astype(BF16)
    lo = (r - mid.astype(F32)).astype(BF16)
    return hi, mid, lo


def _hgrn_kernel(zh_ref, lb_ref, gn_ref, wo_ref, s0_ref, yb_ref, sp_out, ss_out,
                 st, gc_ref, att_ref, cross_ref, o_ref, *, n_prompt_tiles, chunks_per_seq,
                 sample_len):
    i = pl.program_id(0)
    refs = (zh_ref, lb_ref, gn_ref, wo_ref, yb_ref, st, gc_ref, att_ref, cross_ref, o_ref)

    @pl.when(i < n_prompt_tiles)
    def _():
        j = i % chunks_per_seq

        @pl.when(j == 0)
        def _():
            st[0] = jnp.zeros((HG_HEADS, HG_DK, HG_DK), F32)

        _hgrn_tile(*refs, n_seg=1, seg_len=ROW_TILE)

        @pl.when(j == chunks_per_seq - 1)
        def _():
            for h in range(HG_HEADS):
                sp_out[0, h] = st[0, h].T

    @pl.when(i >= n_prompt_tiles)
    def _():
        n_seg = ROW_TILE // sample_len
        for seg in range(n_seg):
            for h in range(HG_HEADS):
                st[seg, h] = s0_ref[seg, h].T
        _hgrn_tile(*refs, n_seg=n_seg, seg_len=sample_len)
        for seg in range(n_seg):
            for h in range(HG_HEADS):
                ss_out[seg, h] = st[seg, h].T


def _hgrn_tile(zh_ref, lb_ref, gn_ref, wo_ref, yb_ref, st, gc_ref, att_ref, cross_ref, o_ref,
               *, n_seg, seg_len):
    rows = n_seg * seg_len
    zq = zh_ref[:, 0:HG_WIDTH]
    zf = zh_ref[:, HG_WIDTH:2 * HG_WIDTH]
    v = zh_ref[:, 2 * HG_WIDTH:3 * HG_WIDTH]
    lb = lb_ref[...]
    q = zq * jax.nn.sigmoid(zq)
    f = lb + (1.0 - lb) * jax.nn.sigmoid(zf)
    k = 1.0 - f
    g = jnp.log(f)
    vb = v.astype(BF16)

    ri = lax.broadcasted_iota(jnp.int32, (rows, rows), 0)
    ci = lax.broadcasted_iota(jnp.int32, (rows, rows), 1)
    seg_shift = int(math.log2(seg_len))
    same_seg = (ri >> seg_shift) == (ci >> seg_shift)

    tri = jnp.where(same_seg & (ci <= ri), 1.0, 0.0).astype(BF16)
    g3 = jnp.concatenate(_split3(g), axis=1)
    gcs = _dot(tri, g3)
    gc = gcs[:, :HG_WIDTH] + gcs[:, HG_WIDTH:2 * HG_WIDTH] + gcs[:, 2 * HG_WIDTH:]
    gc_ref[...] = gc

    n_diag = rows // HG_DIAG
    dr = lax.broadcasted_iota(jnp.int32, (HG_DIAG, HG_DIAG), 0)
    dc = lax.broadcasted_iota(jnp.int32, (HG_DIAG, HG_DIAG), 1)
    dxor = jnp.where(dc < dr, dr ^ dc, 0)
    row_id = lax.broadcasted_iota(jnp.int32, (rows, HG_WIDTH), 0)
    qb = q.astype(BF16)
    kb = k.astype(BF16)
    for h in range(HG_HEADS):
        hs = slice(h * HG_DK, (h + 1) * HG_DK)
        for d in range(n_diag):
            ds_ = slice(d * HG_DIAG, (d + 1) * HG_DIAG)
            att_ref[h, d] = jnp.where(dr == dc, _dot_nt(qb[ds_, hs], kb[ds_, hs]), 0.0)

    g2 = jnp.concatenate(_split2(g), axis=1)
    level = seg_len // 2
    while level >= 1:
        if 2 * level >= SCAN_ROWS:
            mids, picks = [], []
            for b in range(rows // (2 * level)):
                lo = b * 2 * level
                mids.append(jnp.broadcast_to(gc_ref[lo + level - 1:lo + level, :], (2 * level, HG_WIDTH)))
                picks += [k[lo:lo + level], q[lo + level:lo + 2 * level]]
            gmid = mids[0] if len(mids) == 1 else jnp.concatenate(mids, axis=0)
            qk = jnp.concatenate(picks, axis=0)
            dexp = -jnp.abs(gc - gmid)
        else:
            mid = (ri & ~(2 * level - 1)) + (level - 1)
            upper = (ri & level) != 0
            w = (ci > jnp.where(upper, mid, ri)) & (ci <= jnp.where(upper, ri, mid))
            dsum = _dot(jnp.where(w, 1.0, 0.0).astype(BF16), g2)
            dexp = dsum[:, :HG_WIDTH] + dsum[:, HG_WIDTH:]
            qk = jnp.where((row_id & level) != 0, q, k)
        m = (qk * jnp.exp(dexp)).astype(BF16)
        if level >= HG_DIAG:
            for h in range(HG_HEADS):
                hs = slice(h * HG_DK, (h + 1) * HG_DK)
                cross_ref[h] = _dot_nt(m[level:, hs], m[:level, hs])
        else:
            sel = (dxor >> int(math.log2(level))) == 1
            for h in range(HG_HEADS):
                hs = slice(h * HG_DK, (h + 1) * HG_DK)
                for d in range(n_diag):
                    ds_ = slice(d * HG_DIAG, (d + 1) * HG_DIAG)
                    att_ref[h, d] = jnp.where(sel, _dot_nt(m[ds_, hs], m[ds_, hs]), att_ref[h, d])
        level //= 2

    qg = (q * jnp.exp(gc)).astype(BF16)
    for h in range(HG_HEADS):
        hs = slice(h * HG_DK, (h + 1) * HG_DK)
        for d in range(n_diag):
            ds_ = slice(d * HG_DIAG, (d + 1) * HG_DIAG)
            o_ref[ds_, hs] = _dot(att_ref[h, d].astype(BF16), vb[ds_, hs])
        if seg_len > HG_DIAG:
            o_ref[HG_DIAG:, hs] += _dot(cross_ref[h].astype(BF16), vb[:HG_DIAG, hs])

    for seg in range(n_seg):
        rs = slice(seg * seg_len, (seg + 1) * seg_len)
        last = (seg + 1) * seg_len - 1
        g_last = gc_ref[last:last + 1, :]
        kd = (k[rs] * jnp.exp(g_last - gc[rs])).astype(BF16)
        decay = jnp.exp(g_last)
        for h in range(HG_HEADS):
            hs = slice(h * HG_DK, (h + 1) * HG_DK)
            s_t = st[seg, h]
            o_ref[rs, hs] += _dot_nt(qg[rs, hs], s_t.astype(BF16))
            st[seg, h] = s_t * decay[:, hs] + _dot_tn(vb[rs, hs], kd[:, hs])

    zg = zh_ref[:, 3 * HG_WIDTH:]
    gate = zg * jax.nn.sigmoid(zg)
    outs = []
    for h in range(HG_HEADS):
        hs = slice(h * HG_DK, (h + 1) * HG_DK)
        outs.append(_rms(o_ref[:, hs], gn_ref[...]) * gate[:, hs])
    yb_ref[...] = _dot(jnp.concatenate(outs, axis=1).astype(BF16), wo_ref[...])


def _hgrn(zh, lb, gn, wo, s0_sample, *, n_prompt, prompt_len, sample_len):
    t = zh.shape[0]
    chunks_per_seq = prompt_len // ROW_TILE
    n_prompt_tiles = n_prompt * chunks_per_seq
    n_seg = ROW_TILE // sample_len
    n_sample = s0_sample.shape[0]
    row = lambda i: (i, 0)
    fixed = lambda i: (0, 0)
    state = (HG_HEADS, HG_DK, HG_DK)
    prompt_seq = lambda i: (jnp.minimum(i // chunks_per_seq, n_prompt - 1), 0, 0, 0)
    sample_grp = lambda i: (jnp.maximum(i - n_prompt_tiles, 0), 0, 0, 0)
    kern = functools.partial(_hgrn_kernel, n_prompt_tiles=n_prompt_tiles,
                             chunks_per_seq=chunks_per_seq, sample_len=sample_len)
    return pl.pallas_call(
        kern,
        grid=(t // ROW_TILE,),
        in_specs=[pl.BlockSpec((ROW_TILE, 4 * HG_WIDTH), row),
                  pl.BlockSpec((1, HG_WIDTH), fixed),
                  pl.BlockSpec((1, HG_DK), fixed),
                  pl.BlockSpec(wo.shape, fixed),
                  pl.BlockSpec((n_seg,) + state, sample_grp)],
        out_specs=[pl.BlockSpec((ROW_TILE, D_MODEL), row),
                   pl.BlockSpec((1,) + state, prompt_seq),
                   pl.BlockSpec((n_seg,) + state, sample_grp)],
        out_shape=[jax.ShapeDtypeStruct((t, D_MODEL), F32),
                   jax.ShapeDtypeStruct((n_prompt,) + state, F32),
                   jax.ShapeDtypeStruct((n_sample,) + state, F32)],
        scratch_shapes=[pltpu.VMEM((n_seg,) + state, F32),
                        pltpu.VMEM((ROW_TILE, HG_WIDTH), F32),
                        pltpu.VMEM((HG_HEADS, ROW_TILE // HG_DIAG, HG_DIAG, HG_DIAG), F32),
                        pltpu.VMEM((HG_HEADS, HG_DIAG, HG_DIAG), F32),
                        pltpu.VMEM((ROW_TILE, HG_WIDTH), F32)],
        compiler_params=_cparams(1),
        name="hgrn2",
    )(zh, lb, gn, wo, s0_sample)


def _merge_kernel(gab_ref, ya0_ref, ya1_ref, ya2_ref, ya3_ref, yb_ref, xp_ref, xs_ref, wo_ref,
                  gf_ref, rw_ref, rb_ref, x1_ref, xg_ref, pos_ref, gate_ref, cnt_ref, off_ref,
                  hbuf, lbuf, *, n_first, n_tiles):
    i = pl.program_id(0)
    cur = i % 2
    prev = 1 - cur

    @pl.when(i == 0)
    def _():
        hbuf[1] = jnp.zeros(hbuf.shape[1:], hbuf.dtype)
        lbuf[1] = jnp.zeros(lbuf.shape[1:], lbuf.dtype)

    logits_tok = lbuf[prev]

    ga = gab_ref[:, :D_MODEL]
    gb = gab_ref[:, D_MODEL:]
    ya = jnp.concatenate([r[0] for r in (ya0_ref, ya1_ref, ya2_ref, ya3_ref)], axis=0)
    mixed = (jax.nn.sigmoid(ga) * ya + jax.nn.sigmoid(gb) * yb_ref[...]).astype(BF16)

    rows = logits_tok.shape[0]
    logits = logits_tok.T[:N_EXPERTS] + rb_ref[...]
    eid = lax.broadcasted_iota(jnp.int32, (N_EXPERTS, rows), 0).astype(F32)
    lg = logits
    sel = jnp.zeros((N_EXPERTS, rows), F32)
    vals, idxs = [], []
    for _ in range(TOP_K):
        m = jnp.max(lg, axis=0, keepdims=True)
        idx = jnp.min(jnp.where(lg == m, eid, float(N_EXPERTS)), axis=0, keepdims=True)
        hit = eid == idx
        vals.append(m)
        idxs.append(idx)
        lg = jnp.where(hit, -jnp.inf, lg)
        sel = jnp.where(hit, 1.0, sel)

    es = [jnp.exp(vv - vals[0]) for vv in vals]
    tot = es[0] + es[1] + es[2] + es[3]

    ri = lax.broadcasted_iota(jnp.int32, (rows, rows), 0)
    ci = lax.broadcasted_iota(jnp.int32, (rows, rows), 1)
    earlier = jnp.where(ri < ci, 1.0, 0.0).astype(BF16)
    rank = _dot(sel.astype(BF16), earlier)
    gran = jnp.floor((jnp.sum(sel, axis=1, keepdims=True) + (GRANULE - 1)) * (1.0 / GRANULE))
    ei = lax.broadcasted_iota(jnp.int32, (N_EXPERTS, N_EXPERTS), 0)
    ej = lax.broadcasted_iota(jnp.int32, (N_EXPERTS, N_EXPERTS), 1)
    gran_b = jnp.broadcast_to(gran, (N_EXPERTS, rows))
    off = _dot(jnp.where(ej < ei, 1.0, 0.0).astype(BF16), gran_b.astype(BF16))
    slot_of = rank + off * float(GRANULE)
    poss = [jnp.sum(jnp.where(eid == idx, slot_of, 0.0), axis=0, keepdims=True) for idx in idxs]

    per_tok = jnp.concatenate(poss + [e / tot for e in es], axis=0).T

    x = jnp.where(jnp.minimum(i, n_tiles - 1) < n_first, xp_ref[...], xs_ref[...])
    x1 = x + _dot(mixed, wo_ref[...])
    x1_ref[...] = x1

    slot_iota = lax.broadcasted_iota(jnp.int32, (rows, TILE_SLOTS), 1).astype(F32)
    onehot = jnp.zeros((rows, TILE_SLOTS), F32)
    for kk in range(TOP_K):
        onehot = jnp.where(slot_iota == per_tok[:, kk:kk + 1], 1.0, onehot)
    sorted_rows = _dot_tn(onehot.astype(BF16), hbuf[prev])
    xg_ref[...] = pltpu.bitcast(jnp.where(i > n_tiles, 0.0, sorted_rows).astype(BF16), U32)

    a_hi, a_lo = _split2(_rms(x1, gf_ref[...]))
    hbuf[cur] = a_hi
    lbuf[cur] = _dot(a_hi, rw_ref[0]) + _dot(a_hi, rw_ref[1]) + _dot(a_lo, rw_ref[0])

    pos_ref[...] = per_tok[:, :TOP_K].astype(jnp.int32)
    gate_ref[...] = per_tok[:, TOP_K:]
    cnt_ref[0] = gran.astype(jnp.int32)
    off_ref[0] = off[:, :1].astype(jnp.int32)


def _merge(gab, ya_steps, yb, xp, xs, wo, g_ffn, rw2, rb, *, prompt_len):
    n_first = xp.shape[0] // ROW_TILE
    t = xp.shape[0] + xs.shape[0]
    n_tiles = t // ROW_TILE
    chunks_per_seq = prompt_len // ROW_TILE
    quarters = ROW_TILE // S5_STEPS
    ya3 = ya_steps.reshape(-1, S5_STEPS, D_MODEL)

    def ya_spec(q):
        def imap(i):
            i = jnp.minimum(i, n_tiles - 1)
            seq, chunk = i // chunks_per_seq, i % chunks_per_seq
            prompt_blk = (chunk * quarters + q) * S5_SEQS + seq
            return (jnp.where(i < n_first, prompt_blk, i * quarters + q), 0, 0)
        return pl.BlockSpec((1, S5_STEPS, D_MODEL), imap)

    tile = lambda i: jnp.minimum(i, n_tiles - 1)
    prev_tile = lambda i: jnp.clip(i - 1, 0, n_tiles - 1)
    row = lambda i: (tile(i), 0)
    prev_row = lambda i: (prev_tile(i), 0)
    fixed = lambda i: (0, 0)
    per_prev_tile = lambda i: (prev_tile(i), 0, 0)
    xp_spec, xs_spec = _two_stream_specs(n_first)
    return pl.pallas_call(
        functools.partial(_merge_kernel, n_first=n_first, n_tiles=n_tiles),
        grid=(n_tiles + 2,),
        in_specs=[pl.BlockSpec((ROW_TILE, 2 * D_MODEL), row),
                  ya_spec(0), ya_spec(1), ya_spec(2), ya_spec(3),
                  pl.BlockSpec((ROW_TILE, D_MODEL), row),
                  pl.BlockSpec(xp_spec.block_shape, lambda i: xp_spec.index_map(tile(i))),
                  pl.BlockSpec(xs_spec.block_shape, lambda i: xs_spec.index_map(tile(i))),
                  pl.BlockSpec((D_MODEL, D_MODEL), fixed),
                  pl.BlockSpec((1, D_MODEL), fixed),
                  pl.BlockSpec((2, D_MODEL, ROUTER_LANES), lambda i: (0, 0, 0)),
                  pl.BlockSpec((N_EXPERTS, 1), fixed)],
        out_specs=[pl.BlockSpec((ROW_TILE, D_MODEL), row),
                   pl.BlockSpec((TILE_SLOTS // PACK, D_MODEL), lambda i: (jnp.maximum(i - 1, 0), 0)),
                   pl.BlockSpec((ROW_TILE, TOP_K), prev_row),
                   pl.BlockSpec((ROW_TILE, TOP_K), prev_row),
                   pl.BlockSpec((1, N_EXPERTS, 1), per_prev_tile),
                   pl.BlockSpec((1, N_EXPERTS, 1), per_prev_tile)],
        scratch_shapes=[pltpu.VMEM((2, ROW_TILE, D_MODEL), BF16),
                        pltpu.VMEM((2, ROW_TILE, ROUTER_LANES), F32)],
        out_shape=[jax.ShapeDtypeStruct((t, D_MODEL), F32),
                   jax.ShapeDtypeStruct(((n_tiles + 1) * TILE_SLOTS // PACK, D_MODEL), U32),
                   jax.ShapeDtypeStruct((t, TOP_K), jnp.int32),
                   jax.ShapeDtypeStruct((t, TOP_K), F32),
                   jax.ShapeDtypeStruct((n_tiles, N_EXPERTS, 1), jnp.int32),
                   jax.ShapeDtypeStruct((n_tiles, N_EXPERTS, 1), jnp.int32)],
        compiler_params=_cparams(1),
        name="merge_router",
    )(gab, ya3, ya3, ya3, ya3, yb, xp, xs, wo, g_ffn, rw2, rb)


def _expert_kernel(blk_e, nxt_e, n_used, src0_ref, srcn_ref, dst_ref, xg_hbm, w1_hbm, b1_ref, w2_hbm,
                   b2_ref, yt_hbm, xbuf, obuf, gsem, ssem, w1s, w2s, wsem, w1b, w2b):
    b = pl.program_id(0)
    nu = n_used[0]

    gran_rows = GRANULE // PACK
    blk_rows = EXPERT_ROWS // PACK

    def granule_rows(g):
        return pl.ds(pl.multiple_of(g * gran_rows, gran_rows), gran_rows)

    def block_rows(j):
        return pl.ds(j * gran_rows, gran_rows)

    def start_gather(src_ref, slot):
        for j in range(GRAN_PER_BLOCK):
            pltpu.make_async_copy(xg_hbm.at[granule_rows(src_ref[0, 0, j])],
                                  xbuf.at[slot, block_rows(j)], gsem.at[slot]).start()

    def start_scatter(slot):
        for j in range(GRAN_PER_BLOCK):
            pltpu.make_async_copy(obuf.at[slot, block_rows(j)],
                                  yt_hbm.at[granule_rows(dst_ref[0, 0, j])], ssem.at[slot]).start()

    def wait_gather(slot):
        pltpu.make_async_copy(xg_hbm.at[pl.ds(0, blk_rows)], xbuf.at[slot], gsem.at[slot]).wait()

    def wait_scatter(slot):
        pltpu.make_async_copy(obuf.at[slot], yt_hbm.at[pl.ds(0, blk_rows)], ssem.at[slot]).wait()

    def weight_copies(e):
        return (pltpu.make_async_copy(w1_hbm.at[e], w1s, wsem.at[0]),
                pltpu.make_async_copy(w2_hbm.at[e], w2s, wsem.at[1]))

    first = b == 0
    new_expert = first | (blk_e[b] != blk_e[jnp.maximum(b - 1, 0)])

    @pl.when(first)
    def _():
        start_gather(src0_ref, 0)
        for c in weight_copies(blk_e[0]):
            c.start(priority=1)

    @pl.when((b >= 2) & (b <= nu))
    def _():
        wait_scatter(b % 2)

    @pl.when(b == nu)
    def _():
        wait_gather(b % 2)
        wait_scatter((b - 1) % 2)

    @pl.when(b < nu)
    def _():
        slot = b % 2
        wait_gather(slot)

        @pl.when(new_expert)
        def _():
            for c in weight_copies(blk_e[b]):
                c.wait()
            w1b[...] = w1s[...].astype(BF16)
            w2b[...] = w2s[...].astype(BF16)

            @pl.when(nxt_e[b] >= 0)
            def _():
                for c in weight_copies(nxt_e[b]):
                    c.start(priority=1)

        start_gather(srcn_ref, 1 - slot)
        x = pltpu.bitcast(xbuf[slot], BF16)
        u = _dot(x, w1b[...]) + b1_ref[0]
        a = jnp.minimum(u[:, :D_EXPERT], SWIGLU_LIMIT)
        lin = jnp.clip(u[:, D_EXPERT:], -SWIGLU_LIMIT, SWIGLU_LIMIT)
        hmid = a * jax.nn.sigmoid(SWIGLU_ALPHA * a) * (lin + 1.0)
        y = _dot(hmid.astype(BF16), w2b[...]) + b2_ref[0]
        obuf[slot] = pltpu.bitcast(y.astype(BF16), U32)
        start_scatter(slot)


def _experts(xg, plan, w1, b1, w2, b2):
    granule_src, granule_dst, blk_e, nxt_e, n_used = plan
    n_blocks = blk_e.shape[0]
    src3 = granule_src.reshape(n_blocks, 1, GRAN_PER_BLOCK)
    dst3 = granule_dst.reshape(n_blocks, 1, GRAN_PER_BLOCK)
    smem_blk = lambda imap: pl.BlockSpec((1, 1, GRAN_PER_BLOCK), imap, memory_space=pltpu.SMEM)
    by_expert = lambda b, be, *_: (be[b], 0, 0)
    grid_spec = pltpu.PrefetchScalarGridSpec(
        num_scalar_prefetch=3,
        grid=(n_blocks,),
        in_specs=[smem_blk(lambda b, *_: (0, 0, 0)),
                  smem_blk(lambda b, *_: (jnp.minimum(b + 1, n_blocks - 1), 0, 0)),
                  smem_blk(lambda b, *_: (b, 0, 0)),
                  pl.BlockSpec(memory_space=pl.ANY),
                  pl.BlockSpec(memory_space=pl.ANY),
                  pl.BlockSpec((1, 1, 2 * D_EXPERT), by_expert),
                  pl.BlockSpec(memory_space=pl.ANY),
                  pl.BlockSpec((1, 1, D_MODEL), by_expert)],
        out_specs=pl.BlockSpec(memory_space=pl.ANY),
        scratch_shapes=[pltpu.VMEM((2, EXPERT_ROWS // PACK, D_MODEL), U32),
                        pltpu.VMEM((2, EXPERT_ROWS // PACK, D_MODEL), U32),
                        pltpu.SemaphoreType.DMA((2,)),
                        pltpu.SemaphoreType.DMA((2,)),
                        pltpu.VMEM((D_MODEL, 2 * D_EXPERT), F32),
                        pltpu.VMEM((D_EXPERT, D_MODEL), F32),
                        pltpu.SemaphoreType.DMA((2,)),
                        pltpu.VMEM((D_MODEL, 2 * D_EXPERT), BF16),
                        pltpu.VMEM((D_EXPERT, D_MODEL), BF16)],
    )
    return pl.pallas_call(
        _expert_kernel,
        grid_spec=grid_spec,
        out_shape=jax.ShapeDtypeStruct(xg.shape, xg.dtype),
        input_output_aliases={6: 0},
        compiler_params=_cparams(1),
        name="experts",
    )(blk_e, nxt_e, n_used, src3, src3, dst3, xg, w1, b1.reshape(N_EXPERTS, 1, -1), w2,
      b2.reshape(N_EXPERTS, 1, -1))


def _combine_kernel(pos_ref, gate_ref, x1_ref, gf_ref, yt_ref, y_ref):
    rows = pos_ref.shape[0]
    slot_iota = lax.broadcasted_iota(jnp.int32, (rows, TILE_SLOTS), 1)
    pos = pos_ref[...]
    gates = gate_ref[...]
    weights = jnp.zeros((rows, TILE_SLOTS), F32)
    for kk in range(TOP_K):
        weights = jnp.where(slot_iota == pos[:, kk:kk + 1], gates[:, kk:kk + 1], weights)
    moe = _dot(weights.astype(BF16), pltpu.bitcast(yt_ref[...], BF16))
    y_ref[...] = _rms(x1_ref[...] + moe, gf_ref[...])


def _combine(pos, gates, x1, g_final, yt, *, tile0, n_tiles):
    row = lambda i: (tile0 + i, 0)
    return pl.pallas_call(
        _combine_kernel,
        grid=(n_tiles,),
        in_specs=[pl.BlockSpec((ROW_TILE, TOP_K), row),
                  pl.BlockSpec((ROW_TILE, TOP_K), row),
                  pl.BlockSpec((ROW_TILE, D_MODEL), row),
                  pl.BlockSpec((1, D_MODEL), lambda i: (0, 0)),
                  pl.BlockSpec((TILE_SLOTS // PACK, D_MODEL), row)],
        out_specs=pl.BlockSpec((ROW_TILE, D_MODEL), lambda i: (i, 0)),
        out_shape=jax.ShapeDtypeStruct((n_tiles * ROW_TILE, D_MODEL), F32),
        compiler_params=_cparams(1),
        name="combine",
    )(pos, gates, x1, g_final.reshape(1, D_MODEL), yt)


def _granule_plan(gran, goff):
    n_tiles = gran.shape[0]
    max_gran = (n_tiles * ROW_TILE * TOP_K + (GRANULE - 1) * n_tiles * N_EXPERTS) // GRANULE
    n_blocks = max_gran // GRAN_PER_BLOCK + N_EXPERTS + 1
    n_slots = n_blocks * GRAN_PER_BLOCK
    i32 = jnp.int32
    per_e = jnp.sum(gran, axis=0)
    region = (per_e + GRAN_PER_BLOCK - 1) // GRAN_PER_BLOCK * GRAN_PER_BLOCK
    region_end = jnp.cumsum(region)
    region_start = region_end - region
    g_t = gran.T
    run_start = (region_start[:, None] + jnp.cumsum(g_t, axis=1) - g_t).reshape(-1)
    run_end = run_start + g_t.reshape(-1)
    first_gran = (jnp.arange(n_tiles, dtype=i32)[None, :] * GRAN_PER_TILE + goff.T).reshape(-1)
    shift = first_gran - run_start
    d_shift = jnp.diff(shift, prepend=0)
    d_end = jnp.diff(run_end, prepend=0)
    slot = jnp.arange(n_slots, dtype=i32)
    started = slot[:, None] >= run_start[None, :]
    granule = slot + jnp.sum(jnp.where(started, d_shift[None, :], 0), axis=1)
    valid = slot < jnp.sum(jnp.where(started, d_end[None, :], 0), axis=1)
    spare = n_tiles * GRAN_PER_TILE
    j = slot % GRAN_PER_BLOCK
    odd_blk = (slot // GRAN_PER_BLOCK) % 2
    n_zero = GRAN_PER_TILE - 2 * GRAN_PER_BLOCK
    assert n_zero > 0
    granule_src = jnp.where(valid, granule, spare + 2 * GRAN_PER_BLOCK + j % n_zero).astype(i32)
    granule_dst = jnp.where(valid, granule, spare + odd_blk * GRAN_PER_BLOCK + j).astype(i32)
    blk = jnp.arange(n_blocks, dtype=i32) * GRAN_PER_BLOCK
    blk_e = jnp.minimum(jnp.sum(blk[:, None] >= region_end[None, :], axis=1), N_EXPERTS - 1).astype(i32)
    e = jnp.arange(N_EXPERTS, dtype=i32)
    later = jnp.where((e[None, :] > e[:, None]) & (per_e[None, :] > 0), e[None, :], N_EXPERTS)
    nxt = jnp.min(later, axis=1)
    nxt = jnp.where(nxt == N_EXPERTS, -1, nxt)
    nxt_e = jnp.sum(jnp.where(blk_e[:, None] == e[None, :], nxt[None, :], 0), axis=1).astype(i32)
    n_used = (region_end[-1:] // GRAN_PER_BLOCK).astype(i32)
    return granule_src, granule_dst, blk_e, nxt_e, n_used


def kernel(x_prompt, x_sample, state_s5_re, state_s5_im, state_hgrn, g_mix, w_in, s5_lambda_re,
           s5_lambda_im, s5_log_dt, s5_b_re, s5_b_im, s5_c_re, s5_c_im, s5_d, s5_w_glu, s5_w_gate,
           hgrn_lower_bound, hgrn_norm_g, hgrn_w_out, w_out, g_ffn, router_w, router_b, moe_w1,
           moe_b1, moe_w2, moe_b2, g_final):
    depth = g_mix.shape[0]
    n_p, l_p, _ = x_prompt.shape
    n_s, l_s, _ = x_sample.shape
    assert depth == 1 and n_p == S5_SEQS and l_p % ROW_TILE == 0
    assert l_s == S5_STEPS and n_s % S5_SEQS == 0 and ROW_TILE % l_s == 0
    li = 0
    t_p, t_s = n_p * l_p, n_s * l_s
    n_groups = n_s // S5_SEQS

    lb = jnp.cumsum(jax.nn.softmax(hgrn_lower_bound.astype(F32), axis=0), axis=0)[li].reshape(1, -1)
    s5_tabs = _s5_tables(s5_lambda_re[li], s5_lambda_im[li], s5_log_dt[li], s5_b_re[li],
                         s5_b_im[li], s5_c_re[li], s5_c_im[li])
    wg = jnp.concatenate([s5_w_glu[li], s5_w_gate[li]], axis=1).astype(BF16)
    rw_pad = jnp.pad(router_w[li].astype(F32), ((0, 0), (0, ROUTER_LANES - N_EXPERTS)))
    rw_hi = rw_pad.astype(BF16)
    rw2 = jnp.stack([rw_hi, (rw_pad - rw_hi.astype(F32)).astype(BF16)])

    xp = x_prompt.reshape(t_p, D_MODEL)
    xs = x_sample.reshape(t_s, D_MODEL)
    u, zh, gab = _inproj(xp, xs, g_mix[li], w_in[li].astype(BF16))

    s0_s5 = jnp.concatenate([state_s5_re[li].reshape(n_groups, S5_SEQS, S5_LANES),
                             state_s5_im[li].reshape(n_groups, S5_SEQS, S5_LANES)], axis=1).astype(F32)
    ya, s5_p, s5_s = _s5(u, s0_s5, s5_tabs, s5_d[li].reshape(1, -1), wg,
                         prompt_len=l_p, n_sample_groups=n_groups)
    yb, hg_p, hg_s = _hgrn(zh, lb, hgrn_norm_g[li].reshape(1, -1), hgrn_w_out[li].astype(BF16),
                           state_hgrn[li].astype(F32), n_prompt=n_p, prompt_len=l_p, sample_len=l_s)

    x1, xg, pos, gates, gran, goff = _merge(
        gab, ya, yb, xp, xs, w_out[li].astype(BF16), g_ffn[li].reshape(1, -1), rw2,
        router_b[li].reshape(-1, 1).astype(F32), prompt_len=l_p)

    plan = _granule_plan(gran.reshape(-1, N_EXPERTS), goff.reshape(-1, N_EXPERTS))
    yt = _experts(xg, plan, moe_w1[li], moe_b1[li], moe_w2[li], moe_b2[li])
    n_pt = t_p // ROW_TILE
    y_p = _combine(pos, gates, x1, g_final, yt, tile0=0, n_tiles=n_pt)
    y_s = _combine(pos, gates, x1, g_final, yt, tile0=n_pt, n_tiles=t_s // ROW_TILE)

    def s5_out(st, n, dtype):
        re = st[:, :S5_SEQS].reshape(1, n, S5_GROUPS, S5_STATE).astype(dtype)
        im = st[:, S5_SEQS:].reshape(1, n, S5_GROUPS, S5_STATE).astype(dtype)
        return re, im

    dt = x_prompt.dtype
    p_re, p_im = s5_out(s5_p, n_p, dt)
    s_re, _ = s5_out(s5_s, n_s, state_s5_re.dtype)
    _, s_im = s5_out(s5_s, n_s, state_s5_im.dtype)
    return (y_p.reshape(n_p, l_p, D_MODEL), y_s.reshape(n_s, l_s, D_MODEL),
            p_re, p_im, hg_p[None].astype(dt),
            s_re, s_im, hg_s[None].astype(state_hgrn.dtype))
```

```python
import functools
import math

import jax
import jax.numpy as jnp
from jax import lax
from jax.experimental import pallas as pl
from jax.experimental.pallas import tpu as pltpu

F32 = jnp.float32
BF16 = jnp.bfloat16

D_MODEL = 1024
NORM_EPS = 1e-6
S5_WIDTH = 512
S5_GROUP = 16
S5_GROUPS = 32
S5_STATE = 64
S5_LANES = S5_GROUPS * S5_STATE
S5_DT_CLIP = -1e-4
HG_WIDTH = 512
HG_HEADS = 4
HG_DK = 128
HG_DIAG = 128
N_EXPERTS = 32
TOP_K = 4
ROUTER_LANES = 128
D_EXPERT = 1024
SWIGLU_ALPHA = 1.702
SWIGLU_LIMIT = 7.0

ROW_TILE = 256
SCAN_ROWS = 8
S5_SEQS = 4
S5_STEPS = 64
S5_CH_TILE = 128
S5_LANE_TILE = 512
EXPERT_ROWS = 256
GRANULE = 8
TILE_SLOTS = ROW_TILE * TOP_K + N_EXPERTS * GRANULE
GRAN_PER_TILE = TILE_SLOTS // GRANULE
GRAN_PER_BLOCK = EXPERT_ROWS // GRANULE
PACK = 2
U32 = jnp.uint32
VMEM_LIMIT = 56 * 1024 * 1024


def _cparams(n_axes):
    return pltpu.CompilerParams(dimension_semantics=("arbitrary",) * n_axes,
                                vmem_limit_bytes=VMEM_LIMIT)


def _rms(x, g):
    r = lax.rsqrt(jnp.mean(x * x, axis=-1, keepdims=True) + NORM_EPS)
    return x * r * g


def _dot(a, b):
    return jnp.dot(a, b, preferred_element_type=F32)


def _dot_nt(a, b):
    return lax.dot_general(a, b, (((1,), (1,)), ((), ())), preferred_element_type=F32)


def _dot_tn(a, b):
    return lax.dot_general(a, b, (((0,), (0,)), ((), ())), preferred_element_type=F32)


def _two_stream_specs(n_first):
    return (pl.BlockSpec((ROW_TILE, D_MODEL), lambda i: (jnp.minimum(i, n_first - 1), 0)),
            pl.BlockSpec((ROW_TILE, D_MODEL), lambda i: (jnp.maximum(i - n_first, 0), 0)))


def _inproj_kernel(xp_ref, xs_ref, g_ref, w_ref, u_ref, zh_ref, gab_ref, *, n_first):
    x = jnp.where(pl.program_id(0) < n_first, xp_ref[...], xs_ref[...])
    h = _rms(x, g_ref[...]).astype(BF16)
    u_ref[...] = _dot(h, w_ref[:, :S5_WIDTH])
    zh_ref[...] = _dot(h, w_ref[:, S5_WIDTH:S5_WIDTH + 4 * HG_WIDTH])
    gab_ref[...] = _dot(h, w_ref[:, S5_WIDTH + 4 * HG_WIDTH:]).astype(BF16)


def _inproj(xp, xs, g_mix, w_in_bf):
    n_first = xp.shape[0] // ROW_TILE
    t = xp.shape[0] + xs.shape[0]
    ncols = w_in_bf.shape[1]
    row = lambda i: (i, 0)
    fixed = lambda i: (0, 0)
    return pl.pallas_call(
        functools.partial(_inproj_kernel, n_first=n_first),
        grid=(t // ROW_TILE,),
        in_specs=[*_two_stream_specs(n_first),
                  pl.BlockSpec((1, D_MODEL), fixed),
                  pl.BlockSpec((D_MODEL, ncols), fixed)],
        out_specs=[pl.BlockSpec((ROW_TILE, S5_WIDTH), row),
                   pl.BlockSpec((ROW_TILE, 4 * HG_WIDTH), row),
                   pl.BlockSpec((ROW_TILE, 2 * D_MODEL), row)],
        out_shape=[jax.ShapeDtypeStruct((t, S5_WIDTH), F32),
                   jax.ShapeDtypeStruct((t, 4 * HG_WIDTH), F32),
                   jax.ShapeDtypeStruct((t, 2 * D_MODEL), BF16)],
        compiler_params=_cparams(1),
        name="inproj",
    )(xp, xs, g_mix.reshape(1, D_MODEL), w_in_bf)


def _s5_kernel(u0_ref, u1_ref, u2_ref, u3_ref, s0_ref, perm_ref, pinv_ref, wb_ref, wc_ref,
               coef_ref, d_ref, wg_ref, ya_ref, sp_out, ss_out, buf0, buf1, ub0, ub1, v_ref,
               *, n_prompt_steps, n_steps):
    i = pl.program_id(0)
    ia = jnp.minimum(i, n_steps - 1)

    @pl.when(i == 0)
    def _():
        v_ref[...] = jnp.zeros_like(v_ref)
        buf1[...] = jnp.zeros_like(buf1)
        ub1[...] = jnp.zeros_like(ub1)

    @pl.when(ia >= n_prompt_steps)
    def _():
        v_ref[...] = s0_ref[0]

    u_refs = (u0_ref, u1_ref, u2_ref, u3_ref)
    consts = (perm_ref, pinv_ref, wb_ref, wc_ref, coef_ref, d_ref, wg_ref)

    @pl.when(i % 2 == 0)
    def _():
        _s5_body(u_refs, consts, ya_ref, v_ref, cur=(buf0, ub0), prev=(buf1, ub1))

    @pl.when(i % 2 == 1)
    def _():
        _s5_body(u_refs, consts, ya_ref, v_ref, cur=(buf1, ub1), prev=(buf0, ub0))

    @pl.when(ia < n_prompt_steps)
    def _():
        sp_out[0] = v_ref[...]

    @pl.when(ia >= n_prompt_steps)
    def _():
        ss_out[0] = v_ref[...]


def _s5_body(u_refs, consts, ya_ref, v_ref, *, cur, prev):
    perm_ref, pinv_ref, wb_ref, wc_ref, coef_ref, d_ref, wg_ref = consts
    buf_a, ub_a = cur
    buf_b, ub_b = prev
    n_tiles = S5_WIDTH // S5_CH_TILE

    u = jnp.concatenate([r[0] for r in u_refs], axis=0)
    ub_a[...] = u
    rows = 2 * u.shape[0]
    up = _dot(perm_ref[...], u.astype(BF16))
    re_half = (lax.broadcasted_iota(jnp.int32, (rows, S5_CH_TILE), 0) & S5_SEQS) == 0
    ws = []
    for n in range(n_tiles):
        ls = slice(n * S5_LANE_TILE, (n + 1) * S5_LANE_TILE)
        un = up[:, n * S5_CH_TILE:(n + 1) * S5_CH_TILE]
        lhs = jnp.concatenate([jnp.where(re_half, un, 0.0), jnp.where(re_half, 0.0, un)], axis=1)
        buf_a[:, ls] = _dot(lhs.astype(BF16), wb_ref[n])
        z = _dot(buf_b[:, ls].astype(BF16), wc_ref[n])
        ws.append(jnp.where(re_half, z[:, :S5_CH_TILE], z[:, S5_CH_TILE:]))
    w_hi, w_lo = _split2(jnp.concatenate(ws, axis=1))

    def scan(v, t0, t1):
        for t in range(t0, t1):
            rs = slice(t * SCAN_ROWS, (t + 1) * SCAN_ROWS)
            v = coef_ref[0] * v + coef_ref[1] * pltpu.roll(v, S5_SEQS, axis=0) + buf_a[rs, :]
            buf_a[rs, :] = v
        return v

    quarter = S5_STEPS // 4
    v = scan(v_ref[...], 0, quarter)
    y = _dot(pinv_ref[...], w_hi) + _dot(pinv_ref[...], w_lo) + d_ref[...] * ub_b[...]
    y = jax.nn.gelu(y).astype(BF16)
    v = scan(v, quarter, 2 * quarter)
    p_glu = _dot(y, wg_ref[:, :D_MODEL])
    v = scan(v, 2 * quarter, 3 * quarter)
    p_gate = _dot(y, wg_ref[:, D_MODEL:])
    v = scan(v, 3 * quarter, S5_STEPS)
    ya_ref[0] = (p_glu * jax.nn.sigmoid(p_gate)).astype(BF16).reshape(S5_SEQS, S5_STEPS, D_MODEL)
    v_ref[...] = v


def _s5(u, s0_sample, tables, d_skip, wg, *, prompt_len, n_sample_groups):
    perm, pinv, wb, wc, coef = tables
    n_prompt_steps = prompt_len // S5_STEPS
    n_steps = n_prompt_steps + n_sample_groups
    sample_base = S5_SEQS * n_prompt_steps
    u3 = u.reshape(-1, S5_STEPS, S5_WIDTH)

    def u_spec(s):
        def imap(i):
            i = jnp.minimum(i, n_steps - 1)
            prompt_blk = s * n_prompt_steps + i
            sample_blk = sample_base + S5_SEQS * (i - n_prompt_steps) + s
            return (jnp.where(i < n_prompt_steps, prompt_blk, sample_blk), 0, 0)
        return pl.BlockSpec((1, S5_STEPS, S5_WIDTH), imap)

    sample_grp = lambda i: (jnp.clip(i - n_prompt_steps, 0, n_sample_groups - 1), 0, 0)
    whole = lambda a: pl.BlockSpec(a.shape, lambda i: (0,) * a.ndim)
    state_blk = (1, 2 * S5_SEQS, S5_LANES)
    rows = S5_SEQS * S5_STEPS
    return pl.pallas_call(
        functools.partial(_s5_kernel, n_prompt_steps=n_prompt_steps, n_steps=n_steps),
        grid=(n_steps + 1,),
        in_specs=[u_spec(0), u_spec(1), u_spec(2), u_spec(3),
                  pl.BlockSpec(state_blk, sample_grp),
                  whole(perm), whole(pinv), whole(wb), whole(wc), whole(coef), whole(d_skip), whole(wg)],
        out_specs=[pl.BlockSpec((1, S5_SEQS, S5_STEPS, D_MODEL),
                                lambda i: (jnp.maximum(i - 1, 0), 0, 0, 0)),
                   pl.BlockSpec(state_blk, lambda i: (0, 0, 0)),
                   pl.BlockSpec(state_blk, sample_grp)],
        out_shape=[jax.ShapeDtypeStruct((n_steps, S5_SEQS, S5_STEPS, D_MODEL), BF16),
                   jax.ShapeDtypeStruct(state_blk, F32),
                   jax.ShapeDtypeStruct((n_sample_groups,) + state_blk[1:], F32)],
        scratch_shapes=[pltpu.VMEM((2 * rows, S5_LANES), F32),
                        pltpu.VMEM((2 * rows, S5_LANES), F32),
                        pltpu.VMEM((rows, S5_WIDTH), F32),
                        pltpu.VMEM((rows, S5_WIDTH), F32),
                        pltpu.VMEM((2 * S5_SEQS, S5_LANES), F32)],
        compiler_params=_cparams(1),
        name="s5",
    )(u3, u3, u3, u3, s0_sample, perm, pinv, wb, wc, coef, d_skip, wg)


def _s5_tables(lam_re, lam_im, log_dt, b_re, b_im, c_re, c_im):
    lr = jnp.minimum(lam_re.astype(F32), S5_DT_CLIP)
    li = lam_im.astype(F32)
    dt = jnp.exp(log_dt.astype(F32))[:, None]
    mag = jnp.exp(lr * dt)
    ar = mag * jnp.cos(li * dt)
    ai = mag * jnp.sin(li * dt)
    den = lr * lr + li * li
    zr = ((ar - 1.0) * lr + ai * li) / den
    zi = (ai * lr - (ar - 1.0) * li) / den
    br = b_re.astype(F32)
    bi = b_im.astype(F32)
    bbr = zr[..., None] * br - zi[..., None] * bi
    bbi = zr[..., None] * bi + zi[..., None] * br

    ar_f, ai_f = ar.reshape(1, -1), ai.reshape(1, -1)
    sign = jnp.where(jnp.arange(2 * S5_SEQS)[:, None] < S5_SEQS, -1.0, 1.0)
    coef = jnp.stack([jnp.broadcast_to(ar_f, (2 * S5_SEQS, S5_LANES)), sign * ai_f]).astype(F32)

    g_per_tile = S5_CH_TILE // S5_GROUP
    n_tiles = S5_GROUPS // g_per_tile
    eye = jnp.eye(g_per_tile, dtype=F32)

    def b_tile(bb):
        x = bb.reshape(n_tiles, g_per_tile, S5_STATE, S5_GROUP)
        x = x[:, :, :, :, None] * eye[None, :, None, None, :]
        return x.transpose(0, 1, 3, 4, 2).reshape(n_tiles, S5_CH_TILE, S5_LANE_TILE)

    wb = jnp.concatenate([b_tile(bbr), b_tile(bbi)], axis=1).astype(BF16)

    def c_tile(cc):
        x = cc.astype(F32).reshape(n_tiles, g_per_tile, S5_GROUP, S5_STATE)
        x = x[:, :, :, :, None] * eye[None, :, None, None, :]
        return x.transpose(0, 1, 3, 4, 2).reshape(n_tiles, S5_LANE_TILE, S5_CH_TILE)

    wc = jnp.concatenate([c_tile(c_re), -c_tile(c_im)], axis=2).astype(BF16)

    r = jnp.arange(2 * S5_SEQS * S5_STEPS)
    src = (r % S5_SEQS) * S5_STEPS + r // (2 * S5_SEQS)
    perm = (src[:, None] == jnp.arange(S5_SEQS * S5_STEPS)[None, :]).astype(BF16)
    return perm, perm.T, wb, wc, coef


def _split2(x):
    hi = x.astype(BF16)
    lo = (x - hi.astype(F32)).astype(BF16)
    return hi, lo


def _split3(x):
    hi = x.astype(BF16)
    r = x - hi.astype(F32)
    mid = r.astype(BF16)
    lo = (r - mid.astype(F32)).astype(BF16)
    return hi, mid, lo


def _hgrn_kernel(zh_ref, lb_ref, gn_ref, wo_ref, s0_ref, yb_ref, sp_out, ss_out,
                 st, gc_ref, att_ref, cross_ref, o_ref, *, n_prompt_tiles, chunks_per_seq,
                 sample_len):
    i = pl.program_id(0)
    refs = (zh_ref, lb_ref, gn_ref, wo_ref, yb_ref, st, gc_ref, att_ref, cross_ref, o_ref)

    @pl.when(i < n_prompt_tiles)
    def _():
        j = i % chunks_per_seq

        @pl.when(j == 0)
        def _():
            st[0] = jnp.zeros((HG_HEADS, HG_DK, HG_DK), F32)

        _hgrn_tile(*refs, n_seg=1, seg_len=ROW_TILE)

        @pl.when(j == chunks_per_seq - 1)
        def _():
            for h in range(HG_HEADS):
                sp_out[0, h] = st[0, h].T

    @pl.when(i >= n_prompt_tiles)
    def _():
        n_seg = ROW_TILE // sample_len
        for seg in range(n_seg):
            for h in range(HG_HEADS):
                st[seg, h] = s0_ref[seg, h].T
        _hgrn_tile(*refs, n_seg=n_seg, seg_len=sample_len)
        for seg in range(n_seg):
            for h in range(HG_HEADS):
                ss_out[seg, h] = st[seg, h].T


def _hgrn_tile(zh_ref, lb_ref, gn_ref, wo_ref, yb_ref, st, gc_ref, att_ref, cross_ref, o_ref,
               *, n_seg, seg_len):
    rows = n_seg * seg_len
    zq = zh_ref[:, 0:HG_WIDTH]
    zf = zh_ref[:, HG_WIDTH:2 * HG_WIDTH]
    v = zh_ref[:, 2 * HG_WIDTH:3 * HG_WIDTH]
    lb = lb_ref[...]
    q = zq * jax.nn.sigmoid(zq)
    f = lb + (1.0 - lb) * jax.nn.sigmoid(zf)
    k = 1.0 - f
    g = jnp.log(f)
    vb = v.astype(BF16)

    ri = lax.broadcasted_iota(jnp.int32, (rows, rows), 0)
    ci = lax.broadcasted_iota(jnp.int32, (rows, rows), 1)
    seg_shift = int(math.log2(seg_len))
    same_seg = (ri >> seg_shift) == (ci >> seg_shift)

    tri = jnp.where(same_seg & (ci <= ri), 1.0, 0.0).astype(BF16)
    g3 = jnp.concatenate(_split3(g), axis=1)
    gcs = _dot(tri, g3)
    gc = gcs[:, :HG_WIDTH] + gcs[:, HG_WIDTH:2 * HG_WIDTH] + gcs[:, 2 * HG_WIDTH:]
    gc_ref[...] = gc

    n_diag = rows // HG_DIAG
    dr = lax.broadcasted_iota(jnp.int32, (HG_DIAG, HG_DIAG), 0)
    dc = lax.broadcasted_iota(jnp.int32, (HG_DIAG, HG_DIAG), 1)
    dxor = jnp.where(dc < dr, dr ^ dc, 0)
    row_id = lax.broadcasted_iota(jnp.int32, (rows, HG_WIDTH), 0)
    qb = q.astype(BF16)
    kb = k.astype(BF16)
    for h in range(HG_HEADS):
        hs = slice(h * HG_DK, (h + 1) * HG_DK)
        for d in range(n_diag):
            ds_ = slice(d * HG_DIAG, (d + 1) * HG_DIAG)
            att_ref[h, d] = jnp.where(dr == dc, _dot_nt(qb[ds_, hs], kb[ds_, hs]), 0.0)

    g2 = jnp.concatenate(_split2(g), axis=1)
    level = seg_len // 2
    while level >= 1:
        if 2 * level >= SCAN_ROWS:
            mids, picks = [], []
            for b in range(rows // (2 * level)):
                lo = b * 2 * level
                mids.append(jnp.broadcast_to(gc_ref[lo + level - 1:lo + level, :], (2 * level, HG_WIDTH)))
                picks += [k[lo:lo + level], q[lo + level:lo + 2 * level]]
            gmid = mids[0] if len(mids) == 1 else jnp.concatenate(mids, axis=0)
            qk = jnp.concatenate(picks, axis=0)
            dexp = -jnp.abs(gc - gmid)
        else:
            mid = (ri & ~(2 * level - 1)) + (level - 1)
            upper = (ri & level) != 0
            w = (ci > jnp.where(upper, mid, ri)) & (ci <= jnp.where(upper, ri, mid))
            dsum = _dot(jnp.where(w, 1.0, 0.0).astype(BF16), g2)
            dexp = dsum[:, :HG_WIDTH] + dsum[:, HG_WIDTH:]
            qk = jnp.where((row_id & level) != 0, q, k)
        m = (qk * jnp.exp(dexp)).astype(BF16)
        if level >= HG_DIAG:
            for h in range(HG_HEADS):
                hs = slice(h * HG_DK, (h + 1) * HG_DK)
                cross_ref[h] = _dot_nt(m[level:, hs], m[:level, hs])
        else:
            sel = (dxor >> int(math.log2(level))) == 1
            for h in range(HG_HEADS):
                hs = slice(h * HG_DK, (h + 1) * HG_DK)
                for d in range(n_diag):
                    ds_ = slice(d * HG_DIAG, (d + 1) * HG_DIAG)
                    att_ref[h, d] = jnp.where(sel, _dot_nt(m[ds_, hs], m[ds_, hs]), att_ref[h, d])
        level //= 2

    qg = (q * jnp.exp(gc)).astype(BF16)
    for h in range(HG_HEADS):
        hs = slice(h * HG_DK, (h + 1) * HG_DK)
        for d in range(n_diag):
            ds_ = slice(d * HG_DIAG, (d + 1) * HG_DIAG)
            o_ref[ds_, hs] = _dot(att_ref[h, d].astype(BF16), vb[ds_, hs])
        if seg_len > HG_DIAG:
            o_ref[HG_DIAG:, hs] += _dot(cross_ref[h].astype(BF16), vb[:HG_DIAG, hs])

    for seg in range(n_seg):
        rs = slice(seg * seg_len, (seg + 1) * seg_len)
        last = (seg + 1) * seg_len - 1
        g_last = gc_ref[last:last + 1, :]
        kd = (k[rs] * jnp.exp(g_last - gc[rs])).astype(BF16)
        decay = jnp.exp(g_last)
        for h in range(HG_HEADS):
            hs = slice(h * HG_DK, (h + 1) * HG_DK)
            s_t = st[seg, h]
            o_ref[rs, hs] += _dot_nt(qg[rs, hs], s_t.astype(BF16))
            st[seg, h] = s_t * decay[:, hs] + _dot_tn(vb[rs, hs], kd[:, hs])

    zg = zh_ref[:, 3 * HG_WIDTH:]
    gate = zg * jax.nn.sigmoid(zg)
    outs = []
    for h in range(HG_HEADS):
        hs = slice(h * HG_DK, (h + 1) * HG_DK)
        outs.append(_rms(o_ref[:, hs], gn_ref[...]) * gate[:, hs])
    yb_ref[...] = _dot(jnp.concatenate(outs, axis=1).astype(BF16), wo_ref[...]).astype(BF16)


def _hgrn(zh, lb, gn, wo, s0_sample, *, n_prompt, prompt_len, sample_len):
    t = zh.shape[0]
    chunks_per_seq = prompt_len // ROW_TILE
    n_prompt_tiles = n_prompt * chunks_per_seq
    n_seg = ROW_TILE // sample_len
    n_sample = s0_sample.shape[0]
    row = lambda i: (i, 0)
    fixed = lambda i: (0, 0)
    state = (HG_HEADS, HG_DK, HG_DK)
    prompt_seq = lambda i: (jnp.minimum(i // chunks_per_seq, n_prompt - 1), 0, 0, 0)
    sample_grp = lambda i: (jnp.maximum(i - n_prompt_tiles, 0), 0, 0, 0)
    kern = functools.partial(_hgrn_kernel, n_prompt_tiles=n_prompt_tiles,
                             chunks_per_seq=chunks_per_seq, sample_len=sample_len)
    return pl.pallas_call(
        kern,
        grid=(t // ROW_TILE,),
        in_specs=[pl.BlockSpec((ROW_TILE, 4 * HG_WIDTH), row),
                  pl.BlockSpec((1, HG_WIDTH), fixed),
                  pl.BlockSpec((1, HG_DK), fixed),
                  pl.BlockSpec(wo.shape, fixed),
                  pl.BlockSpec((n_seg,) + state, sample_grp)],
        out_specs=[pl.BlockSpec((ROW_TILE, D_MODEL), row),
                   pl.BlockSpec((1,) + state, prompt_seq),
                   pl.BlockSpec((n_seg,) + state, sample_grp)],
        out_shape=[jax.ShapeDtypeStruct((t, D_MODEL), BF16),
                   jax.ShapeDtypeStruct((n_prompt,) + state, F32),
                   jax.ShapeDtypeStruct((n_sample,) + state, F32)],
        scratch_shapes=[pltpu.VMEM((n_seg,) + state, F32),
                        pltpu.VMEM((ROW_TILE, HG_WIDTH), F32),
                        pltpu.VMEM((HG_HEADS, ROW_TILE // HG_DIAG, HG_DIAG, HG_DIAG), F32),
                        pltpu.VMEM((HG_HEADS, HG_DIAG, HG_DIAG), F32),
                        pltpu.VMEM((ROW_TILE, HG_WIDTH), F32)],
        compiler_params=_cparams(1),
        name="hgrn2",
    )(zh, lb, gn, wo, s0_sample)


def _merge_kernel(gab_ref, ya0_ref, ya1_ref, ya2_ref, ya3_ref, yb_ref, xp_ref, xs_ref, wo_ref,
                  gf_ref, rw_ref, rb_ref, x1_ref, xg_ref, pos_ref, gate_ref, cnt_ref, off_ref,
                  hbuf, lbuf, *, n_first, n_tiles):
    i = pl.program_id(0)
    cur = i % 2
    prev = 1 - cur

    @pl.when(i == 0)
    def _():
        hbuf[1] = jnp.zeros(hbuf.shape[1:], hbuf.dtype)
        lbuf[1] = jnp.zeros(lbuf.shape[1:], lbuf.dtype)

    logits_tok = lbuf[prev]

    ga = gab_ref[:, :D_MODEL].astype(F32)
    gb = gab_ref[:, D_MODEL:].astype(F32)
    ya = jnp.concatenate([r[0] for r in (ya0_ref, ya1_ref, ya2_ref, ya3_ref)], axis=0).astype(F32)
    mixed = (jax.nn.sigmoid(ga) * ya + jax.nn.sigmoid(gb) * yb_ref[...].astype(F32)).astype(BF16)

    rows = logits_tok.shape[0]
    logits = logits_tok.T[:N_EXPERTS] + rb_ref[...]
    eid = lax.broadcasted_iota(jnp.int32, (N_EXPERTS, rows), 0).astype(F32)
    lg = logits
    sel = jnp.zeros((N_EXPERTS, rows), F32)
    vals, idxs = [], []
    for _ in range(TOP_K):
        m = jnp.max(lg, axis=0, keepdims=True)
        idx = jnp.min(jnp.where(lg == m, eid, float(N_EXPERTS)), axis=0, keepdims=True)
        hit = eid == idx
        vals.append(m)
        idxs.append(idx)
        lg = jnp.where(hit, -jnp.inf, lg)
        sel = jnp.where(hit, 1.0, sel)

    es = [jnp.exp(vv - vals[0]) for vv in vals]
    tot = es[0] + es[1] + es[2] + es[3]

    ri = lax.broadcasted_iota(jnp.int32, (rows, rows), 0)
    ci = lax.broadcasted_iota(jnp.int32, (rows, rows), 1)
    earlier = jnp.where(ri < ci, 1.0, 0.0).astype(BF16)
    rank = _dot(sel.astype(BF16), earlier)
    gran = jnp.floor((jnp.sum(sel, axis=1, keepdims=True) + (GRANULE - 1)) * (1.0 / GRANULE))
    ei = lax.broadcasted_iota(jnp.int32, (N_EXPERTS, N_EXPERTS), 0)
    ej = lax.broadcasted_iota(jnp.int32, (N_EXPERTS, N_EXPERTS), 1)
    gran_b = jnp.broadcast_to(gran, (N_EXPERTS, rows))
    off = _dot(jnp.where(ej < ei, 1.0, 0.0).astype(BF16), gran_b.astype(BF16))
    slot_of = rank + off * float(GRANULE)
    poss = [jnp.sum(jnp.where(eid == idx, slot_of, 0.0), axis=0, keepdims=True) for idx in idxs]

    per_tok = jnp.concatenate(poss + [e / tot for e in es], axis=0).T

    x = jnp.where(jnp.minimum(i, n_tiles - 1) < n_first, xp_ref[...], xs_ref[...])
    x1 = x + _dot(mixed, wo_ref[...])
    x1_ref[...] = x1

    slot_iota = lax.broadcasted_iota(jnp.int32, (rows, TILE_SLOTS), 1).astype(F32)
    onehot = jnp.zeros((rows, TILE_SLOTS), F32)
    for kk in range(TOP_K):
        onehot = jnp.where(slot_iota == per_tok[:, kk:kk + 1], 1.0, onehot)
    sorted_rows = _dot_tn(onehot.astype(BF16), hbuf[prev])
    xg_ref[...] = pltpu.bitcast(jnp.where(i > n_tiles, 0.0, sorted_rows).astype(BF16), U32)

    a_hi, a_lo = _split2(_rms(x1, gf_ref[...]))
    hbuf[cur] = a_hi
    lbuf[cur] = _dot(a_hi, rw_ref[0]) + _dot(a_hi, rw_ref[1]) + _dot(a_lo, rw_ref[0])

    pos_ref[...] = per_tok[:, :TOP_K].astype(jnp.int32)
    gate_ref[...] = per_tok[:, TOP_K:]
    cnt_ref[0] = gran.astype(jnp.int32)
    off_ref[0] = off[:, :1].astype(jnp.int32)


def _merge(gab, ya_steps, yb, xp, xs, wo, g_ffn, rw2, rb, *, prompt_len):
    n_first = xp.shape[0] // ROW_TILE
    t = xp.shape[0] + xs.shape[0]
    n_tiles = t // ROW_TILE
    chunks_per_seq = prompt_len // ROW_TILE
    quarters = ROW_TILE // S5_STEPS
    ya3 = ya_steps.reshape(-1, S5_STEPS, D_MODEL)

    def ya_spec(q):
        def imap(i):
            i = jnp.minimum(i, n_tiles - 1)
            seq, chunk = i // chunks_per_seq, i % chunks_per_seq
            prompt_blk = (chunk * quarters + q) * S5_SEQS + seq
            return (jnp.where(i < n_first, prompt_blk, i * quarters + q), 0, 0)
        return pl.BlockSpec((1, S5_STEPS, D_MODEL), imap)

    tile = lambda i: jnp.minimum(i, n_tiles - 1)
    prev_tile = lambda i: jnp.clip(i - 1, 0, n_tiles - 1)
    row = lambda i: (tile(i), 0)
    prev_row = lambda i: (prev_tile(i), 0)
    fixed = lambda i: (0, 0)
    per_prev_tile = lambda i: (prev_tile(i), 0, 0)
    xp_spec, xs_spec = _two_stream_specs(n_first)
    return pl.pallas_call(
        functools.partial(_merge_kernel, n_first=n_first, n_tiles=n_tiles),
        grid=(n_tiles + 2,),
        in_specs=[pl.BlockSpec((ROW_TILE, 2 * D_MODEL), row),
                  ya_spec(0), ya_spec(1), ya_spec(2), ya_spec(3),
                  pl.BlockSpec((ROW_TILE, D_MODEL), row),
                  pl.BlockSpec(xp_spec.block_shape, lambda i: xp_spec.index_map(tile(i))),
                  pl.BlockSpec(xs_spec.block_shape, lambda i: xs_spec.index_map(tile(i))),
                  pl.BlockSpec((D_MODEL, D_MODEL), fixed),
                  pl.BlockSpec((1, D_MODEL), fixed),
                  pl.BlockSpec((2, D_MODEL, ROUTER_LANES), lambda i: (0, 0, 0)),
                  pl.BlockSpec((N_EXPERTS, 1), fixed)],
        out_specs=[pl.BlockSpec((ROW_TILE, D_MODEL), row),
                   pl.BlockSpec((TILE_SLOTS // PACK, D_MODEL), lambda i: (jnp.maximum(i - 1, 0), 0)),
                   pl.BlockSpec((ROW_TILE, TOP_K), prev_row),
                   pl.BlockSpec((ROW_TILE, TOP_K), prev_row),
                   pl.BlockSpec((1, N_EXPERTS, 1), per_prev_tile),
                   pl.BlockSpec((1, N_EXPERTS, 1), per_prev_tile)],
        scratch_shapes=[pltpu.VMEM((2, ROW_TILE, D_MODEL), BF16),
                        pltpu.VMEM((2, ROW_TILE, ROUTER_LANES), F32)],
        out_shape=[jax.ShapeDtypeStruct((t, D_MODEL), F32),
                   jax.ShapeDtypeStruct(((n_tiles + 1) * TILE_SLOTS // PACK, D_MODEL), U32),
                   jax.ShapeDtypeStruct((t, TOP_K), jnp.int32),
                   jax.ShapeDtypeStruct((t, TOP_K), F32),
                   jax.ShapeDtypeStruct((n_tiles, N_EXPERTS, 1), jnp.int32),
                   jax.ShapeDtypeStruct((n_tiles, N_EXPERTS, 1), jnp.int32)],
        compiler_params=_cparams(1),
        name="merge_router",
    )(gab, ya3, ya3, ya3, ya3, yb, xp, xs, wo, g_ffn, rw2, rb)


def _expert_kernel(blk_e, nxt_e, n_used, src0_ref, srcn_ref, dst_ref, xg_hbm, w1_hbm, b1_ref, w2_hbm,
                   b2_ref, yt_hbm, xbuf, obuf, hbuf, gsem, ssem, w1s, w2s, wsem, w1b, w2b):
    b = pl.program_id(0)
    nu = n_used[0]

    gran_rows = GRANULE // PACK
    blk_rows = EXPERT_ROWS // PACK

    def granule_rows(g):
        return pl.ds(pl.multiple_of(g * gran_rows, gran_rows), gran_rows)

    def block_rows(j):
        return pl.ds(j * gran_rows, gran_rows)

    def start_gather(src_ref, slot):
        for j in range(GRAN_PER_BLOCK):
            pltpu.make_async_copy(xg_hbm.at[granule_rows(src_ref[0, 0, j])],
                                  xbuf.at[slot, block_rows(j)], gsem.at[slot]).start()

    def start_scatter(slot):
        for j in range(GRAN_PER_BLOCK):
            pltpu.make_async_copy(obuf.at[slot, block_rows(j)],
                                  yt_hbm.at[granule_rows(dst_ref[0, 0, j])], ssem.at[slot]).start()

    def wait_gather(slot):
        pltpu.make_async_copy(xg_hbm.at[pl.ds(0, blk_rows)], xbuf.at[slot], gsem.at[slot]).wait()

    def wait_scatter(slot):
        pltpu.make_async_copy(obuf.at[slot], yt_hbm.at[pl.ds(0, blk_rows)], ssem.at[slot]).wait()

    def w1_copy(e):
        return pltpu.make_async_copy(w1_hbm.at[e], w1s, wsem.at[0])

    def w2_copy(e):
        return pltpu.make_async_copy(w2_hbm.at[e], w2s, wsem.at[1])

    s = b
    prev_blk = jnp.maximum(s - 1, 0)
    new_a = (s == 0) | (blk_e[s] != blk_e[prev_blk])
    new_b = (s == 1) | ((s >= 2) & (blk_e[prev_blk] != blk_e[jnp.maximum(s - 2, 0)]))

    @pl.when(s == 0)
    def _():
        start_gather(src0_ref, 0)
        w1_copy(blk_e[0]).start(priority=1)
        w2_copy(blk_e[0]).start(priority=1)

    @pl.when((s >= 3) & (s <= nu))
    def _():
        wait_scatter((s - 1) % 2)

    @pl.when((s < nu) & new_a)
    def _():
        w1_copy(blk_e[s]).wait()
        w1b[...] = w1s[...].astype(BF16)

        @pl.when(nxt_e[s] >= 0)
        def _():
            w1_copy(nxt_e[s]).start(priority=1)

    @pl.when((s >= 1) & (s <= nu) & new_b)
    def _():
        w2_copy(blk_e[prev_blk]).wait()
        w2b[...] = w2s[...].astype(BF16)

        @pl.when(nxt_e[prev_blk] >= 0)
        def _():
            w2_copy(nxt_e[prev_blk]).start(priority=1)

    def stage_b(slot):
        y = _dot(hbuf[slot], w2b[...]) + b2_ref[0]
        obuf[slot] = pltpu.bitcast(y.astype(BF16), U32)
        start_scatter(slot)

    def stage_a(slot):
        start_gather(srcn_ref, 1 - slot)
        x = pltpu.bitcast(xbuf[slot], BF16)
        u = _dot(x, w1b[...]) + b1_ref[0]
        a = jnp.minimum(u[:, :D_EXPERT], SWIGLU_LIMIT)
        lin = jnp.clip(u[:, D_EXPERT:], -SWIGLU_LIMIT, SWIGLU_LIMIT)
        hbuf[slot] = (a * jax.nn.sigmoid(SWIGLU_ALPHA * a) * (lin + 1.0)).astype(BF16)

    @pl.when(s < nu)
    def _():
        wait_gather(s % 2)

    @pl.when(s == 0)
    def _():
        stage_a(0)

    for parity in range(2):
        @pl.when((s >= 1) & (s < nu) & (s % 2 == parity))
        def _():
            stage_a(parity)
            stage_b(1 - parity)

    @pl.when(s == nu)
    def _():
        for parity in range(2):
            @pl.when(s % 2 == parity)
            def _():
                stage_b(1 - parity)
        wait_gather(s % 2)
        wait_scatter((s - 1) % 2)

        @pl.when(s >= 2)
        def _():
            wait_scatter(s % 2)


def _experts(xg, plan, w1, b1, w2, b2):
    granule_src, granule_dst, blk_e, nxt_e, n_used = plan
    n_blocks = blk_e.shape[0]
    src3 = granule_src.reshape(n_blocks, 1, GRAN_PER_BLOCK)
    dst3 = granule_dst.reshape(n_blocks, 1, GRAN_PER_BLOCK)
    smem_blk = lambda imap: pl.BlockSpec((1, 1, GRAN_PER_BLOCK), imap, memory_space=pltpu.SMEM)
    expert_a = lambda s, be, *_: (be[s], 0, 0)
    expert_b = lambda s, be, *_: (be[jnp.maximum(s - 1, 0)], 0, 0)
    grid_spec = pltpu.PrefetchScalarGridSpec(
        num_scalar_prefetch=3,
        grid=(n_blocks,),
        in_specs=[smem_blk(lambda s, *_: (0, 0, 0)),
                  smem_blk(lambda s, *_: (jnp.minimum(s + 1, n_blocks - 1), 0, 0)),
                  smem_blk(lambda s, *_: (jnp.maximum(s - 1, 0), 0, 0)),
                  pl.BlockSpec(memory_space=pl.ANY),
                  pl.BlockSpec(memory_space=pl.ANY),
                  pl.BlockSpec((1, 1, 2 * D_EXPERT), expert_a),
                  pl.BlockSpec(memory_space=pl.ANY),
                  pl.BlockSpec((1, 1, D_MODEL), expert_b)],
        out_specs=pl.BlockSpec(memory_space=pl.ANY),
        scratch_shapes=[pltpu.VMEM((2, EXPERT_ROWS // PACK, D_MODEL), U32),
                        pltpu.VMEM((2, EXPERT_ROWS // PACK, D_MODEL), U32),
                        pltpu.VMEM((2, EXPERT_ROWS, D_EXPERT), BF16),
                        pltpu.SemaphoreType.DMA((2,)),
                        pltpu.SemaphoreType.DMA((2,)),
                        pltpu.VMEM((D_MODEL, 2 * D_EXPERT), F32),
                        pltpu.VMEM((D_EXPERT, D_MODEL), F32),
                        pltpu.SemaphoreType.DMA((2,)),
                        pltpu.VMEM((D_MODEL, 2 * D_EXPERT), BF16),
                        pltpu.VMEM((D_EXPERT, D_MODEL), BF16)],
    )
    return pl.pallas_call(
        _expert_kernel,
        grid_spec=grid_spec,
        out_shape=jax.ShapeDtypeStruct(xg.shape, xg.dtype),
        input_output_aliases={6: 0},
        compiler_params=_cparams(1),
        name="experts",
    )(blk_e, nxt_e, n_used, src3, src3, dst3, xg, w1, b1.reshape(N_EXPERTS, 1, -1), w2,
      b2.reshape(N_EXPERTS, 1, -1))


def _combine_kernel(pos_ref, gate_ref, x1_ref, gf_ref, yt_ref, y_ref):
    rows = pos_ref.shape[0]
    slot_iota = lax.broadcasted_iota(jnp.int32, (rows, TILE_SLOTS), 1)
    pos = pos_ref[...]
    gates = gate_ref[...]
    weights = jnp.zeros((rows, TILE_SLOTS), F32)
    for kk in range(TOP_K):
        weights = jnp.where(slot_iota == pos[:, kk:kk + 1], gates[:, kk:kk + 1], weights)
    moe = _dot(weights.astype(BF16), pltpu.bitcast(yt_ref[...], BF16))
    y_ref[...] = _rms(x1_ref[...] + moe, gf_ref[...])


def _combine(pos, gates, x1, g_final, yt, *, tile0, n_tiles):
    row = lambda i: (tile0 + i, 0)
    return pl.pallas_call(
        _combine_kernel,
        grid=(n_tiles,),
        in_specs=[pl.BlockSpec((ROW_TILE, TOP_K), row),
                  pl.BlockSpec((ROW_TILE, TOP_K), row),
                  pl.BlockSpec((ROW_TILE, D_MODEL), row),
                  pl.BlockSpec((1, D_MODEL), lambda i: (0, 0)),
                  pl.BlockSpec((TILE_SLOTS // PACK, D_MODEL), row)],
        out_specs=pl.BlockSpec((ROW_TILE, D_MODEL), lambda i: (i, 0)),
        out_shape=jax.ShapeDtypeStruct((n_tiles * ROW_TILE, D_MODEL), F32),
        compiler_params=_cparams(1),
        name="combine",
    )(pos, gates, x1, g_final.reshape(1, D_MODEL), yt)


def _granule_plan(gran, goff):
    n_tiles = gran.shape[0]
    max_gran = (n_tiles * ROW_TILE * TOP_K + (GRANULE - 1) * n_tiles * N_EXPERTS) // GRANULE
    n_blocks = max_gran // GRAN_PER_BLOCK + N_EXPERTS + 1
    n_slots = n_blocks * GRAN_PER_BLOCK
    i32 = jnp.int32
    per_e = jnp.sum(gran, axis=0)
    region = (per_e + GRAN_PER_BLOCK - 1) // GRAN_PER_BLOCK * GRAN_PER_BLOCK
    region_end = jnp.cumsum(region)
    region_start = region_end - region
    g_t = gran.T
    run_start = (region_start[:, None] + jnp.cumsum(g_t, axis=1) - g_t).reshape(-1)
    run_end = run_start + g_t.reshape(-1)
    first_gran = (jnp.arange(n_tiles, dtype=i32)[None, :] * GRAN_PER_TILE + goff.T).reshape(-1)
    shift = first_gran - run_start
    d_shift = jnp.diff(shift, prepend=0)
    d_end = jnp.diff(run_end, prepend=0)
    slot = jnp.arange(n_slots, dtype=i32)
    started = slot[:, None] >= run_start[None, :]
    granule = slot + jnp.sum(jnp.where(started, d_shift[None, :], 0), axis=1)
    valid = slot < jnp.sum(jnp.where(started, d_end[None, :], 0), axis=1)
    spare = n_tiles * GRAN_PER_TILE
    j = slot % GRAN_PER_BLOCK
    odd_blk = (slot // GRAN_PER_BLOCK) % 2
    n_zero = GRAN_PER_TILE - 2 * GRAN_PER_BLOCK
    assert n_zero > 0
    granule_src = jnp.where(valid, granule, spare + 2 * GRAN_PER_BLOCK + j % n_zero).astype(i32)
    granule_dst = jnp.where(valid, granule, spare + odd_blk * GRAN_PER_BLOCK + j).astype(i32)
    blk = jnp.arange(n_blocks, dtype=i32) * GRAN_PER_BLOCK
    blk_e = jnp.minimum(jnp.sum(blk[:, None] >= region_end[None, :], axis=1), N_EXPERTS - 1).astype(i32)
    e = jnp.arange(N_EXPERTS, dtype=i32)
    later = jnp.where((e[None, :] > e[:, None]) & (per_e[None, :] > 0), e[None, :], N_EXPERTS)
    nxt = jnp.min(later, axis=1)
    nxt = jnp.where(nxt == N_EXPERTS, -1, nxt)
    nxt_e = jnp.sum(jnp.where(blk_e[:, None] == e[None, :], nxt[None, :], 0), axis=1).astype(i32)
    n_used = (region_end[-1:] // GRAN_PER_BLOCK).astype(i32)
    return granule_src, granule_dst, blk_e, nxt_e, n_used


def kernel(x_prompt, x_sample, state_s5_re, state_s5_im, state_hgrn, g_mix, w_in, s5_lambda_re,
           s5_lambda_im, s5_log_dt, s5_b_re, s5_b_im, s5_c_re, s5_c_im, s5_d, s5_w_glu, s5_w_gate,
           hgrn_lower_bound, hgrn_norm_g, hgrn_w_out, w_out, g_ffn, router_w, router_b, moe_w1,
           moe_b1, moe_w2, moe_b2, g_final):
    depth = g_mix.shape[0]
    n_p, l_p, _ = x_prompt.shape
    n_s, l_s, _ = x_sample.shape
    assert depth == 1 and n_p == S5_SEQS and l_p % ROW_TILE == 0
    assert l_s == S5_STEPS and n_s % S5_SEQS == 0 and ROW_TILE % l_s == 0
    li = 0
    t_p, t_s = n_p * l_p, n_s * l_s
    n_groups = n_s // S5_SEQS

    lb = jnp.cumsum(jax.nn.softmax(hgrn_lower_bound.astype(F32), axis=0), axis=0)[li].reshape(1, -1)
    s5_tabs = _s5_tables(s5_lambda_re[li], s5_lambda_im[li], s5_log_dt[li], s5_b_re[li],
                         s5_b_im[li], s5_c_re[li], s5_c_im[li])
    wg = jnp.concatenate([s5_w_glu[li], s5_w_gate[li]], axis=1).astype(BF16)
    rw_pad = jnp.pad(router_w[li].astype(F32), ((0, 0), (0, ROUTER_LANES - N_EXPERTS)))
    rw_hi = rw_pad.astype(BF16)
    rw2 = jnp.stack([rw_hi, (rw_pad - rw_hi.astype(F32)).astype(BF16)])

    xp = x_prompt.reshape(t_p, D_MODEL)
    xs = x_sample.reshape(t_s, D_MODEL)
    u, zh, gab = _inproj(xp, xs, g_mix[li], w_in[li].astype(BF16))

    s0_s5 = jnp.concatenate([state_s5_re[li].reshape(n_groups, S5_SEQS, S5_LANES),
                             state_s5_im[li].reshape(n_groups, S5_SEQS, S5_LANES)], axis=1).astype(F32)
    ya, s5_p, s5_s = _s5(u, s0_s5, s5_tabs, s5_d[li].reshape(1, -1), wg,
                         prompt_len=l_p, n_sample_groups=n_groups)
    yb, hg_p, hg_s = _hgrn(zh, lb, hgrn_norm_g[li].reshape(1, -1), hgrn_w_out[li].astype(BF16),
                           state_hgrn[li].astype(F32), n_prompt=n_p, prompt_len=l_p, sample_len=l_s)

    x1, xg, pos, gates, gran, goff = _merge(
        gab, ya, yb, xp, xs, w_out[li].astype(BF16), g_ffn[li].reshape(1, -1), rw2,
        router_b[li].reshape(-1, 1).astype(F32), prompt_len=l_p)

    plan = _granule_plan(gran.reshape(-1, N_EXPERTS), goff.reshape(-1, N_EXPERTS))
    yt = _experts(xg, plan, moe_w1[li], moe_b1[li], moe_w2[li], moe_b2[li])
    n_pt = t_p // ROW_TILE
    y_p = _combine(pos, gates, x1, g_final, yt, tile0=0, n_tiles=n_pt)
    y_s = _combine(pos, gates, x1, g_final, yt, tile0=n_pt, n_tiles=t_s // ROW_TILE)

    def s5_out(st, n, dtype):
        re = st[:, :S5_SEQS].reshape(1, n, S5_GROUPS, S5_STATE).astype(dtype)
        im = st[:, S5_SEQS:].reshape(1, n, S5_GROUPS, S5_STATE).astype(dtype)
        return re, im

    dt = x_prompt.dtype
    p_re, p_im = s5_out(s5_p, n_p, dt)
    s_re, _ = s5_out(s5_s, n_s, state_s5_re.dtype)
    _, s_im = s5_out(s5_s, n_s, state_s5_im.dtype)
    return (y_p.reshape(n_p, l_p, D_MODEL), y_s.reshape(n_s, l_s, D_MODEL),
            p_re, p_im, hg_p[None].astype(dt),
            s_re, s_im, hg_s[None].astype(state_hgrn.dtype))
```

```python
import functools
import math

import jax
import jax.numpy as jnp
from jax import lax
from jax.experimental import pallas as pl
from jax.experimental.pallas import tpu as pltpu

F32 = jnp.float32
BF16 = jnp.bfloat16

D_MODEL = 1024
NORM_EPS = 1e-6
S5_WIDTH = 512
S5_GROUP = 16
S5_GROUPS = 32
S5_STATE = 64
S5_LANES = S5_GROUPS * S5_STATE
S5_DT_CLIP = -1e-4
HG_WIDTH = 512
HG_HEADS = 4
HG_DK = 128
HG_DIAG = 128
N_EXPERTS = 32
TOP_K = 4
ROUTER_LANES = 128
D_EXPERT = 1024
SWIGLU_ALPHA = 1.702
SWIGLU_LIMIT = 7.0

ROW_TILE = 256
INPROJ_ROWS = 512
SCAN_ROWS = 8
S5_SEQS = 4
S5_STEPS = 64
S5_CH_TILE = 128
S5_LANE_TILE = 512
EXPERT_ROWS = 256
GRANULE = 8
TILE_SLOTS = ROW_TILE * TOP_K + N_EXPERTS * GRANULE
GRAN_PER_TILE = TILE_SLOTS // GRANULE
GRAN_PER_BLOCK = EXPERT_ROWS // GRANULE
PACK = 2
U32 = jnp.uint32
VMEM_LIMIT = 56 * 1024 * 1024


def _cparams(n_axes):
    return pltpu.CompilerParams(dimension_semantics=("arbitrary",) * n_axes,
                                vmem_limit_bytes=VMEM_LIMIT)


def _rms(x, g):
    r = lax.rsqrt(jnp.mean(x * x, axis=-1, keepdims=True) + NORM_EPS)
    return x * r * g


def _dot(a, b):
    return jnp.dot(a, b, preferred_element_type=F32)


def _dot_nt(a, b):
    return lax.dot_general(a, b, (((1,), (1,)), ((), ())), preferred_element_type=F32)


def _dot_tn(a, b):
    return lax.dot_general(a, b, (((0,), (0,)), ((), ())), preferred_element_type=F32)


def _two_stream_specs(n_first, rows=ROW_TILE):
    return (pl.BlockSpec((rows, D_MODEL), lambda i: (jnp.minimum(i, n_first - 1), 0)),
            pl.BlockSpec((rows, D_MODEL), lambda i: (jnp.maximum(i - n_first, 0), 0)))


def _inproj_kernel(xp_ref, xs_ref, g_ref, w_ref, u_ref, zh_ref, gab_ref, *, n_first):
    x = jnp.where(pl.program_id(0) < n_first, xp_ref[...], xs_ref[...])
    h = _rms(x, g_ref[...]).astype(BF16)
    u_ref[...] = _dot(h, w_ref[:, :S5_WIDTH])
    zh_ref[...] = _dot(h, w_ref[:, S5_WIDTH:S5_WIDTH + 4 * HG_WIDTH])
    gab_ref[...] = _dot(h, w_ref[:, S5_WIDTH + 4 * HG_WIDTH:]).astype(BF16)


def _inproj(xp, xs, g_mix, w_in_bf):
    assert xp.shape[0] % INPROJ_ROWS == 0 and xs.shape[0] % INPROJ_ROWS == 0
    n_first = xp.shape[0] // INPROJ_ROWS
    t = xp.shape[0] + xs.shape[0]
    ncols = w_in_bf.shape[1]
    row = lambda i: (i, 0)
    fixed = lambda i: (0, 0)
    return pl.pallas_call(
        functools.partial(_inproj_kernel, n_first=n_first),
        grid=(t // INPROJ_ROWS,),
        in_specs=[*_two_stream_specs(n_first, INPROJ_ROWS),
                  pl.BlockSpec((1, D_MODEL), fixed),
                  pl.BlockSpec((D_MODEL, ncols), fixed)],
        out_specs=[pl.BlockSpec((INPROJ_ROWS, S5_WIDTH), row),
                   pl.BlockSpec((INPROJ_ROWS, 4 * HG_WIDTH), row),
                   pl.BlockSpec((INPROJ_ROWS, 2 * D_MODEL), row)],
        out_shape=[jax.ShapeDtypeStruct((t, S5_WIDTH), F32),
                   jax.ShapeDtypeStruct((t, 4 * HG_WIDTH), F32),
                   jax.ShapeDtypeStruct((t, 2 * D_MODEL), BF16)],
        compiler_params=_cparams(1),
        name="inproj",
    )(xp, xs, g_mix.reshape(1, D_MODEL), w_in_bf)


def _s5_kernel(u0_ref, u1_ref, u2_ref, u3_ref, s0_ref, perm_ref, pinv_ref, wb_ref, wc_ref,
               coef_ref, d_ref, wg_ref, ya_ref, sp_out, ss_out, buf0, buf1, ub0, ub1, v_ref,
               *, n_prompt_steps, n_steps):
    i = pl.program_id(0)
    ia = jnp.minimum(i, n_steps - 1)

    @pl.when(i == 0)
    def _():
        v_ref[...] = jnp.zeros_like(v_ref)
        buf1[...] = jnp.zeros_like(buf1)
        ub1[...] = jnp.zeros_like(ub1)

    @pl.when(ia >= n_prompt_steps)
    def _():
        v_ref[...] = s0_ref[0]

    u_refs = (u0_ref, u1_ref, u2_ref, u3_ref)
    consts = (perm_ref, pinv_ref, wb_ref, wc_ref, coef_ref, d_ref, wg_ref)

    @pl.when(i % 2 == 0)
    def _():
        _s5_body(u_refs, consts, ya_ref, v_ref, cur=(buf0, ub0), prev=(buf1, ub1))

    @pl.when(i % 2 == 1)
    def _():
        _s5_body(u_refs, consts, ya_ref, v_ref, cur=(buf1, ub1), prev=(buf0, ub0))

    @pl.when(ia < n_prompt_steps)
    def _():
        sp_out[0] = v_ref[...]

    @pl.when(ia >= n_prompt_steps)
    def _():
        ss_out[0] = v_ref[...]


def _s5_body(u_refs, consts, ya_ref, v_ref, *, cur, prev):
    perm_ref, pinv_ref, wb_ref, wc_ref, coef_ref, d_ref, wg_ref = consts
    buf_a, ub_a = cur
    buf_b, ub_b = prev
    n_tiles = S5_WIDTH // S5_CH_TILE

    u = jnp.concatenate([r[0] for r in u_refs], axis=0)
    ub_a[...] = u
    rows = 2 * u.shape[0]
    up = _dot(perm_ref[...], u.astype(BF16))
    re_half = (lax.broadcasted_iota(jnp.int32, (rows, S5_CH_TILE), 0) & S5_SEQS) == 0
    ws = []
    for n in range(n_tiles):
        ls = slice(n * S5_LANE_TILE, (n + 1) * S5_LANE_TILE)
        un = up[:, n * S5_CH_TILE:(n + 1) * S5_CH_TILE]
        lhs = jnp.concatenate([jnp.where(re_half, un, 0.0), jnp.where(re_half, 0.0, un)], axis=1)
        buf_a[:, ls] = _dot(lhs.astype(BF16), wb_ref[n])
        z = _dot(buf_b[:, ls].astype(BF16), wc_ref[n])
        ws.append(jnp.where(re_half, z[:, :S5_CH_TILE], z[:, S5_CH_TILE:]))
    w_hi, w_lo = _split2(jnp.concatenate(ws, axis=1))

    def scan(v, t0, t1):
        for t in range(t0, t1):
            rs = slice(t * SCAN_ROWS, (t + 1) * SCAN_ROWS)
            v = coef_ref[0] * v + coef_ref[1] * pltpu.roll(v, S5_SEQS, axis=0) + buf_a[rs, :]
            buf_a[rs, :] = v
        return v

    quarter = S5_STEPS // 4
    v = scan(v_ref[...], 0, quarter)
    y = _dot(pinv_ref[...], w_hi) + _dot(pinv_ref[...], w_lo) + d_ref[...] * ub_b[...]
    y = jax.nn.gelu(y).astype(BF16)
    v = scan(v, quarter, 2 * quarter)
    p_glu = _dot(y, wg_ref[:, :D_MODEL])
    v = scan(v, 2 * quarter, 3 * quarter)
    p_gate = _dot(y, wg_ref[:, D_MODEL:])
    v = scan(v, 3 * quarter, S5_STEPS)
    ya_ref[0] = (p_glu * jax.nn.sigmoid(p_gate)).astype(BF16).reshape(S5_SEQS, S5_STEPS, D_MODEL)
    v_ref[...] = v


def _s5(u, s0_sample, tables, d_skip, wg, *, prompt_len, n_sample_groups):
    perm, pinv, wb, wc, coef = tables
    n_prompt_steps = prompt_len // S5_STEPS
    n_steps = n_prompt_steps + n_sample_groups
    sample_base = S5_SEQS * n_prompt_steps
    u3 = u.reshape(-1, S5_STEPS, S5_WIDTH)

    def u_spec(s):
        def imap(i):
            i = jnp.minimum(i, n_steps - 1)
            prompt_blk = s * n_prompt_steps + i
            sample_blk = sample_base + S5_SEQS * (i - n_prompt_steps) + s
            return (jnp.where(i < n_prompt_steps, prompt_blk, sample_blk), 0, 0)
        return pl.BlockSpec((1, S5_STEPS, S5_WIDTH), imap)

    sample_grp = lambda i: (jnp.clip(i - n_prompt_steps, 0, n_sample_groups - 1), 0, 0)
    whole = lambda a: pl.BlockSpec(a.shape, lambda i: (0,) * a.ndim)
    state_blk = (1, 2 * S5_SEQS, S5_LANES)
    rows = S5_SEQS * S5_STEPS
    return pl.pallas_call(
        functools.partial(_s5_kernel, n_prompt_steps=n_prompt_steps, n_steps=n_steps),
        grid=(n_steps + 1,),
        in_specs=[u_spec(0), u_spec(1), u_spec(2), u_spec(3),
                  pl.BlockSpec(state_blk, sample_grp),
                  whole(perm), whole(pinv), whole(wb), whole(wc), whole(coef), whole(d_skip), whole(wg)],
        out_specs=[pl.BlockSpec((1, S5_SEQS, S5_STEPS, D_MODEL),
                                lambda i: (jnp.maximum(i - 1, 0), 0, 0, 0)),
                   pl.BlockSpec(state_blk, lambda i: (0, 0, 0)),
                   pl.BlockSpec(state_blk, sample_grp)],
        out_shape=[jax.ShapeDtypeStruct((n_steps, S5_SEQS, S5_STEPS, D_MODEL), BF16),
                   jax.ShapeDtypeStruct(state_blk, F32),
                   jax.ShapeDtypeStruct((n_sample_groups,) + state_blk[1:], F32)],
        scratch_shapes=[pltpu.VMEM((2 * rows, S5_LANES), F32),
                        pltpu.VMEM((2 * rows, S5_LANES), F32),
                        pltpu.VMEM((rows, S5_WIDTH), F32),
                        pltpu.VMEM((rows, S5_WIDTH), F32),
                        pltpu.VMEM((2 * S5_SEQS, S5_LANES), F32)],
        compiler_params=_cparams(1),
        name="s5",
    )(u3, u3, u3, u3, s0_sample, perm, pinv, wb, wc, coef, d_skip, wg)


def _s5_tables(lam_re, lam_im, log_dt, b_re, b_im, c_re, c_im):
    lr = jnp.minimum(lam_re.astype(F32), S5_DT_CLIP)
    li = lam_im.astype(F32)
    dt = jnp.exp(log_dt.astype(F32))[:, None]
    mag = jnp.exp(lr * dt)
    ar = mag * jnp.cos(li * dt)
    ai = mag * jnp.sin(li * dt)
    den = lr * lr + li * li
    zr = ((ar - 1.0) * lr + ai * li) / den
    zi = (ai * lr - (ar - 1.0) * li) / den
    br = b_re.astype(F32)
    bi = b_im.astype(F32)
    bbr = zr[..., None] * br - zi[..., None] * bi
    bbi = zr[..., None] * bi + zi[..., None] * br

    ar_f, ai_f = ar.reshape(1, -1), ai.reshape(1, -1)
    sign = jnp.where(jnp.arange(2 * S5_SEQS)[:, None] < S5_SEQS, -1.0, 1.0)
    coef = jnp.stack([jnp.broadcast_to(ar_f, (2 * S5_SEQS, S5_LANES)), sign * ai_f]).astype(F32)

    g_per_tile = S5_CH_TILE // S5_GROUP
    n_tiles = S5_GROUPS // g_per_tile
    eye = jnp.eye(g_per_tile, dtype=F32)

    def b_tile(bb):
        x = bb.reshape(n_tiles, g_per_tile, S5_STATE, S5_GROUP)
        x = x[:, :, :, :, None] * eye[None, :, None, None, :]
        return x.transpose(0, 1, 3, 4, 2).reshape(n_tiles, S5_CH_TILE, S5_LANE_TILE)

    wb = jnp.concatenate([b_tile(bbr), b_tile(bbi)], axis=1).astype(BF16)

    def c_tile(cc):
        x = cc.astype(F32).reshape(n_tiles, g_per_tile, S5_GROUP, S5_STATE)
        x = x[:, :, :, :, None] * eye[None, :, None, None, :]
        return x.transpose(0, 1, 3, 4, 2).reshape(n_tiles, S5_LANE_TILE, S5_CH_TILE)

    wc = jnp.concatenate([c_tile(c_re), -c_tile(c_im)], axis=2).astype(BF16)

    r = jnp.arange(2 * S5_SEQS * S5_STEPS)
    src = (r % S5_SEQS) * S5_STEPS + r // (2 * S5_SEQS)
    perm = (src[:, None] == jnp.arange(S5_SEQS * S5_STEPS)[None, :]).astype(BF16)
    return perm, perm.T, wb, wc, coef


def _split2(x):
    hi = x.astype(BF16)
    lo = (x - hi.astype(F32)).astype(BF16)
    return hi, lo


def _split3(x):
    hi = x.astype(BF16)
    r = x - hi.astype(F32)
    mid = r.astype(BF16)
    lo = (r - mid.astype(F32)).astype(BF16)
    return hi, mid, lo


def _hgrn_kernel(zh_ref, lb_ref, gn_ref, wo_ref, s0_ref, yb_ref, sp_out, ss_out,
                 st, gc_ref, att_ref, cross_ref, o_ref, *, n_prompt_tiles, chunks_per_seq,
                 sample_len):
    i = pl.program_id(0)
    refs = (zh_ref, lb_ref, gn_ref, wo_ref, yb_ref, st, gc_ref, att_ref, cross_ref, o_ref)

    @pl.when(i < n_prompt_tiles)
    def _():
        j = i % chunks_per_seq

        @pl.when(j == 0)
        def _():
            st[0] = jnp.zeros((HG_HEADS, HG_DK, HG_DK), F32)

        _hgrn_tile(*refs, n_seg=1, seg_len=ROW_TILE)

        @pl.when(j == chunks_per_seq - 1)
        def _():
            for h in range(HG_HEADS):
                sp_out[0, h] = st[0, h].T

    @pl.when(i >= n_prompt_tiles)
    def _():
        n_seg = ROW_TILE // sample_len
        for seg in range(n_seg):
            for h in range(HG_HEADS):
                st[seg, h] = s0_ref[seg, h].T
        _hgrn_tile(*refs, n_seg=n_seg, seg_len=sample_len)
        for seg in range(n_seg):
            for h in range(HG_HEADS):
                ss_out[seg, h] = st[seg, h].T


def _hgrn_tile(zh_ref, lb_ref, gn_ref, wo_ref, yb_ref, st, gc_ref, att_ref, cross_ref, o_ref,
               *, n_seg, seg_len):
    rows = n_seg * seg_len
    zq = zh_ref[:, 0:HG_WIDTH]
    zf = zh_ref[:, HG_WIDTH:2 * HG_WIDTH]
    v = zh_ref[:, 2 * HG_WIDTH:3 * HG_WIDTH]
    lb = lb_ref[...]
    q = zq * jax.nn.sigmoid(zq)
    f = lb + (1.0 - lb) * jax.nn.sigmoid(zf)
    k = 1.0 - f
    g = jnp.log(f)
    vb = v.astype(BF16)

    ri = lax.broadcasted_iota(jnp.int32, (rows, rows), 0)
    ci = lax.broadcasted_iota(jnp.int32, (rows, rows), 1)
    seg_shift = int(math.log2(seg_len))
    same_seg = (ri >> seg_shift) == (ci >> seg_shift)

    tri = jnp.where(same_seg & (ci <= ri), 1.0, 0.0).astype(BF16)
    g3 = jnp.concatenate(_split3(g), axis=1)
    gcs = _dot(tri, g3)
    gc = gcs[:, :HG_WIDTH] + gcs[:, HG_WIDTH:2 * HG_WIDTH] + gcs[:, 2 * HG_WIDTH:]
    gc_ref[...] = gc

    n_diag = rows // HG_DIAG
    dr = lax.broadcasted_iota(jnp.int32, (HG_DIAG, HG_DIAG), 0)
    dc = lax.broadcasted_iota(jnp.int32, (HG_DIAG, HG_DIAG), 1)
    dxor = jnp.where(dc < dr, dr ^ dc, 0)
    row_id = lax.broadcasted_iota(jnp.int32, (rows, HG_WIDTH), 0)
    qb = q.astype(BF16)
    kb = k.astype(BF16)
    for h in range(HG_HEADS):
        hs = slice(h * HG_DK, (h + 1) * HG_DK)
        for d in range(n_diag):
            ds_ = slice(d * HG_DIAG, (d + 1) * HG_DIAG)
            att_ref[h, d] = jnp.where(dr == dc, _dot_nt(qb[ds_, hs], kb[ds_, hs]), 0.0)

    g2 = jnp.concatenate(_split2(g), axis=1)
    level = seg_len // 2
    while level >= 1:
        if 2 * level >= SCAN_ROWS:
            mids, picks = [], []
            for b in range(rows // (2 * level)):
                lo = b * 2 * level
                mids.append(jnp.broadcast_to(gc_ref[lo + level - 1:lo + level, :], (2 * level, HG_WIDTH)))
                picks += [k[lo:lo + level], q[lo + level:lo + 2 * level]]
            gmid = mids[0] if len(mids) == 1 else jnp.concatenate(mids, axis=0)
            qk = jnp.concatenate(picks, axis=0)
            dexp = -jnp.abs(gc - gmid)
        else:
            mid = (ri & ~(2 * level - 1)) + (level - 1)
            upper = (ri & level) != 0
            w = (ci > jnp.where(upper, mid, ri)) & (ci <= jnp.where(upper, ri, mid))
            dsum = _dot(jnp.where(w, 1.0, 0.0).astype(BF16), g2)
            dexp = dsum[:, :HG_WIDTH] + dsum[:, HG_WIDTH:]
            qk = jnp.where((row_id & level) != 0, q, k)
        m = (qk * jnp.exp(dexp)).astype(BF16)
        if level >= HG_DIAG:
            for h in range(HG_HEADS):
                hs = slice(h * HG_DK, (h + 1) * HG_DK)
                cross_ref[h] = _dot_nt(m[level:, hs], m[:level, hs])
        else:
            sel = (dxor >> int(math.log2(level))) == 1
            for h in range(HG_HEADS):
                hs = slice(h * HG_DK, (h + 1) * HG_DK)
                for d in range(n_diag):
                    ds_ = slice(d * HG_DIAG, (d + 1) * HG_DIAG)
                    att_ref[h, d] = jnp.where(sel, _dot_nt(m[ds_, hs], m[ds_, hs]), att_ref[h, d])
        level //= 2

    qg = (q * jnp.exp(gc)).astype(BF16)
    for h in range(HG_HEADS):
        hs = slice(h * HG_DK, (h + 1) * HG_DK)
        for d in range(n_diag):
            ds_ = slice(d * HG_DIAG, (d + 1) * HG_DIAG)
            o_ref[ds_, hs] = _dot(att_ref[h, d].astype(BF16), vb[ds_, hs])
        if seg_len > HG_DIAG:
            o_ref[HG_DIAG:, hs] += _dot(cross_ref[h].astype(BF16), vb[:HG_DIAG, hs])

    for seg in range(n_seg):
        rs = slice(seg * seg_len, (seg + 1) * seg_len)
        last = (seg + 1) * seg_len - 1
        g_last = gc_ref[last:last + 1, :]
        kd = (k[rs] * jnp.exp(g_last - gc[rs])).astype(BF16)
        decay = jnp.exp(g_last)
        for h in range(HG_HEADS):
            hs = slice(h * HG_DK, (h + 1) * HG_DK)
            s_t = st[seg, h]
            o_ref[rs, hs] += _dot_nt(qg[rs, hs], s_t.astype(BF16))
            st[seg, h] = s_t * decay[:, hs] + _dot_tn(vb[rs, hs], kd[:, hs])

    zg = zh_ref[:, 3 * HG_WIDTH:]
    gate = zg * jax.nn.sigmoid(zg)
    outs = []
    for h in range(HG_HEADS):
        hs = slice(h * HG_DK, (h + 1) * HG_DK)
        outs.append(_rms(o_ref[:, hs], gn_ref[...]) * gate[:, hs])
    yb_ref[...] = _dot(jnp.concatenate(outs, axis=1).astype(BF16), wo_ref[...]).astype(BF16)


def _hgrn(zh, lb, gn, wo, s0_sample, *, n_prompt, prompt_len, sample_len):
    t = zh.shape[0]
    chunks_per_seq = prompt_len // ROW_TILE
    n_prompt_tiles = n_prompt * chunks_per_seq
    n_seg = ROW_TILE // sample_len
    n_sample = s0_sample.shape[0]
    row = lambda i: (i, 0)
    fixed = lambda i: (0, 0)
    state = (HG_HEADS, HG_DK, HG_DK)
    prompt_seq = lambda i: (jnp.minimum(i // chunks_per_seq, n_prompt - 1), 0, 0, 0)
    sample_grp = lambda i: (jnp.maximum(i - n_prompt_tiles, 0), 0, 0, 0)
    kern = functools.partial(_hgrn_kernel, n_prompt_tiles=n_prompt_tiles,
                             chunks_per_seq=chunks_per_seq, sample_len=sample_len)
    return pl.pallas_call(
        kern,
        grid=(t // ROW_TILE,),
        in_specs=[pl.BlockSpec((ROW_TILE, 4 * HG_WIDTH), row),
                  pl.BlockSpec((1, HG_WIDTH), fixed),
                  pl.BlockSpec((1, HG_DK), fixed),
                  pl.BlockSpec(wo.shape, fixed),
                  pl.BlockSpec((n_seg,) + state, sample_grp)],
        out_specs=[pl.BlockSpec((ROW_TILE, D_MODEL), row),
                   pl.BlockSpec((1,) + state, prompt_seq),
                   pl.BlockSpec((n_seg,) + state, sample_grp)],
        out_shape=[jax.ShapeDtypeStruct((t, D_MODEL), BF16),
                   jax.ShapeDtypeStruct((n_prompt,) + state, F32),
                   jax.ShapeDtypeStruct((n_sample,) + state, F32)],
        scratch_shapes=[pltpu.VMEM((n_seg,) + state, F32),
                        pltpu.VMEM((ROW_TILE, HG_WIDTH), F32),
                        pltpu.VMEM((HG_HEADS, ROW_TILE // HG_DIAG, HG_DIAG, HG_DIAG), F32),
                        pltpu.VMEM((HG_HEADS, HG_DIAG, HG_DIAG), F32),
                        pltpu.VMEM((ROW_TILE, HG_WIDTH), F32)],
        compiler_params=_cparams(1),
        name="hgrn2",
    )(zh, lb, gn, wo, s0_sample)


def _merge_kernel(gab_ref, ya0_ref, ya1_ref, ya2_ref, ya3_ref, yb_ref, xp_ref, xs_ref, wo_ref,
                  gf_ref, rw_ref, rb_ref, x1_ref, xg_ref, pos_ref, gate_ref, cnt_ref, off_ref,
                  hbuf, lbuf, *, n_first, n_tiles):
    i = pl.program_id(0)
    cur = i % 2
    prev = 1 - cur

    @pl.when(i == 0)
    def _():
        hbuf[1] = jnp.zeros(hbuf.shape[1:], hbuf.dtype)
        lbuf[1] = jnp.zeros(lbuf.shape[1:], lbuf.dtype)

    logits_tok = lbuf[prev]

    ga = gab_ref[:, :D_MODEL].astype(F32)
    gb = gab_ref[:, D_MODEL:].astype(F32)
    ya = jnp.concatenate([r[0] for r in (ya0_ref, ya1_ref, ya2_ref, ya3_ref)], axis=0).astype(F32)
    mixed = (jax.nn.sigmoid(ga) * ya + jax.nn.sigmoid(gb) * yb_ref[...].astype(F32)).astype(BF16)

    rows = logits_tok.shape[0]
    logits = logits_tok.T[:N_EXPERTS] + rb_ref[...]
    eid = lax.broadcasted_iota(jnp.int32, (N_EXPERTS, rows), 0).astype(F32)
    lg = logits
    sel = jnp.zeros((N_EXPERTS, rows), F32)
    vals, idxs = [], []
    for _ in range(TOP_K):
        m = jnp.max(lg, axis=0, keepdims=True)
        idx = jnp.min(jnp.where(lg == m, eid, float(N_EXPERTS)), axis=0, keepdims=True)
        hit = eid == idx
        vals.append(m)
        idxs.append(idx)
        lg = jnp.where(hit, -jnp.inf, lg)
        sel = jnp.where(hit, 1.0, sel)

    es = [jnp.exp(vv - vals[0]) for vv in vals]
    tot = es[0] + es[1] + es[2] + es[3]

    ri = lax.broadcasted_iota(jnp.int32, (rows, rows), 0)
    ci = lax.broadcasted_iota(jnp.int32, (rows, rows), 1)
    earlier = jnp.where(ri < ci, 1.0, 0.0).astype(BF16)
    rank = _dot(sel.astype(BF16), earlier)
    gran = jnp.floor((jnp.sum(sel, axis=1, keepdims=True) + (GRANULE - 1)) * (1.0 / GRANULE))
    ei = lax.broadcasted_iota(jnp.int32, (N_EXPERTS, N_EXPERTS), 0)
    ej = lax.broadcasted_iota(jnp.int32, (N_EXPERTS, N_EXPERTS), 1)
    gran_b = jnp.broadcast_to(gran, (N_EXPERTS, rows))
    off = _dot(jnp.where(ej < ei, 1.0, 0.0).astype(BF16), gran_b.astype(BF16))
    slot_of = rank + off * float(GRANULE)
    poss = [jnp.sum(jnp.where(eid == idx, slot_of, 0.0), axis=0, keepdims=True) for idx in idxs]

    per_tok = jnp.concatenate(poss + [e / tot for e in es], axis=0).T

    x = jnp.where(jnp.minimum(i, n_tiles - 1) < n_first, xp_ref[...], xs_ref[...])
    x1 = x + _dot(mixed, wo_ref[...])
    x1_ref[...] = x1

    slot_iota = lax.broadcasted_iota(jnp.int32, (rows, TILE_SLOTS), 1).astype(F32)
    onehot = jnp.zeros((rows, TILE_SLOTS), F32)
    for kk in range(TOP_K):
        onehot = jnp.where(slot_iota == per_tok[:, kk:kk + 1], 1.0, onehot)
    sorted_rows = _dot_tn(onehot.astype(BF16), hbuf[prev])
    xg_ref[...] = pltpu.bitcast(jnp.where(i > n_tiles, 0.0, sorted_rows).astype(BF16), U32)

    a_hi, a_lo = _split2(_rms(x1, gf_ref[...]))
    hbuf[cur] = a_hi
    lbuf[cur] = _dot(a_hi, rw_ref[0]) + _dot(a_hi, rw_ref[1]) + _dot(a_lo, rw_ref[0])

    pos_ref[...] = per_tok[:, :TOP_K].astype(jnp.int32)
    gate_ref[...] = per_tok[:, TOP_K:]
    cnt_ref[0] = gran.astype(jnp.int32)
    off_ref[0] = off[:, :1].astype(jnp.int32)


def _merge(gab, ya_steps, yb, xp, xs, wo, g_ffn, rw2, rb, *, prompt_len):
    n_first = xp.shape[0] // ROW_TILE
    t = xp.shape[0] + xs.shape[0]
    n_tiles = t // ROW_TILE
    chunks_per_seq = prompt_len // ROW_TILE
    quarters = ROW_TILE // S5_STEPS
    ya3 = ya_steps.reshape(-1, S5_STEPS, D_MODEL)

    def ya_spec(q):
        def imap(i):
            i = jnp.minimum(i, n_tiles - 1)
            seq, chunk = i // chunks_per_seq, i % chunks_per_seq
            prompt_blk = (chunk * quarters + q) * S5_SEQS + seq
            return (jnp.where(i < n_first, prompt_blk, i * quarters + q), 0, 0)
        return pl.BlockSpec((1, S5_STEPS, D_MODEL), imap)

    tile = lambda i: jnp.minimum(i, n_tiles - 1)
    prev_tile = lambda i: jnp.clip(i - 1, 0, n_tiles - 1)
    row = lambda i: (tile(i), 0)
    prev_row = lambda i: (prev_tile(i), 0)
    fixed = lambda i: (0, 0)
    per_prev_tile = lambda i: (prev_tile(i), 0, 0)
    xp_spec, xs_spec = _two_stream_specs(n_first)
    return pl.pallas_call(
        functools.partial(_merge_kernel, n_first=n_first, n_tiles=n_tiles),
        grid=(n_tiles + 2,),
        in_specs=[pl.BlockSpec((ROW_TILE, 2 * D_MODEL), row),
                  ya_spec(0), ya_spec(1), ya_spec(2), ya_spec(3),
                  pl.BlockSpec((ROW_TILE, D_MODEL), row),
                  pl.BlockSpec(xp_spec.block_shape, lambda i: xp_spec.index_map(tile(i))),
                  pl.BlockSpec(xs_spec.block_shape, lambda i: xs_spec.index_map(tile(i))),
                  pl.BlockSpec((D_MODEL, D_MODEL), fixed),
                  pl.BlockSpec((1, D_MODEL), fixed),
                  pl.BlockSpec((2, D_MODEL, ROUTER_LANES), lambda i: (0, 0, 0)),
                  pl.BlockSpec((N_EXPERTS, 1), fixed)],
        out_specs=[pl.BlockSpec((ROW_TILE, D_MODEL), row),
                   pl.BlockSpec((TILE_SLOTS // PACK, D_MODEL), lambda i: (jnp.maximum(i - 1, 0), 0)),
                   pl.BlockSpec((ROW_TILE, TOP_K), prev_row),
                   pl.BlockSpec((ROW_TILE, TOP_K), prev_row),
                   pl.BlockSpec((1, N_EXPERTS, 1), per_prev_tile),
                   pl.BlockSpec((1, N_EXPERTS, 1), per_prev_tile)],
        scratch_shapes=[pltpu.VMEM((2, ROW_TILE, D_MODEL), BF16),
                        pltpu.VMEM((2, ROW_TILE, ROUTER_LANES), F32)],
        out_shape=[jax.ShapeDtypeStruct((t, D_MODEL), F32),
                   jax.ShapeDtypeStruct(((n_tiles + 1) * TILE_SLOTS // PACK, D_MODEL), U32),
                   jax.ShapeDtypeStruct((t, TOP_K), jnp.int32),
                   jax.ShapeDtypeStruct((t, TOP_K), F32),
                   jax.ShapeDtypeStruct((n_tiles, N_EXPERTS, 1), jnp.int32),
                   jax.ShapeDtypeStruct((n_tiles, N_EXPERTS, 1), jnp.int32)],
        compiler_params=_cparams(1),
        name="merge_router",
    )(gab, ya3, ya3, ya3, ya3, yb, xp, xs, wo, g_ffn, rw2, rb)


def _expert_kernel(blk_e, nxt_e, n_used, src0_ref, srcn_ref, dst_ref, xg_hbm, w1_hbm, b1_ref, w2_hbm,
                   b2_ref, yt_hbm, xbuf, obuf, gsem, ssem, w1s, w2s, wsem, w1b, w2b):
    b = pl.program_id(0)
    nu = n_used[0]

    gran_rows = GRANULE // PACK
    blk_rows = EXPERT_ROWS // PACK

    def granule_rows(g):
        return pl.ds(pl.multiple_of(g * gran_rows, gran_rows), gran_rows)

    def block_rows(j):
        return pl.ds(j * gran_rows, gran_rows)

    def start_gather(src_ref, slot):
        for j in range(GRAN_PER_BLOCK):
            pltpu.make_async_copy(xg_hbm.at[granule_rows(src_ref[0, 0, j])],
                                  xbuf.at[slot, block_rows(j)], gsem.at[slot]).start()

    def start_scatter(slot):
        for j in range(GRAN_PER_BLOCK):
            pltpu.make_async_copy(obuf.at[slot, block_rows(j)],
                                  yt_hbm.at[granule_rows(dst_ref[0, 0, j])], ssem.at[slot]).start()

    def wait_gather(slot):
        pltpu.make_async_copy(xg_hbm.at[pl.ds(0, blk_rows)], xbuf.at[slot], gsem.at[slot]).wait()

    def wait_scatter(slot):
        pltpu.make_async_copy(obuf.at[slot], yt_hbm.at[pl.ds(0, blk_rows)], ssem.at[slot]).wait()

    def weight_copies(e):
        return (pltpu.make_async_copy(w1_hbm.at[e], w1s, wsem.at[0]),
                pltpu.make_async_copy(w2_hbm.at[e], w2s, wsem.at[1]))

    first = b == 0
    new_expert = first | (blk_e[b] != blk_e[jnp.maximum(b - 1, 0)])

    @pl.when(first)
    def _():
        start_gather(src0_ref, 0)
        for c in weight_copies(blk_e[0]):
            c.start(priority=1)

    @pl.when((b >= 2) & (b <= nu))
    def _():
        wait_scatter(b % 2)

    @pl.when(b == nu)
    def _():
        wait_gather(b % 2)
        wait_scatter((b - 1) % 2)

    @pl.when(b < nu)
    def _():
        slot = b % 2
        wait_gather(slot)

        @pl.when(new_expert)
        def _():
            for c in weight_copies(blk_e[b]):
                c.wait()
            w1b[...] = w1s[...].astype(BF16)
            w2b[...] = w2s[...].astype(BF16)

            @pl.when(nxt_e[b] >= 0)
            def _():
                for c in weight_copies(nxt_e[b]):
                    c.start(priority=1)

        start_gather(srcn_ref, 1 - slot)
        x = pltpu.bitcast(xbuf[slot], BF16)
        u = _dot(x, w1b[...]) + b1_ref[0]
        a = jnp.minimum(u[:, :D_EXPERT], SWIGLU_LIMIT)
        lin = jnp.clip(u[:, D_EXPERT:], -SWIGLU_LIMIT, SWIGLU_LIMIT)
        hmid = a * jax.nn.sigmoid(SWIGLU_ALPHA * a) * (lin + 1.0)
        y = _dot(hmid.astype(BF16), w2b[...]) + b2_ref[0]
        obuf[slot] = pltpu.bitcast(y.astype(BF16), U32)
        start_scatter(slot)


def _experts(xg, plan, w1, b1, w2, b2):
    granule_src, granule_dst, blk_e, nxt_e, n_used = plan
    n_blocks = blk_e.shape[0]
    src3 = granule_src.reshape(n_blocks, 1, GRAN_PER_BLOCK)
    dst3 = granule_dst.reshape(n_blocks, 1, GRAN_PER_BLOCK)
    smem_blk = lambda imap: pl.BlockSpec((1, 1, GRAN_PER_BLOCK), imap, memory_space=pltpu.SMEM)
    live = lambda b, nu: jnp.minimum(b, nu[0])
    by_expert = lambda b, be, nx, nu: (be[live(b, nu)], 0, 0)
    grid_spec = pltpu.PrefetchScalarGridSpec(
        num_scalar_prefetch=3,
        grid=(n_blocks,),
        in_specs=[smem_blk(lambda b, be, nx, nu: (0, 0, 0)),
                  smem_blk(lambda b, be, nx, nu: (jnp.minimum(live(b, nu) + 1, n_blocks - 1), 0, 0)),
                  smem_blk(lambda b, be, nx, nu: (live(b, nu), 0, 0)),
                  pl.BlockSpec(memory_space=pl.ANY),
                  pl.BlockSpec(memory_space=pl.ANY),
                  pl.BlockSpec((1, 1, 2 * D_EXPERT), by_expert),
                  pl.BlockSpec(memory_space=pl.ANY),
                  pl.BlockSpec((1, 1, D_MODEL), by_expert)],
        out_specs=pl.BlockSpec(memory_space=pl.ANY),
        scratch_shapes=[pltpu.VMEM((2, EXPERT_ROWS // PACK, D_MODEL), U32),
                        pltpu.VMEM((2, EXPERT_ROWS // PACK, D_MODEL), U32),
                        pltpu.SemaphoreType.DMA((2,)),
                        pltpu.SemaphoreType.DMA((2,)),
                        pltpu.VMEM((D_MODEL, 2 * D_EXPERT), F32),
                        pltpu.VMEM((D_EXPERT, D_MODEL), F32),
                        pltpu.SemaphoreType.DMA((2,)),
                        pltpu.VMEM((D_MODEL, 2 * D_EXPERT), BF16),
                        pltpu.VMEM((D_EXPERT, D_MODEL), BF16)],
    )
    return pl.pallas_call(
        _expert_kernel,
        grid_spec=grid_spec,
        out_shape=jax.ShapeDtypeStruct(xg.shape, xg.dtype),
        input_output_aliases={6: 0},
        compiler_params=_cparams(1),
        name="experts",
    )(blk_e, nxt_e, n_used, src3, src3, dst3, xg, w1, b1.reshape(N_EXPERTS, 1, -1), w2,
      b2.reshape(N_EXPERTS, 1, -1))


def _combine_kernel(pos_ref, gate_ref, x1_ref, gf_ref, yt_ref, y_ref):
    rows = pos_ref.shape[0]
    slot_iota = lax.broadcasted_iota(jnp.int32, (rows, TILE_SLOTS), 1)
    pos = pos_ref[...]
    gates = gate_ref[...]
    weights = jnp.zeros((rows, TILE_SLOTS), F32)
    for kk in range(TOP_K):
        weights = jnp.where(slot_iota == pos[:, kk:kk + 1], gates[:, kk:kk + 1], weights)
    moe = _dot(weights.astype(BF16), pltpu.bitcast(yt_ref[...], BF16))
    y_ref[...] = _rms(x1_ref[...] + moe, gf_ref[...])


def _combine(pos, gates, x1, g_final, yt, *, tile0, n_tiles):
    row = lambda i: (tile0 + i, 0)
    return pl.pallas_call(
        _combine_kernel,
        grid=(n_tiles,),
        in_specs=[pl.BlockSpec((ROW_TILE, TOP_K), row),
                  pl.BlockSpec((ROW_TILE, TOP_K), row),
                  pl.BlockSpec((ROW_TILE, D_MODEL), row),
                  pl.BlockSpec((1, D_MODEL), lambda i: (0, 0)),
                  pl.BlockSpec((TILE_SLOTS // PACK, D_MODEL), row)],
        out_specs=pl.BlockSpec((ROW_TILE, D_MODEL), lambda i: (i, 0)),
        out_shape=jax.ShapeDtypeStruct((n_tiles * ROW_TILE, D_MODEL), F32),
        compiler_params=_cparams(1),
        name="combine",
    )(pos, gates, x1, g_final.reshape(1, D_MODEL), yt)


def _granule_plan(gran, goff):
    n_tiles = gran.shape[0]
    max_gran = (n_tiles * ROW_TILE * TOP_K + (GRANULE - 1) * n_tiles * N_EXPERTS) // GRANULE
    n_blocks = max_gran // GRAN_PER_BLOCK + N_EXPERTS + 1
    n_slots = n_blocks * GRAN_PER_BLOCK
    i32 = jnp.int32
    per_e = jnp.sum(gran, axis=0)
    region = (per_e + GRAN_PER_BLOCK - 1) // GRAN_PER_BLOCK * GRAN_PER_BLOCK
    region_end = jnp.cumsum(region)
    region_start = region_end - region
    g_t = gran.T
    run_start = (region_start[:, None] + jnp.cumsum(g_t, axis=1) - g_t).reshape(-1)
    run_end = run_start + g_t.reshape(-1)
    first_gran = (jnp.arange(n_tiles, dtype=i32)[None, :] * GRAN_PER_TILE + goff.T).reshape(-1)
    shift = first_gran - run_start
    d_shift = jnp.diff(shift, prepend=0)
    d_end = jnp.diff(run_end, prepend=0)
    slot = jnp.arange(n_slots, dtype=i32)
    started = slot[:, None] >= run_start[None, :]
    granule = slot + jnp.sum(jnp.where(started, d_shift[None, :], 0), axis=1)
    valid = slot < jnp.sum(jnp.where(started, d_end[None, :], 0), axis=1)
    spare = n_tiles * GRAN_PER_TILE
    j = slot % GRAN_PER_BLOCK
    odd_blk = (slot // GRAN_PER_BLOCK) % 2
    n_zero = GRAN_PER_TILE - 2 * GRAN_PER_BLOCK
    assert n_zero > 0
    granule_src = jnp.where(valid, granule, spare + 2 * GRAN_PER_BLOCK + j % n_zero).astype(i32)
    granule_dst = jnp.where(valid, granule, spare + odd_blk * GRAN_PER_BLOCK + j).astype(i32)
    blk = jnp.arange(n_blocks, dtype=i32) * GRAN_PER_BLOCK
    blk_e = jnp.minimum(jnp.sum(blk[:, None] >= region_end[None, :], axis=1), N_EXPERTS - 1).astype(i32)
    e = jnp.arange(N_EXPERTS, dtype=i32)
    later = jnp.where((e[None, :] > e[:, None]) & (per_e[None, :] > 0), e[None, :], N_EXPERTS)
    nxt = jnp.min(later, axis=1)
    nxt = jnp.where(nxt == N_EXPERTS, -1, nxt)
    nxt_e = jnp.sum(jnp.where(blk_e[:, None] == e[None, :], nxt[None, :], 0), axis=1).astype(i32)
    n_used = (region_end[-1:] // GRAN_PER_BLOCK).astype(i32)
    return granule_src, granule_dst, blk_e, nxt_e, n_used


def kernel(x_prompt, x_sample, state_s5_re, state_s5_im, state_hgrn, g_mix, w_in, s5_lambda_re,
           s5_lambda_im, s5_log_dt, s5_b_re, s5_b_im, s5_c_re, s5_c_im, s5_d, s5_w_glu, s5_w_gate,
           hgrn_lower_bound, hgrn_norm_g, hgrn_w_out, w_out, g_ffn, router_w, router_b, moe_w1,
           moe_b1, moe_w2, moe_b2, g_final):
    depth = g_mix.shape[0]
    n_p, l_p, _ = x_prompt.shape
    n_s, l_s, _ = x_sample.shape
    assert depth == 1 and n_p == S5_SEQS and l_p % ROW_TILE == 0
    assert l_s == S5_STEPS and n_s % S5_SEQS == 0 and ROW_TILE % l_s == 0
    li = 0
    t_p, t_s = n_p * l_p, n_s * l_s
    n_groups = n_s // S5_SEQS

    lb = jnp.cumsum(jax.nn.softmax(hgrn_lower_bound.astype(F32), axis=0), axis=0)[li].reshape(1, -1)
    s5_tabs = _s5_tables(s5_lambda_re[li], s5_lambda_im[li], s5_log_dt[li], s5_b_re[li],
                         s5_b_im[li], s5_c_re[li], s5_c_im[li])
    wg = jnp.concatenate([s5_w_glu[li], s5_w_gate[li]], axis=1).astype(BF16)
    rw_pad = jnp.pad(router_w[li].astype(F32), ((0, 0), (0, ROUTER_LANES - N_EXPERTS)))
    rw_hi = rw_pad.astype(BF16)
    rw2 = jnp.stack([rw_hi, (rw_pad - rw_hi.astype(F32)).astype(BF16)])

    xp = x_prompt.reshape(t_p, D_MODEL)
    xs = x_sample.reshape(t_s, D_MODEL)
    u, zh, gab = _inproj(xp, xs, g_mix[li], w_in[li].astype(BF16))

    s0_s5 = jnp.concatenate([state_s5_re[li].reshape(n_groups, S5_SEQS, S5_LANES),
                             state_s5_im[li].reshape(n_groups, S5_SEQS, S5_LANES)], axis=1).astype(F32)
    ya, s5_p, s5_s = _s5(u, s0_s5, s5_tabs, s5_d[li].reshape(1, -1), wg,
                         prompt_len=l_p, n_sample_groups=n_groups)
    yb, hg_p, hg_s = _hgrn(zh, lb, hgrn_norm_g[li].reshape(1, -1), hgrn_w_out[li].astype(BF16),
                           state_hgrn[li].astype(F32), n_prompt=n_p, prompt_len=l_p, sample_len=l_s)

    x1, xg, pos, gates, gran, goff = _merge(
        gab, ya, yb, xp, xs, w_out[li].astype(BF16), g_ffn[li].reshape(1, -1), rw2,
        router_b[li].reshape(-1, 1).astype(F32), prompt_len=l_p)

    plan = _granule_plan(gran.reshape(-1, N_EXPERTS), goff.reshape(-1, N_EXPERTS))
    yt = _experts(xg, plan, moe_w1[li], moe_b1[li], moe_w2[li], moe_b2[li])
    n_pt = t_p // ROW_TILE
    y_p = _combine(pos, gates, x1, g_final, yt, tile0=0, n_tiles=n_pt)
    y_s = _combine(pos, gates, x1, g_final, yt, tile0=n_pt, n_tiles=t_s // ROW_TILE)

    def s5_out(st, n, dtype):
        re = st[:, :S5_SEQS].reshape(1, n, S5_GROUPS, S5_STATE).astype(dtype)
        im = st[:, S5_SEQS:].reshape(1, n, S5_GROUPS, S5_STATE).astype(dtype)
        return re, im

    dt = x_prompt.dtype
    p_re, p_im = s5_out(s5_p, n_p, dt)
    s_re, _ = s5_out(s5_s, n_s, state_s5_re.dtype)
    _, s_im = s5_out(s5_s, n_s, state_s5_im.dtype)
    return (y_p.reshape(n_p, l_p, D_MODEL), y_s.reshape(n_s, l_s, D_MODEL),
            p_re, p_im, hg_p[None].astype(dt),
            s_re, s_im, hg_s[None].astype(state_hgrn.dtype))
```

```python
import functools
import math

import jax
import jax.numpy as jnp
from jax import lax
from jax.experimental import pallas as pl
from jax.experimental.pallas import tpu as pltpu

F32 = jnp.float32
BF16 = jnp.bfloat16

D_MODEL = 1024
NORM_EPS = 1e-6
S5_WIDTH = 512
S5_GROUP = 16
S5_GROUPS = 32
S5_STATE = 64
S5_LANES = S5_GROUPS * S5_STATE
S5_DT_CLIP = -1e-4
HG_WIDTH = 512
HG_HEADS = 4
HG_DK = 128
HG_DIAG = 128
N_EXPERTS = 32
TOP_K = 4
ROUTER_LANES = 128
D_EXPERT = 1024
SWIGLU_ALPHA = 1.702
SWIGLU_LIMIT = 7.0

ROW_TILE = 256
INPROJ_ROWS = 512
SCAN_ROWS = 8
S5_SEQS = 4
S5_STEPS = 64
S5_CH_TILE = 128
S5_LANE_TILE = 512
EXPERT_ROWS = 256
GRANULE = 8
TILE_SLOTS = ROW_TILE * TOP_K + N_EXPERTS * GRANULE
GRAN_PER_TILE = TILE_SLOTS // GRANULE
GRAN_PER_BLOCK = EXPERT_ROWS // GRANULE
PACK = 2
U32 = jnp.uint32
VMEM_LIMIT = 56 * 1024 * 1024


def _cparams(n_axes):
    return pltpu.CompilerParams(dimension_semantics=("arbitrary",) * n_axes,
                                vmem_limit_bytes=VMEM_LIMIT)


def _rms(x, g):
    r = lax.rsqrt(jnp.mean(x * x, axis=-1, keepdims=True) + NORM_EPS)
    return x * r * g


def _dot(a, b):
    return jnp.dot(a, b, preferred_element_type=F32)


def _dot_nt(a, b):
    return lax.dot_general(a, b, (((1,), (1,)), ((), ())), preferred_element_type=F32)


def _dot_tn(a, b):
    return lax.dot_general(a, b, (((0,), (0,)), ((), ())), preferred_element_type=F32)


def _two_stream_specs(n_first, rows=ROW_TILE):
    return (pl.BlockSpec((rows, D_MODEL), lambda i: (jnp.minimum(i, n_first - 1), 0)),
            pl.BlockSpec((rows, D_MODEL), lambda i: (jnp.maximum(i - n_first, 0), 0)))


def _inproj_kernel(xp_ref, xs_ref, g_ref, w_ref, u_ref, zh_ref, gab_ref, *, n_first):
    x = jnp.where(pl.program_id(0) < n_first, xp_ref[...], xs_ref[...])
    h = _rms(x, g_ref[...]).astype(BF16)
    u_ref[...] = _dot(h, w_ref[:, :S5_WIDTH])
    zh_ref[...] = _dot(h, w_ref[:, S5_WIDTH:S5_WIDTH + 4 * HG_WIDTH])
    gab_ref[...] = _dot(h, w_ref[:, S5_WIDTH + 4 * HG_WIDTH:]).astype(BF16)


def _inproj(xp, xs, g_mix, w_in_bf):
    assert xp.shape[0] % INPROJ_ROWS == 0 and xs.shape[0] % INPROJ_ROWS == 0
    n_first = xp.shape[0] // INPROJ_ROWS
    t = xp.shape[0] + xs.shape[0]
    ncols = w_in_bf.shape[1]
    row = lambda i: (i, 0)
    fixed = lambda i: (0, 0)
    return pl.pallas_call(
        functools.partial(_inproj_kernel, n_first=n_first),
        grid=(t // INPROJ_ROWS,),
        in_specs=[*_two_stream_specs(n_first, INPROJ_ROWS),
                  pl.BlockSpec((1, D_MODEL), fixed),
                  pl.BlockSpec((D_MODEL, ncols), fixed)],
        out_specs=[pl.BlockSpec((INPROJ_ROWS, S5_WIDTH), row),
                   pl.BlockSpec((INPROJ_ROWS, 4 * HG_WIDTH), row),
                   pl.BlockSpec((INPROJ_ROWS, 2 * D_MODEL), row)],
        out_shape=[jax.ShapeDtypeStruct((t, S5_WIDTH), F32),
                   jax.ShapeDtypeStruct((t, 4 * HG_WIDTH), F32),
                   jax.ShapeDtypeStruct((t, 2 * D_MODEL), BF16)],
        compiler_params=_cparams(1),
        name="inproj",
    )(xp, xs, g_mix.reshape(1, D_MODEL), w_in_bf)


def _s5_kernel(u0_ref, u1_ref, u2_ref, u3_ref, s0_ref, perm_ref, pinv_ref, wb_ref, wc_ref,
               coef_ref, d_ref, wg_ref, ya_ref, sp_out, ss_out, buf0, buf1, ub0, ub1, v_ref,
               *, n_prompt_steps, n_steps):
    i = pl.program_id(0)
    ia = jnp.minimum(i, n_steps - 1)

    @pl.when(i == 0)
    def _():
        v_ref[...] = jnp.zeros_like(v_ref)
        buf1[...] = jnp.zeros_like(buf1)
        ub1[...] = jnp.zeros_like(ub1)

    @pl.when(ia >= n_prompt_steps)
    def _():
        v_ref[...] = s0_ref[0]

    u_refs = (u0_ref, u1_ref, u2_ref, u3_ref)
    consts = (perm_ref, pinv_ref, wb_ref, wc_ref, coef_ref, d_ref, wg_ref)

    @pl.when(i % 2 == 0)
    def _():
        _s5_body(u_refs, consts, ya_ref, v_ref, cur=(buf0, ub0), prev=(buf1, ub1))

    @pl.when(i % 2 == 1)
    def _():
        _s5_body(u_refs, consts, ya_ref, v_ref, cur=(buf1, ub1), prev=(buf0, ub0))

    @pl.when(ia < n_prompt_steps)
    def _():
        sp_out[0] = v_ref[...]

    @pl.when(ia >= n_prompt_steps)
    def _():
        ss_out[0] = v_ref[...]


def _s5_body(u_refs, consts, ya_ref, v_ref, *, cur, prev):
    perm_ref, pinv_ref, wb_ref, wc_ref, coef_ref, d_ref, wg_ref = consts
    buf_a, ub_a = cur
    buf_b, ub_b = prev
    n_tiles = S5_WIDTH // S5_CH_TILE

    u = jnp.concatenate([r[0] for r in u_refs], axis=0)
    ub_a[...] = u
    rows = 2 * u.shape[0]
    up = _dot(perm_ref[...], u.astype(BF16))
    re_half = (lax.broadcasted_iota(jnp.int32, (rows, S5_CH_TILE), 0) & S5_SEQS) == 0
    ws = []
    for n in range(n_tiles):
        ls = slice(n * S5_LANE_TILE, (n + 1) * S5_LANE_TILE)
        un = up[:, n * S5_CH_TILE:(n + 1) * S5_CH_TILE]
        lhs = jnp.concatenate([jnp.where(re_half, un, 0.0), jnp.where(re_half, 0.0, un)], axis=1)
        buf_a[:, ls] = _dot(lhs.astype(BF16), wb_ref[n])
        z = _dot(buf_b[:, ls].astype(BF16), wc_ref[n])
        ws.append(jnp.where(re_half, z[:, :S5_CH_TILE], z[:, S5_CH_TILE:]))
    w_hi, w_lo = _split2(jnp.concatenate(ws, axis=1))

    def scan(v, t0, t1):
        for t in range(t0, t1):
            rs = slice(t * SCAN_ROWS, (t + 1) * SCAN_ROWS)
            v = coef_ref[0] * v + coef_ref[1] * pltpu.roll(v, S5_SEQS, axis=0) + buf_a[rs, :]
            buf_a[rs, :] = v
        return v

    quarter = S5_STEPS // 4
    v = scan(v_ref[...], 0, quarter)
    y = _dot(pinv_ref[...], w_hi) + _dot(pinv_ref[...], w_lo) + d_ref[...] * ub_b[...]
    y = jax.nn.gelu(y).astype(BF16)
    v = scan(v, quarter, 2 * quarter)
    p_glu = _dot(y, wg_ref[:, :D_MODEL])
    v = scan(v, 2 * quarter, 3 * quarter)
    p_gate = _dot(y, wg_ref[:, D_MODEL:])
    v = scan(v, 3 * quarter, S5_STEPS)
    ya_ref[0] = (p_glu * jax.nn.sigmoid(p_gate)).astype(BF16).reshape(S5_SEQS, S5_STEPS, D_MODEL)
    v_ref[...] = v


def _s5(u, s0_sample, tables, d_skip, wg, *, prompt_len, n_sample_groups):
    perm, pinv, wb, wc, coef = tables
    n_prompt_steps = prompt_len // S5_STEPS
    n_steps = n_prompt_steps + n_sample_groups
    sample_base = S5_SEQS * n_prompt_steps
    u3 = u.reshape(-1, S5_STEPS, S5_WIDTH)

    def u_spec(s):
        def imap(i):
            i = jnp.minimum(i, n_steps - 1)
            prompt_blk = s * n_prompt_steps + i
            sample_blk = sample_base + S5_SEQS * (i - n_prompt_steps) + s
            return (jnp.where(i < n_prompt_steps, prompt_blk, sample_blk), 0, 0)
        return pl.BlockSpec((1, S5_STEPS, S5_WIDTH), imap)

    sample_grp = lambda i: (jnp.clip(i - n_prompt_steps, 0, n_sample_groups - 1), 0, 0)
    whole = lambda a: pl.BlockSpec(a.shape, lambda i: (0,) * a.ndim)
    state_blk = (1, 2 * S5_SEQS, S5_LANES)
    rows = S5_SEQS * S5_STEPS
    return pl.pallas_call(
        functools.partial(_s5_kernel, n_prompt_steps=n_prompt_steps, n_steps=n_steps),
        grid=(n_steps + 1,),
        in_specs=[u_spec(0), u_spec(1), u_spec(2), u_spec(3),
                  pl.BlockSpec(state_blk, sample_grp),
                  whole(perm), whole(pinv), whole(wb), whole(wc), whole(coef), whole(d_skip), whole(wg)],
        out_specs=[pl.BlockSpec((1, S5_SEQS, S5_STEPS, D_MODEL),
                                lambda i: (jnp.maximum(i - 1, 0), 0, 0, 0)),
                   pl.BlockSpec(state_blk, lambda i: (0, 0, 0)),
                   pl.BlockSpec(state_blk, sample_grp)],
        out_shape=[jax.ShapeDtypeStruct((n_steps, S5_SEQS, S5_STEPS, D_MODEL), BF16),
                   jax.ShapeDtypeStruct(state_blk, F32),
                   jax.ShapeDtypeStruct((n_sample_groups,) + state_blk[1:], F32)],
        scratch_shapes=[pltpu.VMEM((2 * rows, S5_LANES), F32),
                        pltpu.VMEM((2 * rows, S5_LANES), F32),
                        pltpu.VMEM((rows, S5_WIDTH), F32),
                        pltpu.VMEM((rows, S5_WIDTH), F32),
                        pltpu.VMEM((2 * S5_SEQS, S5_LANES), F32)],
        compiler_params=_cparams(1),
        name="s5",
    )(u3, u3, u3, u3, s0_sample, perm, pinv, wb, wc, coef, d_skip, wg)


def _s5_tables(lam_re, lam_im, log_dt, b_re, b_im, c_re, c_im):
    lr = jnp.minimum(lam_re.astype(F32), S5_DT_CLIP)
    li = lam_im.astype(F32)
    dt = jnp.exp(log_dt.astype(F32))[:, None]
    mag = jnp.exp(lr * dt)
    ar = mag * jnp.cos(li * dt)
    ai = mag * jnp.sin(li * dt)
    den = lr * lr + li * li
    zr = ((ar - 1.0) * lr + ai * li) / den
    zi = (ai * lr - (ar - 1.0) * li) / den
    br = b_re.astype(F32)
    bi = b_im.astype(F32)
    bbr = zr[..., None] * br - zi[..., None] * bi
    bbi = zr[..., None] * bi + zi[..., None] * br

    ar_f, ai_f = ar.reshape(1, -1), ai.reshape(1, -1)
    sign = jnp.where(jnp.arange(2 * S5_SEQS)[:, None] < S5_SEQS, -1.0, 1.0)
    coef = jnp.stack([jnp.broadcast_to(ar_f, (2 * S5_SEQS, S5_LANES)), sign * ai_f]).astype(F32)

    g_per_tile = S5_CH_TILE // S5_GROUP
    n_tiles = S5_GROUPS // g_per_tile
    eye = jnp.eye(g_per_tile, dtype=F32)

    def b_tile(bb):
        x = bb.reshape(n_tiles, g_per_tile, S5_STATE, S5_GROUP)
        x = x[:, :, :, :, None] * eye[None, :, None, None, :]
        return x.transpose(0, 1, 3, 4, 2).reshape(n_tiles, S5_CH_TILE, S5_LANE_TILE)

    wb = jnp.concatenate([b_tile(bbr), b_tile(bbi)], axis=1).astype(BF16)

    def c_tile(cc):
        x = cc.astype(F32).reshape(n_tiles, g_per_tile, S5_GROUP, S5_STATE)
        x = x[:, :, :, :, None] * eye[None, :, None, None, :]
        return x.transpose(0, 1, 3, 4, 2).reshape(n_tiles, S5_LANE_TILE, S5_CH_TILE)

    wc = jnp.concatenate([c_tile(c_re), -c_tile(c_im)], axis=2).astype(BF16)

    r = jnp.arange(2 * S5_SEQS * S5_STEPS)
    src = (r % S5_SEQS) * S5_STEPS + r // (2 * S5_SEQS)
    perm = (src[:, None] == jnp.arange(S5_SEQS * S5_STEPS)[None, :]).astype(BF16)
    return perm, perm.T, wb, wc, coef


def _split2(x):
    hi = x.astype(BF16)
    lo = (x - hi.astype(F32)).astype(BF16)
    return hi, lo


def _split3(x):
    hi = x.astype(BF16)
    r = x - hi.astype(F32)
    mid = r.astype(BF16)
    lo = (r - mid.astype(F32)).astype(BF16)
    return hi, mid, lo


def _hgrn_kernel(zh_ref, lb_ref, gn_ref, wo_ref, s0_ref, yb_ref, sp_out, ss_out,
                 st, gc_ref, att_ref, cross_ref, o_ref, *, n_prompt_tiles, chunks_per_seq,
                 sample_len):
    i = pl.program_id(0)
    refs = (zh_ref, lb_ref, gn_ref, wo_ref, yb_ref, st, gc_ref, att_ref, cross_ref, o_ref)

    @pl.when(i < n_prompt_tiles)
    def _():
        j = i % chunks_per_seq

        @pl.when(j == 0)
        def _():
            st[0] = jnp.zeros((HG_HEADS, HG_DK, HG_DK), F32)

        _hgrn_tile(*refs, n_seg=1, seg_len=ROW_TILE)

        @pl.when(j == chunks_per_seq - 1)
        def _():
            for h in range(HG_HEADS):
                sp_out[0, h] = st[0, h].T

    @pl.when(i >= n_prompt_tiles)
    def _():
        n_seg = ROW_TILE // sample_len
        for seg in range(n_seg):
            for h in range(HG_HEADS):
                st[seg, h] = s0_ref[seg, h].T
        _hgrn_tile(*refs, n_seg=n_seg, seg_len=sample_len)
        for seg in range(n_seg):
            for h in range(HG_HEADS):
                ss_out[seg, h] = st[seg, h].T


def _hgrn_tile(zh_ref, lb_ref, gn_ref, wo_ref, yb_ref, st, gc_ref, att_ref, cross_ref, o_ref,
               *, n_seg, seg_len):
    rows = n_seg * seg_len
    zq = zh_ref[:, 0:HG_WIDTH]
    zf = zh_ref[:, HG_WIDTH:2 * HG_WIDTH]
    v = zh_ref[:, 2 * HG_WIDTH:3 * HG_WIDTH]
    lb = lb_ref[...]
    q = zq * jax.nn.sigmoid(zq)
    f = lb + (1.0 - lb) * jax.nn.sigmoid(zf)
    k = 1.0 - f
    g = jnp.log(f)
    vb = v.astype(BF16)

    ri = lax.broadcasted_iota(jnp.int32, (rows, rows), 0)
    ci = lax.broadcasted_iota(jnp.int32, (rows, rows), 1)
    seg_shift = int(math.log2(seg_len))
    same_seg = (ri >> seg_shift) == (ci >> seg_shift)

    tri = jnp.where(same_seg & (ci <= ri), 1.0, 0.0).astype(BF16)
    g3 = jnp.concatenate(_split3(g), axis=1)
    gcs = _dot(tri, g3)
    gc = gcs[:, :HG_WIDTH] + gcs[:, HG_WIDTH:2 * HG_WIDTH] + gcs[:, 2 * HG_WIDTH:]
    gc_ref[...] = gc

    n_diag = rows // HG_DIAG
    dr = lax.broadcasted_iota(jnp.int32, (HG_DIAG, HG_DIAG), 0)
    dc = lax.broadcasted_iota(jnp.int32, (HG_DIAG, HG_DIAG), 1)
    dxor = jnp.where(dc < dr, dr ^ dc, 0)
    row_id = lax.broadcasted_iota(jnp.int32, (rows, HG_WIDTH), 0)
    qb = q.astype(BF16)
    kb = k.astype(BF16)
    for h in range(HG_HEADS):
        hs = slice(h * HG_DK, (h + 1) * HG_DK)
        for d in range(n_diag):
            ds_ = slice(d * HG_DIAG, (d + 1) * HG_DIAG)
            att_ref[h, d] = jnp.where(dr == dc, _dot_nt(qb[ds_, hs], kb[ds_, hs]), 0.0)

    g2 = jnp.concatenate(_split2(g), axis=1)
    level = seg_len // 2
    while level >= 1:
        if 2 * level >= SCAN_ROWS:
            mids, picks = [], []
            for b in range(rows // (2 * level)):
                lo = b * 2 * level
                mids.append(jnp.broadcast_to(gc_ref[lo + level - 1:lo + level, :], (2 * level, HG_WIDTH)))
                picks += [k[lo:lo + level], q[lo + level:lo + 2 * level]]
            gmid = mids[0] if len(mids) == 1 else jnp.concatenate(mids, axis=0)
            qk = jnp.concatenate(picks, axis=0)
            dexp = -jnp.abs(gc - gmid)
        else:
            mid = (ri & ~(2 * level - 1)) + (level - 1)
            upper = (ri & level) != 0
            w = (ci > jnp.where(upper, mid, ri)) & (ci <= jnp.where(upper, ri, mid))
            dsum = _dot(jnp.where(w, 1.0, 0.0).astype(BF16), g2)
            dexp = dsum[:, :HG_WIDTH] + dsum[:, HG_WIDTH:]
            qk = jnp.where((row_id & level) != 0, q, k)
        m = (qk * jnp.exp(dexp)).astype(BF16)
        if level >= HG_DIAG:
            for h in range(HG_HEADS):
                hs = slice(h * HG_DK, (h + 1) * HG_DK)
                cross_ref[h] = _dot_nt(m[level:, hs], m[:level, hs])
        else:
            sel = (dxor >> int(math.log2(level))) == 1
            for h in range(HG_HEADS):
                hs = slice(h * HG_DK, (h + 1) * HG_DK)
                for d in range(n_diag):
                    ds_ = slice(d * HG_DIAG, (d + 1) * HG_DIAG)
                    att_ref[h, d] = jnp.where(sel, _dot_nt(m[ds_, hs], m[ds_, hs]), att_ref[h, d])
        level //= 2

    qg = (q * jnp.exp(gc)).astype(BF16)
    for h in range(HG_HEADS):
        hs = slice(h * HG_DK, (h + 1) * HG_DK)
        for d in range(n_diag):
            ds_ = slice(d * HG_DIAG, (d + 1) * HG_DIAG)
            o_ref[ds_, hs] = _dot(att_ref[h, d].astype(BF16), vb[ds_, hs])
        if seg_len > HG_DIAG:
            o_ref[HG_DIAG:, hs] += _dot(cross_ref[h].astype(BF16), vb[:HG_DIAG, hs])

    for seg in range(n_seg):
        rs = slice(seg * seg_len, (seg + 1) * seg_len)
        last = (seg + 1) * seg_len - 1
        g_last = gc_ref[last:last + 1, :]
        kd = (k[rs] * jnp.exp(g_last - gc[rs])).astype(BF16)
        decay = jnp.exp(g_last)
        for h in range(HG_HEADS):
            hs = slice(h * HG_DK, (h + 1) * HG_DK)
            s_t = st[seg, h]
            o_ref[rs, hs] += _dot_nt(qg[rs, hs], s_t.astype(BF16))
            st[seg, h] = s_t * decay[:, hs] + _dot_tn(vb[rs, hs], kd[:, hs])

    zg = zh_ref[:, 3 * HG_WIDTH:]
    gate = zg * jax.nn.sigmoid(zg)
    outs = []
    for h in range(HG_HEADS):
        hs = slice(h * HG_DK, (h + 1) * HG_DK)
        outs.append(_rms(o_ref[:, hs], gn_ref[...]) * gate[:, hs])
    yb_ref[...] = _dot(jnp.concatenate(outs, axis=1).astype(BF16), wo_ref[...]).astype(BF16)


def _hgrn(zh, lb, gn, wo, s0_sample, *, n_prompt, prompt_len, sample_len):
    t = zh.shape[0]
    chunks_per_seq = prompt_len // ROW_TILE
    n_prompt_tiles = n_prompt * chunks_per_seq
    n_seg = ROW_TILE // sample_len
    n_sample = s0_sample.shape[0]
    row = lambda i: (i, 0)
    fixed = lambda i: (0, 0)
    state = (HG_HEADS, HG_DK, HG_DK)
    prompt_seq = lambda i: (jnp.minimum(i // chunks_per_seq, n_prompt - 1), 0, 0, 0)
    sample_grp = lambda i: (jnp.maximum(i - n_prompt_tiles, 0), 0, 0, 0)
    kern = functools.partial(_hgrn_kernel, n_prompt_tiles=n_prompt_tiles,
                             chunks_per_seq=chunks_per_seq, sample_len=sample_len)
    return pl.pallas_call(
        kern,
        grid=(t // ROW_TILE,),
        in_specs=[pl.BlockSpec((ROW_TILE, 4 * HG_WIDTH), row),
                  pl.BlockSpec((1, HG_WIDTH), fixed),
                  pl.BlockSpec((1, HG_DK), fixed),
                  pl.BlockSpec(wo.shape, fixed),
                  pl.BlockSpec((n_seg,) + state, sample_grp)],
        out_specs=[pl.BlockSpec((ROW_TILE, D_MODEL), row),
                   pl.BlockSpec((1,) + state, prompt_seq),
                   pl.BlockSpec((n_seg,) + state, sample_grp)],
        out_shape=[jax.ShapeDtypeStruct((t, D_MODEL), BF16),
                   jax.ShapeDtypeStruct((n_prompt,) + state, F32),
                   jax.ShapeDtypeStruct((n_sample,) + state, F32)],
        scratch_shapes=[pltpu.VMEM((n_seg,) + state, F32),
                        pltpu.VMEM((ROW_TILE, HG_WIDTH), F32),
                        pltpu.VMEM((HG_HEADS, ROW_TILE // HG_DIAG, HG_DIAG, HG_DIAG), F32),
                        pltpu.VMEM((HG_HEADS, HG_DIAG, HG_DIAG), F32),
                        pltpu.VMEM((ROW_TILE, HG_WIDTH), F32)],
        compiler_params=_cparams(1),
        name="hgrn2",
    )(zh, lb, gn, wo, s0_sample)


def _merge_kernel(gab_ref, ya0_ref, ya1_ref, ya2_ref, ya3_ref, yb_ref, xp_ref, xs_ref, wo_ref,
                  gf_ref, rw_ref, rb_ref, x1_ref, xg_ref, pos_ref, gate_ref, cnt_ref, off_ref,
                  hbuf, lbuf, *, n_first, n_tiles):
    i = pl.program_id(0)
    cur = i % 2
    prev = 1 - cur

    @pl.when(i == 0)
    def _():
        hbuf[1] = jnp.zeros(hbuf.shape[1:], hbuf.dtype)
        lbuf[1] = jnp.zeros(lbuf.shape[1:], lbuf.dtype)

    logits_tok = lbuf[prev]

    ga = gab_ref[:, :D_MODEL].astype(F32)
    gb = gab_ref[:, D_MODEL:].astype(F32)
    ya = jnp.concatenate([r[0] for r in (ya0_ref, ya1_ref, ya2_ref, ya3_ref)], axis=0).astype(F32)
    mixed = (jax.nn.sigmoid(ga) * ya + jax.nn.sigmoid(gb) * yb_ref[...].astype(F32)).astype(BF16)

    rows = logits_tok.shape[0]
    logits = logits_tok.T[:N_EXPERTS] + rb_ref[...]
    eid = lax.broadcasted_iota(jnp.int32, (N_EXPERTS, rows), 0).astype(F32)
    lg = logits
    sel = jnp.zeros((N_EXPERTS, rows), F32)
    vals, idxs = [], []
    for _ in range(TOP_K):
        m = jnp.max(lg, axis=0, keepdims=True)
        idx = jnp.min(jnp.where(lg == m, eid, float(N_EXPERTS)), axis=0, keepdims=True)
        hit = eid == idx
        vals.append(m)
        idxs.append(idx)
        lg = jnp.where(hit, -jnp.inf, lg)
        sel = jnp.where(hit, 1.0, sel)

    es = [jnp.exp(vv - vals[0]) for vv in vals]
    tot = es[0] + es[1] + es[2] + es[3]

    ri = lax.broadcasted_iota(jnp.int32, (rows, rows), 0)
    ci = lax.broadcasted_iota(jnp.int32, (rows, rows), 1)
    earlier = jnp.where(ri < ci, 1.0, 0.0).astype(BF16)
    rank = _dot(sel.astype(BF16), earlier)
    gran = jnp.floor((jnp.sum(sel, axis=1, keepdims=True) + (GRANULE - 1)) * (1.0 / GRANULE))
    ei = lax.broadcasted_iota(jnp.int32, (N_EXPERTS, N_EXPERTS), 0)
    ej = lax.broadcasted_iota(jnp.int32, (N_EXPERTS, N_EXPERTS), 1)
    gran_b = jnp.broadcast_to(gran, (N_EXPERTS, rows))
    off = _dot(jnp.where(ej < ei, 1.0, 0.0).astype(BF16), gran_b.astype(BF16))
    slot_of = rank + off * float(GRANULE)
    poss = [jnp.sum(jnp.where(eid == idx, slot_of, 0.0), axis=0, keepdims=True) for idx in idxs]

    per_tok = jnp.concatenate(poss + [e / tot for e in es], axis=0).T

    x = jnp.where(jnp.minimum(i, n_tiles - 1) < n_first, xp_ref[...], xs_ref[...])
    x1 = x + _dot(mixed, wo_ref[...])
    x1_ref[...] = x1

    slot_iota = lax.broadcasted_iota(jnp.int32, (rows, TILE_SLOTS), 1).astype(F32)
    onehot = jnp.zeros((rows, TILE_SLOTS), F32)
    for kk in range(TOP_K):
        onehot = jnp.where(slot_iota == per_tok[:, kk:kk + 1], 1.0, onehot)
    sorted_rows = _dot_tn(onehot.astype(BF16), hbuf[prev])
    xg_ref[...] = pltpu.bitcast(jnp.where(i > n_tiles, 0.0, sorted_rows).astype(BF16), U32)

    a_hi, a_lo = _split2(_rms(x1, gf_ref[...]))
    hbuf[cur] = a_hi
    lbuf[cur] = _dot(a_hi, rw_ref[0]) + _dot(a_hi, rw_ref[1]) + _dot(a_lo, rw_ref[0])

    pos_ref[...] = per_tok[:, :TOP_K].astype(jnp.int32)
    gate_ref[...] = per_tok[:, TOP_K:]
    cnt_ref[0] = gran.astype(jnp.int32)
    off_ref[0] = off[:, :1].astype(jnp.int32)


def _merge(gab, ya_steps, yb, xp, xs, wo, g_ffn, rw2, rb, *, prompt_len):
    n_first = xp.shape[0] // ROW_TILE
    t = xp.shape[0] + xs.shape[0]
    n_tiles = t // ROW_TILE
    chunks_per_seq = prompt_len // ROW_TILE
    quarters = ROW_TILE // S5_STEPS
    ya3 = ya_steps.reshape(-1, S5_STEPS, D_MODEL)

    def ya_spec(q):
        def imap(i):
            i = jnp.minimum(i, n_tiles - 1)
            seq, chunk = i // chunks_per_seq, i % chunks_per_seq
            prompt_blk = (chunk * quarters + q) * S5_SEQS + seq
            return (jnp.where(i < n_first, prompt_blk, i * quarters + q), 0, 0)
        return pl.BlockSpec((1, S5_STEPS, D_MODEL), imap)

    tile = lambda i: jnp.minimum(i, n_tiles - 1)
    prev_tile = lambda i: jnp.clip(i - 1, 0, n_tiles - 1)
    row = lambda i: (tile(i), 0)
    prev_row = lambda i: (prev_tile(i), 0)
    fixed = lambda i: (0, 0)
    per_prev_tile = lambda i: (prev_tile(i), 0, 0)
    xp_spec, xs_spec = _two_stream_specs(n_first)
    return pl.pallas_call(
        functools.partial(_merge_kernel, n_first=n_first, n_tiles=n_tiles),
        grid=(n_tiles + 2,),
        in_specs=[pl.BlockSpec((ROW_TILE, 2 * D_MODEL), row),
                  ya_spec(0), ya_spec(1), ya_spec(2), ya_spec(3),
                  pl.BlockSpec((ROW_TILE, D_MODEL), row),
                  pl.BlockSpec(xp_spec.block_shape, lambda i: xp_spec.index_map(tile(i))),
                  pl.BlockSpec(xs_spec.block_shape, lambda i: xs_spec.index_map(tile(i))),
                  pl.BlockSpec((D_MODEL, D_MODEL), fixed),
                  pl.BlockSpec((1, D_MODEL), fixed),
                  pl.BlockSpec((2, D_MODEL, ROUTER_LANES), lambda i: (0, 0, 0)),
                  pl.BlockSpec((N_EXPERTS, 1), fixed)],
        out_specs=[pl.BlockSpec((ROW_TILE, D_MODEL), row),
                   pl.BlockSpec((TILE_SLOTS // PACK, D_MODEL), lambda i: (jnp.maximum(i - 1, 0), 0)),
                   pl.BlockSpec((ROW_TILE, TOP_K), prev_row),
                   pl.BlockSpec((ROW_TILE, TOP_K), prev_row),
                   pl.BlockSpec((1, N_EXPERTS, 1), per_prev_tile),
                   pl.BlockSpec((1, N_EXPERTS, 1), per_prev_tile)],
        scratch_shapes=[pltpu.VMEM((2, ROW_TILE, D_MODEL), BF16),
                        pltpu.VMEM((2, ROW_TILE, ROUTER_LANES), F32)],
        out_shape=[jax.ShapeDtypeStruct((t, D_MODEL), F32),
                   jax.ShapeDtypeStruct(((n_tiles + 1) * TILE_SLOTS // PACK, D_MODEL), U32),
                   jax.ShapeDtypeStruct((t, TOP_K), jnp.int32),
                   jax.ShapeDtypeStruct((t, TOP_K), F32),
                   jax.ShapeDtypeStruct((n_tiles, N_EXPERTS, 1), jnp.int32),
                   jax.ShapeDtypeStruct((n_tiles, N_EXPERTS, 1), jnp.int32)],
        compiler_params=_cparams(1),
        name="merge_router",
    )(gab, ya3, ya3, ya3, ya3, yb, xp, xs, wo, g_ffn, rw2, rb)


def _expert_kernel(blk_e, nxt_e, n_used, src0_ref, srcn_ref, dst_ref, xg_hbm, w1_hbm, b1_ref, w2_hbm,
                   b2_ref, yt_hbm, xbuf, obuf, gsem, ssem, w1s, w2s, wsem, w1b, w2b):
    b = pl.program_id(0)
    nu = n_used[0]

    gran_rows = GRANULE // PACK
    blk_rows = EXPERT_ROWS // PACK

    def granule_rows(g):
        return pl.ds(pl.multiple_of(g * gran_rows, gran_rows), gran_rows)

    def block_rows(j):
        return pl.ds(j * gran_rows, gran_rows)

    def start_gather(src_ref, slot):
        for j in range(GRAN_PER_BLOCK):
            pltpu.make_async_copy(xg_hbm.at[granule_rows(src_ref[0, 0, j])],
                                  xbuf.at[slot, block_rows(j)], gsem.at[slot]).start()

    def start_scatter(slot):
        for j in range(GRAN_PER_BLOCK):
            pltpu.make_async_copy(obuf.at[slot, block_rows(j)],
                                  yt_hbm.at[granule_rows(dst_ref[0, 0, j])], ssem.at[slot]).start()

    def wait_gather(slot):
        pltpu.make_async_copy(xg_hbm.at[pl.ds(0, blk_rows)], xbuf.at[slot], gsem.at[slot]).wait()

    def wait_scatter(slot):
        pltpu.make_async_copy(obuf.at[slot], yt_hbm.at[pl.ds(0, blk_rows)], ssem.at[slot]).wait()

    def weight_copies(e):
        return (pltpu.make_async_copy(w1_hbm.at[e], w1s, wsem.at[0]),
                pltpu.make_async_copy(w2_hbm.at[e], w2s, wsem.at[1]))

    first = b == 0
    new_expert = first | (blk_e[b] != blk_e[jnp.maximum(b - 1, 0)])

    @pl.when(first)
    def _():
        start_gather(src0_ref, 0)
        for c in weight_copies(blk_e[0]):
            c.start(priority=1)

    @pl.when((b >= 2) & (b <= nu))
    def _():
        wait_scatter(b % 2)

    @pl.when(b == nu)
    def _():
        wait_gather(b % 2)
        wait_scatter((b - 1) % 2)

    @pl.when(b < nu)
    def _():
        slot = b % 2
        wait_gather(slot)

        @pl.when(new_expert)
        def _():
            for c in weight_copies(blk_e[b]):
                c.wait()
            w1b[...] = w1s[...].astype(BF16)
            w2b[...] = w2s[...].astype(BF16)

            @pl.when(nxt_e[b] >= 0)
            def _():
                for c in weight_copies(nxt_e[b]):
                    c.start(priority=1)

        start_gather(srcn_ref, 1 - slot)
        x = pltpu.bitcast(xbuf[slot], BF16)
        u = _dot(x, w1b[...]) + b1_ref[0]
        a = jnp.minimum(u[:, :D_EXPERT], SWIGLU_LIMIT)
        lin = jnp.clip(u[:, D_EXPERT:], -SWIGLU_LIMIT, SWIGLU_LIMIT)
        hmid = a * jax.nn.sigmoid(SWIGLU_ALPHA * a) * (lin + 1.0)
        y = _dot(hmid.astype(BF16), w2b[...]) + b2_ref[0]
        obuf[slot] = pltpu.bitcast(y.astype(BF16), U32)
        start_scatter(slot)


def _experts(xg, plan, w1, b1, w2, b2):
    granule_src, granule_dst, blk_e, nxt_e, n_used = plan
    n_blocks = blk_e.shape[0]
    src3 = granule_src.reshape(n_blocks, 1, GRAN_PER_BLOCK)
    dst3 = granule_dst.reshape(n_blocks, 1, GRAN_PER_BLOCK)
    smem_blk = lambda imap: pl.BlockSpec((1, 1, GRAN_PER_BLOCK), imap, memory_space=pltpu.SMEM)
    live = lambda b, nu: jnp.minimum(b, nu[0])
    by_expert = lambda b, be, nx, nu: (be[live(b, nu)], 0, 0)
    grid_spec = pltpu.PrefetchScalarGridSpec(
        num_scalar_prefetch=3,
        grid=(n_blocks,),
        in_specs=[smem_blk(lambda b, be, nx, nu: (0, 0, 0)),
                  smem_blk(lambda b, be, nx, nu: (jnp.minimum(live(b, nu) + 1, n_blocks - 1), 0, 0)),
                  smem_blk(lambda b, be, nx, nu: (live(b, nu), 0, 0)),
                  pl.BlockSpec(memory_space=pl.ANY),
                  pl.BlockSpec(memory_space=pl.ANY),
                  pl.BlockSpec((1, 1, 2 * D_EXPERT), by_expert),
                  pl.BlockSpec(memory_space=pl.ANY),
                  pl.BlockSpec((1, 1, D_MODEL), by_expert)],
        out_specs=pl.BlockSpec(memory_space=pl.ANY),
        scratch_shapes=[pltpu.VMEM((2, EXPERT_ROWS // PACK, D_MODEL), U32),
                        pltpu.VMEM((2, EXPERT_ROWS // PACK, D_MODEL), U32),
                        pltpu.SemaphoreType.DMA((2,)),
                        pltpu.SemaphoreType.DMA((2,)),
                        pltpu.VMEM((D_MODEL, 2 * D_EXPERT), F32),
                        pltpu.VMEM((D_EXPERT, D_MODEL), F32),
                        pltpu.SemaphoreType.DMA((2,)),
                        pltpu.VMEM((D_MODEL, 2 * D_EXPERT), BF16),
                        pltpu.VMEM((D_EXPERT, D_MODEL), BF16)],
    )
    return pl.pallas_call(
        _expert_kernel,
        grid_spec=grid_spec,
        out_shape=jax.ShapeDtypeStruct(xg.shape, xg.dtype),
        input_output_aliases={6: 0},
        compiler_params=_cparams(1),
        name="experts",
    )(blk_e, nxt_e, n_used, src3, src3, dst3, xg, w1, b1.reshape(N_EXPERTS, 1, -1), w2,
      b2.reshape(N_EXPERTS, 1, -1))


def _combine_kernel(pos_ref, gate_ref, x1_ref, gf_ref, yt_ref, y_ref):
    rows = pos_ref.shape[0]
    slot_iota = lax.broadcasted_iota(jnp.int32, (rows, TILE_SLOTS), 1)
    pos = pos_ref[...]
    gates = gate_ref[...]
    weights = jnp.zeros((rows, TILE_SLOTS), F32)
    for kk in range(TOP_K):
        weights = jnp.where(slot_iota == pos[:, kk:kk + 1], gates[:, kk:kk + 1], weights)
    moe = _dot(weights.astype(BF16), pltpu.bitcast(yt_ref[...], BF16))
    y_ref[...] = _rms(x1_ref[...] + moe, gf_ref[...])


def _combine(pos, gates, x1, g_final, yt, *, tile0, n_tiles):
    row = lambda i: (tile0 + i, 0)
    return pl.pallas_call(
        _combine_kernel,
        grid=(n_tiles,),
        in_specs=[pl.BlockSpec((ROW_TILE, TOP_K), row),
                  pl.BlockSpec((ROW_TILE, TOP_K), row),
                  pl.BlockSpec((ROW_TILE, D_MODEL), row),
                  pl.BlockSpec((1, D_MODEL), lambda i: (0, 0)),
                  pl.BlockSpec((TILE_SLOTS // PACK, D_MODEL), row)],
        out_specs=pl.BlockSpec((ROW_TILE, D_MODEL), lambda i: (i, 0)),
        out_shape=jax.ShapeDtypeStruct((n_tiles * ROW_TILE, D_MODEL), F32),
        compiler_params=_cparams(1),
        name="combine",
    )(pos, gates, x1, g_final.reshape(1, D_MODEL), yt)


def _granule_plan(gran, goff):
    n_tiles = gran.shape[0]
    max_gran = (n_tiles * ROW_TILE * TOP_K + (GRANULE - 1) * n_tiles * N_EXPERTS) // GRANULE
    n_blocks = max_gran // GRAN_PER_BLOCK + N_EXPERTS + 1
    n_slots = n_blocks * GRAN_PER_BLOCK
    i32 = jnp.int32
    per_e = jnp.sum(gran, axis=0)
    region = (per_e + GRAN_PER_BLOCK - 1) // GRAN_PER_BLOCK * GRAN_PER_BLOCK
    region_end = jnp.cumsum(region)
    region_start = region_end - region
    blk = jnp.arange(n_blocks, dtype=i32) * GRAN_PER_BLOCK
    blk_e = jnp.minimum(jnp.sum(blk[:, None] >= region_end[None, :], axis=1), N_EXPERTS - 1).astype(i32)
    e = jnp.arange(N_EXPERTS, dtype=i32)
    g_t = gran.T
    start_t = jnp.cumsum(g_t, axis=1) - g_t
    first_t = jnp.arange(n_tiles, dtype=i32)[None, :] * GRAN_PER_TILE + goff.T
    d_shift_t = jnp.diff(first_t - start_t, axis=1, prepend=0)
    d_end_t = jnp.diff(start_t + g_t, axis=1, prepend=0)
    is_e = blk_e[:, None] == e[None, :]
    pick = lambda tab: jnp.sum(jnp.where(is_e[:, :, None], tab[None, :, :], 0), axis=1)
    run_start, d_shift, d_end = pick(start_t), pick(d_shift_t), pick(d_end_t)
    base = jnp.sum(jnp.where(is_e, region_start[None, :], 0), axis=1)
    j = jnp.arange(GRAN_PER_BLOCK, dtype=i32)[None, :]
    local = blk[:, None] + j - base[:, None]
    started = local[:, :, None] >= run_start[:, None, :]
    granule = local + jnp.sum(jnp.where(started, d_shift[:, None, :], 0), axis=2)
    valid = local < jnp.sum(jnp.where(started, d_end[:, None, :], 0), axis=2)
    spare = n_tiles * GRAN_PER_TILE
    odd_blk = (jnp.arange(n_blocks, dtype=i32) % 2)[:, None]
    n_zero = GRAN_PER_TILE - 2 * GRAN_PER_BLOCK
    assert n_zero > 0
    granule_src = jnp.where(valid, granule, spare + 2 * GRAN_PER_BLOCK + j % n_zero).astype(i32)
    granule_dst = jnp.where(valid, granule, spare + odd_blk * GRAN_PER_BLOCK + j).astype(i32)
    later = jnp.where((e[None, :] > e[:, None]) & (per_e[None, :] > 0), e[None, :], N_EXPERTS)
    nxt = jnp.min(later, axis=1)
    nxt = jnp.where(nxt == N_EXPERTS, -1, nxt)
    nxt_e = jnp.sum(jnp.where(blk_e[:, None] == e[None, :], nxt[None, :], 0), axis=1).astype(i32)
    n_used = (region_end[-1:] // GRAN_PER_BLOCK).astype(i32)
    return granule_src, granule_dst, blk_e, nxt_e, n_used


def kernel(x_prompt, x_sample, state_s5_re, state_s5_im, state_hgrn, g_mix, w_in, s5_lambda_re,
           s5_lambda_im, s5_log_dt, s5_b_re, s5_b_im, s5_c_re, s5_c_im, s5_d, s5_w_glu, s5_w_gate,
           hgrn_lower_bound, hgrn_norm_g, hgrn_w_out, w_out, g_ffn, router_w, router_b, moe_w1,
           moe_b1, moe_w2, moe_b2, g_final):
    depth = g_mix.shape[0]
    n_p, l_p, _ = x_prompt.shape
    n_s, l_s, _ = x_sample.shape
    assert depth == 1 and n_p == S5_SEQS and l_p % ROW_TILE == 0
    assert l_s == S5_STEPS and n_s % S5_SEQS == 0 and ROW_TILE % l_s == 0
    li = 0
    t_p, t_s = n_p * l_p, n_s * l_s
    n_groups = n_s // S5_SEQS

    lb = jnp.cumsum(jax.nn.softmax(hgrn_lower_bound.astype(F32), axis=0), axis=0)[li].reshape(1, -1)
    s5_tabs = _s5_tables(s5_lambda_re[li], s5_lambda_im[li], s5_log_dt[li], s5_b_re[li],
                         s5_b_im[li], s5_c_re[li], s5_c_im[li])
    wg = jnp.concatenate([s5_w_glu[li], s5_w_gate[li]], axis=1).astype(BF16)
    rw_pad = jnp.pad(router_w[li].astype(F32), ((0, 0), (0, ROUTER_LANES - N_EXPERTS)))
    rw_hi = rw_pad.astype(BF16)
    rw2 = jnp.stack([rw_hi, (rw_pad - rw_hi.astype(F32)).astype(BF16)])

    xp = x_prompt.reshape(t_p, D_MODEL)
    xs = x_sample.reshape(t_s, D_MODEL)
    u, zh, gab = _inproj(xp, xs, g_mix[li], w_in[li].astype(BF16))

    s0_s5 = jnp.concatenate([state_s5_re[li].reshape(n_groups, S5_SEQS, S5_LANES),
                             state_s5_im[li].reshape(n_groups, S5_SEQS, S5_LANES)], axis=1).astype(F32)
    ya, s5_p, s5_s = _s5(u, s0_s5, s5_tabs, s5_d[li].reshape(1, -1), wg,
                         prompt_len=l_p, n_sample_groups=n_groups)
    yb, hg_p, hg_s = _hgrn(zh, lb, hgrn_norm_g[li].reshape(1, -1), hgrn_w_out[li].astype(BF16),
                           state_hgrn[li].astype(F32), n_prompt=n_p, prompt_len=l_p, sample_len=l_s)

    x1, xg, pos, gates, gran, goff = _merge(
        gab, ya, yb, xp, xs, w_out[li].astype(BF16), g_ffn[li].reshape(1, -1), rw2,
        router_b[li].reshape(-1, 1).astype(F32), prompt_len=l_p)

    plan = _granule_plan(gran.reshape(-1, N_EXPERTS), goff.reshape(-1, N_EXPERTS))
    yt = _experts(xg, plan, moe_w1[li], moe_b1[li], moe_w2[li], moe_b2[li])
    n_pt = t_p // ROW_TILE
    y_p = _combine(pos, gates, x1, g_final, yt, tile0=0, n_tiles=n_pt)
    y_s = _combine(pos, gates, x1, g_final, yt, tile0=n_pt, n_tiles=t_s // ROW_TILE)

    def s5_out(st, n, dtype):
        re = st[:, :S5_SEQS].reshape(1, n, S5_GROUPS, S5_STATE).astype(dtype)
        im = st[:, S5_SEQS:].reshape(1, n, S5_GROUPS, S5_STATE).astype(dtype)
        return re, im

    dt = x_prompt.dtype
    p_re, p_im = s5_out(s5_p, n_p, dt)
    s_re, _ = s5_out(s5_s, n_s, state_s5_re.dtype)
    _, s_im = s5_out(s5_s, n_s, state_s5_im.dtype)
    return (y_p.reshape(n_p, l_p, D_MODEL), y_s.reshape(n_s, l_s, D_MODEL),
            p_re, p_im, hg_p[None].astype(dt),
            s_re, s_im, hg_s[None].astype(state_hgrn.dtype))
```
